```python
import math
import jax, jax.numpy as jnp
from jax import lax
import numpy as np

D_MODEL = 1024
BATCH = 4
SEQ = 8192
DEPTH = 2

EPS = 1e-6
N_BRANCHES = 3

POOL_WINDOWS = (2, 4, 8, 16)
POOL_GROUP = 128
POOL_WIDTH = POOL_GROUP * len(POOL_WINDOWS)

SSM_GROUP = 16
SSM_GROUPS = 32
SSM_WIDTH = SSM_GROUP * SSM_GROUPS
SSM_STATE = 64

HEAD_DIM = 64
HEADS_PER_GROUP = 4
ATTN_PATTERNS = ((128, 1), (512, 4), (2048, 16))
ATTN_HEADS = HEADS_PER_GROUP * len(ATTN_PATTERNS)
ATTN_WIDTH = ATTN_HEADS * HEAD_DIM
ATTN_OUT = HEADS_PER_GROUP * HEAD_DIM
ATTN_BLOCK = 128
ROPE_THETA = 10000.0

IN_SPLITS = (POOL_WIDTH, SSM_WIDTH, ATTN_WIDTH, ATTN_WIDTH, ATTN_WIDTH, N_BRANCHES * D_MODEL)
IN_COLS = sum(IN_SPLITS)

PEER_HEADS = 8
PEER_NKEYS = 128
PEER_EXPERTS = PEER_NKEYS * PEER_NKEYS
PEER_QDIM = 256
PEER_TOPK = 16
PEER_CHUNK = 128

kernel_name = "hybrid_pool_s5_dilated_attn_peer"

F32 = jnp.float32


def rmsnorm(x, g):
    xf = x.astype(F32)
    y = xf * lax.rsqrt(jnp.mean(xf * xf, axis=-1, keepdims=True) + EPS)
    return (y * g.astype(F32)).astype(x.dtype)


def pool_branch(u, w, scale):
    B, L, _ = u.shape
    uf = u.astype(F32)
    cs = jnp.concatenate([jnp.zeros((B, 1, POOL_WIDTH), F32), jnp.cumsum(uf, axis=1)], axis=1)
    t = jnp.arange(L)
    outs = []
    for gi, win in enumerate(POOL_WINDOWS):
        sl = slice(gi * POOL_GROUP, (gi + 1) * POOL_GROUP)
        start = jnp.maximum(t + 1 - win, 0)
        count = (t + 1 - start).astype(F32)[None, :, None]
        outs.append((cs[:, 1:, sl] - cs[:, start, sl]) / count - uf[:, :, sl])
    pooled = jnp.stack(outs, axis=2)
    mixed = jnp.einsum('blgc,gcd->blgd', pooled, w.astype(F32)).reshape(B, L, POOL_WIDTH)
    return (mixed * scale.astype(F32)).astype(u.dtype)


def diag_linear_scan(a_re, a_im, b_re, b_im):
    def combine(left, right):
        lar, lai, lbr, lbi = left
        rar, rai, rbr, rbi = right
        return (lar * rar - lai * rai,
                lar * rai + lai * rar,
                rar * lbr - rai * lbi + rbr,
                rar * lbi + rai * lbr + rbi)
    _, _, s_re, s_im = lax.associative_scan(combine, (a_re, a_im, b_re, b_im), axis=0)
    return s_re, s_im


def ssm_branch(u, a_re, a_im, log_dt, b_re, b_im, c_re, c_im, d_skip, w_glu):
    B, L, _ = u.shape
    uf = u.astype(F32).reshape(B, L, SSM_GROUPS, SSM_GROUP)
    dt = jnp.exp(log_dt.astype(F32))[:, None]
    ar, ai = a_re.astype(F32), a_im.astype(F32)
    decay = jnp.exp(ar * dt)
    abar_re = decay * jnp.cos(ai * dt)
    abar_im = decay * jnp.sin(ai * dt)
    den = ar * ar + ai * ai
    nr = abar_re - 1.0
    k_re = (nr * ar + abar_im * ai) / den
    k_im = (abar_im * ar - nr * ai) / den
    br, bi = b_re.astype(F32), b_im.astype(F32)
    bbar_re = k_re[:, :, None] * br - k_im[:, :, None] * bi
    bbar_im = k_re[:, :, None] * bi + k_im[:, :, None] * br
    x_re = jnp.einsum('blgp,gnp->blgn', uf, bbar_re)
    x_im = jnp.einsum('blgp,gnp->blgn', uf, bbar_im)
    a_seq_re = jnp.broadcast_to(abar_re, (L,) + abar_re.shape)
    a_seq_im = jnp.broadcast_to(abar_im, (L,) + abar_im.shape)
    s_re, s_im = jax.vmap(diag_linear_scan, in_axes=(None, None, 0, 0))(a_seq_re, a_seq_im, x_re, x_im)
    y = (jnp.einsum('blgn,gpn->blgp', s_re, c_re.astype(F32))
         - jnp.einsum('blgn,gpn->blgp', s_im, c_im.astype(F32))
         + d_skip.astype(F32).reshape(SSM_GROUPS, SSM_GROUP) * uf)
    y = jax.nn.gelu(y.reshape(B, L, SSM_WIDTH))
    val, gate = jnp.split(y @ w_glu.astype(F32), 2, axis=-1)
    return (val * jax.nn.sigmoid(gate)).astype(u.dtype)


def rope(x, positions):
    half = HEAD_DIM // 2
    inv_freq = ROPE_THETA ** (-jnp.arange(half, dtype=F32) / half)
    ang = positions.astype(F32)[..., None] * inv_freq
    cos = jnp.cos(ang)[:, :, None, :]
    sin = jnp.sin(ang)[:, :, None, :]
    x1, x2 = x[..., :half], x[..., half:]
    return jnp.concatenate([x1 * cos - x2 * sin, x2 * cos + x1 * sin], axis=-1)


def dilated_window_attention(q, k, v, dilation, n_back):
    B, L, H, Dh = q.shape
    span = dilation * ATTN_BLOCK
    Lp = -(-L // span) * span
    pad = Lp - L
    M = Lp // dilation
    nb = M // ATTN_BLOCK

    def to_blocks(a):
        a = jnp.pad(a, ((0, 0), (0, pad), (0, 0), (0, 0)))
        a = a.reshape(B, M, dilation, H, Dh).transpose(0, 2, 3, 1, 4)
        return a.reshape(B, dilation, H, nb, ATTN_BLOCK, Dh)

    def with_prev(a):
        prev = jnp.pad(a, ((0, 0), (0, 0), (0, 0), (1, 0), (0, 0), (0, 0)))[:, :, :, :-1]
        return jnp.concatenate([prev, a], axis=4)

    qb = to_blocks(q)
    kc = with_prev(to_blocks(k))
    vc = with_prev(to_blocks(v))
    s = jnp.einsum('brhnqd,brhnkd->brhnqk', qb, kc) * (Dh ** -0.5)
    qi = jnp.arange(ATTN_BLOCK)[:, None] + ATTN_BLOCK
    ki = jnp.arange(2 * ATTN_BLOCK)[None, :]
    dist = qi - ki
    band = (dist >= 0) & (dist <= n_back)
    has_prev = (jnp.arange(nb)[:, None, None] > 0) | (ki[None] >= ATTN_BLOCK)
    mask = band[None] & has_prev
    s = jnp.where(mask, s, -jnp.inf)
    m = jnp.max(s, axis=-1, keepdims=True)
    p = jnp.exp(s - m)
    den = jnp.sum(p, axis=-1, keepdims=True)
    o = jnp.einsum('brhnqk,brhnkd->brhnqd', p, vc) / den
    lse = (m + jnp.log(den))[..., 0]
    o = o.reshape(B, dilation, H, M, Dh).transpose(0, 3, 1, 2, 4).reshape(B, Lp, H, Dh)[:, :L]
    lse = lse.reshape(B, dilation, H, M).transpose(0, 3, 1, 2).reshape(B, Lp, H)[:, :L]
    return o, lse


def attention_branch(q, k, v, positions):
    B, L, _ = q.shape
    q = rope(q.astype(F32).reshape(B, L, ATTN_HEADS, HEAD_DIM), positions)
    k = rope(k.astype(F32).reshape(B, L, ATTN_HEADS, HEAD_DIM), positions)
    v = v.astype(F32).reshape(B, L, ATTN_HEADS, HEAD_DIM)
    outs, lses = [], []
    for gi, (window, dil) in enumerate(ATTN_PATTERNS):
        hs = slice(gi * HEADS_PER_GROUP, (gi + 1) * HEADS_PER_GROUP)
        o, lse = dilated_window_attention(q[:, :, hs], k[:, :, hs], v[:, :, hs], dil, window // dil)
        outs.append(o)
        lses.append(lse)
    w = jax.nn.softmax(jnp.stack(lses), axis=0)
    out = jnp.einsum('gblh,gblhd->blhd', w, jnp.stack(outs))
    return out.reshape(B, L, ATTN_OUT)


def peer_ffn(h, wq, k1, k2, u_tab, v_tab):
    B, L, D = h.shape
    q = (h @ wq).astype(F32).reshape(B, L, PEER_HEADS, 2, PEER_QDIM // 2)
    s1 = jnp.einsum('blhc,kc->blhk', q[..., 0, :], k1.astype(F32))
    s2 = jnp.einsum('blhc,kc->blhk', q[..., 1, :], k2.astype(F32))
    v1, i1 = lax.top_k(s1, PEER_TOPK)
    v2, i2 = lax.top_k(s2, PEER_TOPK)
    cand_s = (v1[..., :, None] + v2[..., None, :]).reshape(B, L, PEER_HEADS, PEER_TOPK * PEER_TOPK)
    cand_i = (i1[..., :, None] * PEER_NKEYS + i2[..., None, :]).reshape(B, L, PEER_HEADS, PEER_TOPK * PEER_TOPK)
    top_s, pos = lax.top_k(cand_s, PEER_TOPK)
    expert = jnp.take_along_axis(cand_i, pos, axis=-1)
    gate = jax.nn.softmax(top_s, axis=-1)
    n_chunks = (B * L) // PEER_CHUNK
    hc = h.reshape(n_chunks, PEER_CHUNK, D)
    ec = expert.reshape(n_chunks, PEER_CHUNK, PEER_HEADS * PEER_TOPK)
    gc = gate.reshape(n_chunks, PEER_CHUNK, PEER_HEADS * PEER_TOPK).astype(h.dtype)

    def eval_chunk(args):
        hx, e, g = args
        a = jnp.einsum('cd,ckd->ck', hx, u_tab[e])
        return jnp.einsum('ck,ckd->cd', g * jax.nn.gelu(a), v_tab[e])

    out = lax.map(eval_chunk, (hc, ec, gc))
    return out.reshape(B, L, D)


def setup_inputs(seed: int = 0) -> dict:
    key = jax.random.key(seed)
    ks = list(jax.random.split(key, 32))

    def nrm(i, shape, scale):
        return jax.random.normal(ks[i], shape, F32) * scale

    n_idx = jnp.arange(SSM_STATE, dtype=F32)
    x = nrm(0, (BATCH, SEQ, D_MODEL), 1.0)
    positions = (jnp.arange(SEQ, dtype=jnp.int32)[None, :]
                 + jax.random.randint(ks[1], (BATCH, 1), 0, 1024, dtype=jnp.int32))
    norm1_g = 1.0 + nrm(2, (DEPTH, D_MODEL), 0.02)
    w_in = nrm(3, (DEPTH, D_MODEL, IN_COLS), D_MODEL ** -0.5)
    pool_w = nrm(4, (DEPTH, len(POOL_WINDOWS), POOL_GROUP, POOL_GROUP), POOL_GROUP ** -0.5)
    pool_scale = 1.0 + nrm(5, (DEPTH, POOL_WIDTH), 0.02)
    pool_proj = nrm(6, (DEPTH, POOL_WIDTH, D_MODEL), POOL_WIDTH ** -0.5)
    ssm_a_re = -0.5 + nrm(7, (DEPTH, SSM_GROUPS, SSM_STATE), 0.01)
    ssm_a_im = math.pi * n_idx + nrm(8, (DEPTH, SSM_GROUPS, SSM_STATE), 0.01)
    ssm_log_dt = jax.random.uniform(ks[9], (DEPTH, SSM_GROUPS), F32, math.log(1e-3), math.log(1e-1))
    b_scale = (2.0 * SSM_GROUP) ** -0.5
    ssm_b_re = nrm(10, (DEPTH, SSM_GROUPS, SSM_STATE, SSM_GROUP), b_scale)
    ssm_b_im = nrm(11, (DEPTH, SSM_GROUPS, SSM_STATE, SSM_GROUP), b_scale)
    c_scale = (2.0 * SSM_STATE) ** -0.5 * 4.0
    ssm_c_re = nrm(12, (DEPTH, SSM_GROUPS, SSM_GROUP, SSM_STATE), c_scale)
    ssm_c_im = nrm(13, (DEPTH, SSM_GROUPS, SSM_GROUP, SSM_STATE), c_scale)
    ssm_d = nrm(14, (DEPTH, SSM_WIDTH), 1.0)
    ssm_glu = nrm(15, (DEPTH, SSM_WIDTH, 2 * SSM_WIDTH), SSM_WIDTH ** -0.5)
    ssm_proj = nrm(16, (DEPTH, SSM_WIDTH, D_MODEL), SSM_WIDTH ** -0.5)
    attn_proj = nrm(17, (DEPTH, ATTN_OUT, D_MODEL), ATTN_OUT ** -0.5)
    w_out = nrm(18, (DEPTH, D_MODEL, D_MODEL), D_MODEL ** -0.5)
    norm2_g = 1.0 + nrm(19, (DEPTH, D_MODEL), 0.02)
    peer_wq = nrm(20, (DEPTH, D_MODEL, PEER_HEADS * PEER_QDIM), D_MODEL ** -0.5)
    peer_k1 = nrm(21, (DEPTH, PEER_NKEYS, PEER_QDIM // 2), (PEER_QDIM // 2) ** -0.5)
    peer_k2 = nrm(22, (DEPTH, PEER_NKEYS, PEER_QDIM // 2), (PEER_QDIM // 2) ** -0.5)
    peer_u = nrm(23, (DEPTH, PEER_EXPERTS, D_MODEL), D_MODEL ** -0.5)
    peer_v = nrm(24, (DEPTH, PEER_EXPERTS, D_MODEL), PEER_HEADS ** -0.5)
    final_g = 1.0 + nrm(25, (D_MODEL,), 0.02)
    return {"x": x, "positions": positions, "norm1_g": norm1_g, "w_in": w_in,
            "pool_w": pool_w, "pool_scale": pool_scale, "pool_proj": pool_proj,
            "ssm_a_re": ssm_a_re, "ssm_a_im": ssm_a_im, "ssm_log_dt": ssm_log_dt,
            "ssm_b_re": ssm_b_re, "ssm_b_im": ssm_b_im, "ssm_c_re": ssm_c_re, "ssm_c_im": ssm_c_im,
            "ssm_d": ssm_d, "ssm_glu": ssm_glu, "ssm_proj": ssm_proj, "attn_proj": attn_proj,
            "w_out": w_out, "norm2_g": norm2_g, "peer_wq": peer_wq, "peer_k1": peer_k1,
            "peer_k2": peer_k2, "peer_u": peer_u, "peer_v": peer_v, "final_g": final_g}


def reference(x, positions, norm1_g, w_in, pool_w, pool_scale, pool_proj,
              ssm_a_re, ssm_a_im, ssm_log_dt, ssm_b_re, ssm_b_im, ssm_c_re, ssm_c_im,
              ssm_d, ssm_glu, ssm_proj, attn_proj, w_out, norm2_g, peer_wq, peer_k1,
              peer_k2, peer_u, peer_v, final_g):
    B, L, D = x.shape
    offsets = [int(o) for o in np.cumsum(IN_SPLITS)[:-1]]
    for l in range(DEPTH):
        h = rmsnorm(x, norm1_g[l])
        z = h @ w_in[l]
        u_pool, u_ssm, q, k, v, gate_logits = jnp.split(z, offsets, axis=-1)
        y_pool = pool_branch(u_pool, pool_w[l], pool_scale[l]) @ pool_proj[l]
        y_ssm = ssm_branch(u_ssm, ssm_a_re[l], ssm_a_im[l], ssm_log_dt[l], ssm_b_re[l], ssm_b_im[l],
                           ssm_c_re[l], ssm_c_im[l], ssm_d[l], ssm_glu[l]) @ ssm_proj[l]
        y_attn = attention_branch(q, k, v, positions).astype(x.dtype) @ attn_proj[l]
        g = jax.nn.sigmoid(gate_logits).reshape(B, L, N_BRANCHES, D)
        merged = g[:, :, 0] * y_pool + g[:, :, 1] * y_ssm + g[:, :, 2] * y_attn
        x = x + merged @ w_out[l]
        h2 = rmsnorm(x, norm2_g[l])
        x = x + peer_ffn(h2, peer_wq[l], peer_k1[l], peer_k2[l], peer_u[l], peer_v[l])
    return rmsnorm(x, final_g)
```

```python
import functools
import math

import jax
import jax.numpy as jnp
from jax import lax
from jax.experimental import pallas as pl
from jax.experimental.pallas import tpu as pltpu

F32 = jnp.float32
BF16 = jnp.bfloat16

EPS = 1e-6
POOL_WINDOWS = (2, 4, 8, 16)
POOL_GROUP = 128
SSM_GROUP = 16
SSM_GROUPS = 32
SSM_STATE = 64
HEAD_DIM = 64
HEADS_PER_GROUP = 4
ATTN_DILATIONS = (1, 4, 16)
ATTN_BLOCK = 128
ROPE_THETA = 10000.0
PEER_HEADS = 8
PEER_TOPK = 16

LANES = 128
SUBLANES = 8
VMEM_LIMIT = 56 * 1024 * 1024
NEG_BIG = -1e30


def _cparams(sem):
    return pltpu.CompilerParams(dimension_semantics=sem, vmem_limit_bytes=VMEM_LIMIT)


def _gelu(x):
    c = math.sqrt(2.0 / math.pi)
    return x * (0.5 * (1.0 + jnp.tanh(c * (x + 0.044715 * (x * x * x)))))


def _sigmoid(x):
    return 1.0 / (1.0 + jnp.exp(-x))


def _rmsnorm(x, g):
    ms = jnp.mean(x * x, axis=-1, keepdims=True)
    return x * lax.rsqrt(ms + EPS) * g


def _dot(a, b):
    return jnp.dot(a, b, preferred_element_type=F32)


def _dot_nt(a, b):
    return lax.dot_general(a, b, (((1,), (1,)), ((), ())), preferred_element_type=F32)


def _rope_kernel(pos_ref, invf_ref, sign_ref, cos_ref, sin_ref):
    ang = pos_ref[...].astype(F32) * invf_ref[...]
    cos_ref[...] = jnp.cos(ang)
    sin_ref[...] = jnp.sin(ang) * sign_ref[...]


def _rope_tables(positions):
    b, l = positions.shape
    t = b * l
    half = HEAD_DIM // 2
    inv_freq = ROPE_THETA ** (-jnp.arange(half, dtype=F32) / half)
    invf = jnp.tile(inv_freq, LANES // half).reshape(1, LANES)
    sign = jnp.tile(jnp.concatenate([-jnp.ones((half,), F32), jnp.ones((half,), F32)]),
                    LANES // HEAD_DIM).reshape(1, LANES)
    pos_b = jnp.broadcast_to(positions.reshape(t, 1), (t, LANES))
    tr = 2048
    cos, sin = pl.pallas_call(
        _rope_kernel,
        grid=(t // tr,),
        in_specs=[pl.BlockSpec((tr, LANES), lambda i: (i, 0)),
                  pl.BlockSpec((1, LANES), lambda i: (0, 0)),
                  pl.BlockSpec((1, LANES), lambda i: (0, 0))],
        out_specs=[pl.BlockSpec((tr, LANES), lambda i: (i, 0))] * 2,
        out_shape=[jax.ShapeDtypeStruct((t, LANES), F32)] * 2,
        compiler_params=_cparams(("parallel",)),
        name="rope_tables",
    )(pos_b, invf, sign)
    return cos.reshape(b, l, LANES), sin.reshape(b, l, LANES)


def _inproj_kernel(*refs, has_delta, pool_w, ssm_w, attn_w, d_model):
    if has_delta:
        x_ref, dl_ref, g_ref, w_ref, cos_ref, sin_ref = refs[:6]
        outs = refs[6:]
    else:
        x_ref, g_ref, w_ref, cos_ref, sin_ref = refs[:5]
        outs = refs[5:]
    pool_ref, ssm_ref, q_ref, k_ref, v_ref, gate_ref = outs[:6]
    x = x_ref[0]
    if has_delta:
        x = x + dl_ref[0]
        outs[6][0] = x
    h = _rmsnorm(x, g_ref[...]).astype(BF16)
    c0 = 0
    z = _dot(h, w_ref[:, c0:c0 + pool_w + ssm_w])
    pool_ref[0] = z[:, :pool_w]
    ssm_ref[...] = z[:, pool_w:]
    c0 += pool_w + ssm_w
    cos = cos_ref[0]
    sin = sin_ref[0]
    nslab = attn_w // LANES
    for dst in (q_ref, k_ref):
        z = _dot(h, w_ref[:, c0:c0 + 2 * attn_w])
        for s in range(nslab):
            a = z[:, s * LANES:(s + 1) * LANES]
            a_sw = z[:, attn_w + s * LANES:attn_w + (s + 1) * LANES]
            dst[0, s] = a * cos + a_sw * sin
        c0 += 2 * attn_w
    z = _dot(h, w_ref[:, c0:c0 + attn_w])
    for s in range(nslab):
        v_ref[0, s] = z[:, s * LANES:(s + 1) * LANES]
    c0 += attn_w
    z = _dot(h, w_ref[:, c0:c0 + 3 * d_model])
    gate_ref[0] = _sigmoid(z).astype(BF16)


def _in_projection(x, delta, g, w_ext, cos, sin, *, pool_w, ssm_w, attn_w):
    b, l, d = x.shape
    tm = 256
    nslab = attn_w // LANES
    has_delta = delta is not None
    kern = functools.partial(_inproj_kernel, has_delta=has_delta, pool_w=pool_w, ssm_w=ssm_w,
                             attn_w=attn_w, d_model=d)
    row = lambda bi, i: (bi, i, 0)
    slab = lambda bi, i: (bi, 0, i, 0)
    const2 = lambda bi, i: (0, 0)
    in_specs = [pl.BlockSpec((1, tm, d), row)]
    args = [x]
    if has_delta:
        in_specs.append(pl.BlockSpec((1, tm, d), row))
        args.append(delta)
    in_specs += [pl.BlockSpec((1, d), const2),
                 pl.BlockSpec(w_ext.shape, const2, pipeline_mode=pl.Buffered(1)),
                 pl.BlockSpec((1, tm, LANES), row),
                 pl.BlockSpec((1, tm, LANES), row)]
    args += [g.reshape(1, d), w_ext, cos, sin]
    out_specs = [pl.BlockSpec((1, tm, pool_w), row),
                 pl.BlockSpec((tm, ssm_w), lambda bi, i: (i, bi)),
                 pl.BlockSpec((1, nslab, tm, LANES), slab),
                 pl.BlockSpec((1, nslab, tm, LANES), slab),
                 pl.BlockSpec((1, nslab, tm, LANES), slab),
                 pl.BlockSpec((1, tm, 3 * d), row)]
    out_shape = [jax.ShapeDtypeStruct((b, l, pool_w), F32),
                 jax.ShapeDtypeStruct((l, b * ssm_w), F32),
                 jax.ShapeDtypeStruct((b, nslab, l, LANES), F32),
                 jax.ShapeDtypeStruct((b, nslab, l, LANES), F32),
                 jax.ShapeDtypeStruct((b, nslab, l, LANES), F32),
                 jax.ShapeDtypeStruct((b, l, 3 * d), BF16)]
    if has_delta:
        out_specs.append(pl.BlockSpec((1, tm, d), row))
        out_shape.append(jax.ShapeDtypeStruct((b, l, d), F32))
    return pl.pallas_call(
        kern, grid=(b, l // tm), in_specs=in_specs, out_specs=out_specs, out_shape=out_shape,
        compiler_params=_cparams(("parallel", "parallel")), name="in_projection",
    )(*args)


def _pool_kernel(u_ref, halo_ref, w_ref, scale_ref, o_ref, *, tm, halo):
    i = pl.program_id(1)
    cur = u_ref[0]
    prev = jnp.where(i > 0, halo_ref[0], 0.0)
    ext = jnp.concatenate([prev, cur], axis=0)
    t_idx = i * tm + lax.broadcasted_iota(jnp.int32, (tm, POOL_GROUP), 0)
    for gi, win in enumerate(POOL_WINDOWS):
        cs = slice(gi * POOL_GROUP, (gi + 1) * POOL_GROUP)
        s = ext[:, cs]
        sh = 1
        while sh < win:
            s = s + pltpu.roll(s, sh, axis=0)
            sh *= 2
        count = jnp.minimum(t_idx + 1, win).astype(F32)
        pooled = s[halo:, :] / count - cur[:, cs]
        mixed = _dot(pooled.astype(BF16), w_ref[gi]) * scale_ref[:, cs]
        o_ref[0, :, cs] = mixed.astype(BF16)


def _pool_mixer(u, w, scale):
    b, l, c = u.shape
    tm, halo = 512, 16
    assert halo >= max(POOL_WINDOWS) and halo % SUBLANES == 0
    kern = functools.partial(_pool_kernel, tm=tm, halo=halo)
    return pl.pallas_call(
        kern, grid=(b, l // tm),
        in_specs=[pl.BlockSpec((1, tm, c), lambda bi, i: (bi, i, 0)),
                  pl.BlockSpec((1, halo, c), lambda bi, i: (bi, jnp.maximum(i * (tm // halo) - 1, 0), 0)),
                  pl.BlockSpec(w.shape, lambda bi, i: (0, 0, 0)),
                  pl.BlockSpec((1, c), lambda bi, i: (0, 0))],
        out_specs=pl.BlockSpec((1, tm, c), lambda bi, i: (bi, i, 0)),
        out_shape=jax.ShapeDtypeStruct((b, l, c), BF16),
        compiler_params=_cparams(("parallel", "parallel")), name="pool_mixer",
    )(u, u, w, scale.reshape(1, c))


def _ssm_param_kernel(are_ref, aim_ref, ldt_ref, bre_ref, bim_ref, oar_ref, oai_ref, obr_ref, obi_ref):
    ar = are_ref[...]
    ai = aim_ref[...]
    dt = jnp.exp(ldt_ref[...])
    decay = jnp.exp(ar * dt)
    abar_re = decay * jnp.cos(ai * dt)
    abar_im = decay * jnp.sin(ai * dt)
    den = ar * ar + ai * ai
    nr = abar_re - 1.0
    k_re = (nr * ar + abar_im * ai) / den
    k_im = (abar_im * ar - nr * ai) / den
    oar_ref[...] = abar_re
    oai_ref[...] = abar_im
    for gi in range(are_ref.shape[0]):
        kr = k_re[gi:gi + 1, :]
        ki = k_im[gi:gi + 1, :]
        br = bre_ref[gi]
        bi = bim_ref[gi]
        obr_ref[gi] = kr * br - ki * bi
        obi_ref[gi] = kr * bi + ki * br


def _ssm_params(a_re, a_im, log_dt, b_re, b_im):
    g, n, p = b_re.shape
    brt = jnp.transpose(b_re, (0, 2, 1))
    bit = jnp.transpose(b_im, (0, 2, 1))
    return pl.pallas_call(
        _ssm_param_kernel,
        out_shape=[jax.ShapeDtypeStruct((g, n), F32)] * 2 + [jax.ShapeDtypeStruct((g, p, n), F32)] * 2,
        name="ssm_params",
    )(a_re, a_im, log_dt.reshape(g, 1), brt, bit)


def _block_diag(m):
    k, r, c = m.shape
    eye = jnp.eye(k, dtype=m.dtype)
    return (m[:, :, None, :] * eye[:, None, :, None]).reshape(k * r, k * c)


def _ssm_kernel(u_ref, bre_ref, bim_ref, are_ref, aim_ref, cre_ref, cim_ref, d_ref, y_ref,
                st_ref, xre_ref, xim_ref, sre_ref, sim_ref, *, tc, hw):
    rows = tc * SUBLANES

    @pl.when(pl.program_id(0) == 0)
    def _():
        st_ref[...] = jnp.zeros_like(st_ref)

    u = u_ref[...]
    odd = (lax.broadcasted_iota(jnp.int32, (rows, hw), 0) % 2) == 1
    ub = u.astype(BF16)
    zero = jnp.zeros_like(ub)
    u_exp = jnp.concatenate([jnp.where(odd, zero, ub), jnp.where(odd, ub, zero)], axis=1)
    xre_ref[...] = _dot(u_exp, bre_ref[...])
    xim_ref[...] = _dot(u_exp, bim_ref[...])
    a_re = are_ref[...]
    a_im = aim_ref[...]

    def step(t, carry):
        s_re, s_im = carry
        r0 = pl.multiple_of(t * SUBLANES, SUBLANES)
        n_re = a_re * s_re - a_im * s_im + xre_ref[pl.ds(r0, SUBLANES), :]
        n_im = a_re * s_im + a_im * s_re + xim_ref[pl.ds(r0, SUBLANES), :]
        sre_ref[pl.ds(r0, SUBLANES), :] = n_re
        sim_ref[pl.ds(r0, SUBLANES), :] = n_im
        return n_re, n_im

    s_re, s_im = lax.fori_loop(0, tc, step, (st_ref[0], st_ref[1]), unroll=8)
    st_ref[0] = s_re
    st_ref[1] = s_im
    yf = _dot(sre_ref[...].astype(BF16), cre_ref[...]) - _dot(sim_ref[...].astype(BF16), cim_ref[...])
    y = jnp.where(odd, yf[:, hw:], yf[:, :hw]) + pltpu.repeat(d_ref[...], tc, axis=0) * u
    y_ref[...] = _gelu(y).astype(BF16)


def _ssm_mixer(u_scan, batch, abar_re, abar_im, bbar_re_t, bbar_im_t, c_re, c_im, d_skip):
    l = u_scan.shape[0]
    g, p, n = bbar_re_t.shape
    hg = g // 2
    hw = hg * p
    hs = hg * n
    assert batch * 2 == SUBLANES
    rows_total = l * SUBLANES
    u2 = u_scan.reshape(rows_total, hw)

    def stack_b(bt):
        return jnp.concatenate([_block_diag(bt[:hg]), _block_diag(bt[hg:])], axis=0).astype(BF16)

    def cat_c(c):
        ct = jnp.transpose(c, (0, 2, 1))
        return jnp.concatenate([_block_diag(ct[:hg]), _block_diag(ct[hg:])], axis=1).astype(BF16)

    def tile_a(a):
        return jnp.tile(a.reshape(2, hs), (batch, 1))

    d8 = jnp.tile(d_skip.reshape(2, hw), (batch, 1))
    tc = 64
    rows = tc * SUBLANES
    kern = functools.partial(_ssm_kernel, tc=tc, hw=hw)
    const = lambda i: (0, 0)
    y = pl.pallas_call(
        kern, grid=(l // tc,),
        in_specs=[pl.BlockSpec((rows, hw), lambda i: (i, 0)),
                  pl.BlockSpec((2 * hw, hs), const), pl.BlockSpec((2 * hw, hs), const),
                  pl.BlockSpec((SUBLANES, hs), const), pl.BlockSpec((SUBLANES, hs), const),
                  pl.BlockSpec((hs, 2 * hw), const), pl.BlockSpec((hs, 2 * hw), const),
                  pl.BlockSpec((SUBLANES, hw), const)],
        out_specs=pl.BlockSpec((rows, hw), lambda i: (i, 0)),
        out_shape=jax.ShapeDtypeStruct((rows_total, hw), BF16),
        scratch_shapes=[pltpu.VMEM((2, SUBLANES, hs), F32),
                        pltpu.VMEM((rows, hs), F32), pltpu.VMEM((rows, hs), F32),
                        pltpu.VMEM((rows, hs), F32), pltpu.VMEM((rows, hs), F32)],
        compiler_params=_cparams(("arbitrary",)), name="ssm_mixer",
    )(u2, stack_b(bbar_re_t), stack_b(bbar_im_t), tile_a(abar_re), tile_a(abar_im),
      cat_c(c_re), cat_c(c_im), d8)
    return y.reshape(l, batch * 2 * hw)


def _attn_group(q_ref, k_ref, v_ref, kb, vb, acc, mrun, drun, *, d, ta, i):
    p = ATTN_BLOCK * d
    ncombo = ta // ATTN_BLOCK
    nslab = q_ref.shape[1]
    scale = HEAD_DIM ** -0.5
    kb[:, p:p + ta, :] = k_ref[0]
    vb[:, p:p + ta, :] = v_ref[0]

    @pl.when(i == 0)
    def _():
        kb[:, :p, :] = jnp.zeros((nslab, p, LANES), F32)
        vb[:, :p, :] = jnp.zeros((nslab, p, LANES), F32)

    qi = lax.broadcasted_iota(jnp.int32, (ATTN_BLOCK, ATTN_BLOCK), 0)
    kk = lax.broadcasted_iota(jnp.int32, (ATTN_BLOCK, ATTN_BLOCK), 1)
    mask_cur = kk <= qi
    mask_prev = kk >= qi
    low = kk < HEAD_DIM

    def rows_at(start):
        return pl.ds(start, ATTN_BLOCK) if d == 1 else pl.ds(start, ATTN_BLOCK, stride=d)

    def combo(c, carry):
        r = c % d
        n = c // d
        qs = n * p + r
        has_prev = jnp.logical_or(i > 0, n > 0)
        mask_p = jnp.logical_and(mask_prev, has_prev)
        for s in range(nslab):
            q = q_ref[0, s, rows_at(qs), :] * scale
            kc = kb[s, rows_at(p + qs), :].astype(BF16)
            kp = kb[s, rows_at(qs), :].astype(BF16)
            vc = vb[s, rows_at(p + qs), :].astype(BF16)
            vp = vb[s, rows_at(qs), :].astype(BF16)
            parts = []
            for hh in range(LANES // HEAD_DIM):
                mine = low if hh == 0 else jnp.logical_not(low)
                qh = jnp.where(mine, q, 0.0).astype(BF16)
                sc = jnp.where(mask_cur, _dot_nt(qh, kc), -jnp.inf)
                sp = jnp.where(mask_p, _dot_nt(qh, kp), -jnp.inf)
                m = jnp.maximum(jnp.max(sc, axis=1, keepdims=True), jnp.max(sp, axis=1, keepdims=True))
                pc = jnp.exp(sc - m)
                pp = jnp.exp(sp - m)
                den = jnp.sum(pc, axis=1, keepdims=True) + jnp.sum(pp, axis=1, keepdims=True)
                o = _dot(pc.astype(BF16), vc) + _dot(pp.astype(BF16), vp)
                parts.append((m, den, o))
            m_blk = jnp.where(low, parts[0][0], parts[1][0])
            l_blk = jnp.where(low, parts[0][1], parts[1][1])
            o_blk = jnp.where(low, parts[0][2], parts[1][2])
            rows = rows_at(qs)
            m_old = mrun[s, rows, :]
            m_new = jnp.maximum(m_old, m_blk)
            a_old = jnp.exp(m_old - m_new)
            a_blk = jnp.exp(m_blk - m_new)
            acc[s, rows, :] = acc[s, rows, :] * a_old + o_blk * a_blk
            drun[s, rows, :] = drun[s, rows, :] * a_old + l_blk * a_blk
            mrun[s, rows, :] = m_new
        return carry

    lax.fori_loop(0, ncombo, combo, 0)
    kb[:, :p, :] = kb[:, ta:ta + p, :]
    vb[:, :p, :] = vb[:, ta:ta + p, :]


def _attn_kernel(q_ref, k_ref, v_ref, o_ref, *scratch, ta):
    ng = len(ATTN_DILATIONS)
    kbs = scratch[0:2 * ng:2]
    vbs = scratch[1:2 * ng:2]
    acc, mrun, drun = scratch[2 * ng:]
    i = pl.program_id(1)
    g = pl.program_id(2)

    @pl.when(g == 0)
    def _():
        acc[...] = jnp.zeros_like(acc)
        drun[...] = jnp.zeros_like(drun)
        mrun[...] = jnp.full(mrun.shape, NEG_BIG, F32)

    for gi, d in enumerate(ATTN_DILATIONS):
        @pl.when(g == gi)
        def _(gi=gi, d=d):
            _attn_group(q_ref, k_ref, v_ref, kbs[gi], vbs[gi], acc, mrun, drun, d=d, ta=ta, i=i)

    @pl.when(g == ng - 1)
    def _():
        for s in range(acc.shape[0]):
            o_ref[0, :, s * LANES:(s + 1) * LANES] = acc[s] / drun[s]


def _attention(q, k, v):
    b, nslab, l, _ = q.shape
    ng = len(ATTN_DILATIONS)
    gs = nslab // ng
    ta = ATTN_BLOCK * max(ATTN_DILATIONS)
    assert l % ta == 0
    blk = pl.BlockSpec((1, gs, ta, LANES), lambda bi, i, g: (bi, g, i, 0))
    scratch = []
    for d in ATTN_DILATIONS:
        scratch += [pltpu.VMEM((gs, ATTN_BLOCK * d + ta, LANES), F32)] * 2
    scratch += [pltpu.VMEM((gs, ta, LANES), F32)] * 3
    return pl.pallas_call(
        functools.partial(_attn_kernel, ta=ta), grid=(b, l // ta, ng),
        in_specs=[blk, blk, blk],
        out_specs=pl.BlockSpec((1, ta, gs * LANES), lambda bi, i, g: (bi, i, 0)),
        out_shape=jax.ShapeDtypeStruct((b, l, gs * LANES), F32),
        scratch_shapes=scratch,
        compiler_params=_cparams(("arbitrary", "arbitrary", "arbitrary")), name="dilated_attention",
    )(q, k, v)


def _merge_kernel(pm_ref, ys_ref, ao_ref, gate_ref, x_ref, wp_ref, wg_ref, ws_ref, wa_ref, wo_ref, g2_ref,
                  xo_ref, h2_ref, *, d_model, ssm_w):
    y_pool = _dot(pm_ref[0], wp_ref[...])
    glu = _dot(ys_ref[...], wg_ref[...])
    sg = (glu[:, :ssm_w] * _sigmoid(glu[:, ssm_w:])).astype(BF16)
    y_ssm = _dot(sg, ws_ref[...])
    y_attn = _dot(ao_ref[0].astype(BF16), wa_ref[...])
    gate = gate_ref[0].astype(F32)
    merged = (gate[:, :d_model] * y_pool + gate[:, d_model:2 * d_model] * y_ssm
              + gate[:, 2 * d_model:] * y_attn)
    xn = x_ref[0] + _dot(merged.astype(BF16), wo_ref[...])
    xo_ref[0] = xn
    h2_ref[0] = _rmsnorm(xn, g2_ref[...]).astype(BF16)


def _merge(pm, ys, ao, gates, x, w_pool, w_glu, w_ssm, w_attn, w_out, g2):
    b, l, d = x.shape
    tm = 256
    ssm_w = w_glu.shape[0]
    row = lambda bi, i: (bi, i, 0)
    const = lambda bi, i: (0, 0)
    full = lambda a: pl.BlockSpec(a.shape, const)
    return pl.pallas_call(
        functools.partial(_merge_kernel, d_model=d, ssm_w=ssm_w), grid=(b, l // tm),
        in_specs=[pl.BlockSpec((1, tm, pm.shape[-1]), row),
                  pl.BlockSpec((tm, ssm_w), lambda bi, i: (i, bi)),
                  pl.BlockSpec((1, tm, ao.shape[-1]), row),
                  pl.BlockSpec((1, tm, 3 * d), row),
                  pl.BlockSpec((1, tm, d), row),
                  full(w_pool), full(w_glu), full(w_ssm), full(w_attn), full(w_out),
                  pl.BlockSpec((1, d), const)],
        out_specs=[pl.BlockSpec((1, tm, d), row), pl.BlockSpec((1, tm, d), row)],
        out_shape=[jax.ShapeDtypeStruct((b, l, d), F32), jax.ShapeDtypeStruct((b, l, d), BF16)],
        compiler_params=_cparams(("parallel", "parallel")), name="branch_merge",
    )(pm, ys, ao, gates, x, w_pool, w_glu, w_ssm, w_attn, w_out, g2.reshape(1, d))


def _candidate_pairs(k):
    return [(a, b) for a in range(k) for b in range(k) if (a + 1) * (b + 1) <= k]


def _extract_topk(vals, k, on_pick):
    nrows = vals.shape[0]
    rows = lax.broadcasted_iota(jnp.int32, vals.shape, 0).astype(F32)
    work = vals
    for r in range(k):
        m = jnp.max(work, axis=0, keepdims=True)
        idx = jnp.min(jnp.where(work == m, rows, float(nrows)), axis=0, keepdims=True)
        hit = rows == idx
        work = jnp.where(hit, -jnp.inf, work)
        on_pick(r, m, hit)


def _route_kernel(h_ref, wq_ref, k1_ref, k2_ref, r2_ref, e2_ref, n1_ref, e1_ref,
                  qt_ref, s1_ref, s2_ref, v1_ref, v2_ref, cand_ref, sel_ref, *, nheads, nkeys, topk, tt):
    qt_ref[...] = _dot_nt(wq_ref[...], h_ref[...])
    hc = k1_ref.shape[1]
    pairs = _candidate_pairs(topk)
    ncand = cand_ref.shape[0]

    def ranks_of(s, v_ref):
        state = {"rank": jnp.full(s.shape, float(nkeys), F32)}

        def pick(r, m, hit):
            state["rank"] = jnp.where(hit, float(r), state["rank"])
            v_ref[r:r + 1, :] = m

        _extract_topk(s, topk, pick)
        return state["rank"]

    def head(hd, carry):
        base = pl.multiple_of(hd * 2 * hc, 2 * hc)
        q1 = qt_ref[pl.ds(base, hc), :].astype(BF16)
        q2 = qt_ref[pl.ds(base + hc, hc), :].astype(BF16)
        s1_ref[...] = _dot(k1_ref[...], q1)
        s2_ref[...] = _dot(k2_ref[...], q2)

        def col(j, carry2):
            c0 = pl.multiple_of(j * LANES, LANES)
            cols = pl.ds(c0, LANES)
            s1 = s1_ref[:, cols]
            s2 = s2_ref[:, cols]
            rank1 = ranks_of(s1, v1_ref)
            rank2 = ranks_of(s2, v2_ref)
            for ci, (a, bq) in enumerate(pairs):
                cand_ref[ci:ci + 1, :] = v1_ref[a:a + 1, :] + v2_ref[bq:bq + 1, :]
            if ncand > len(pairs):
                cand_ref[len(pairs):, :] = jnp.full((ncand - len(pairs), LANES), -jnp.inf, F32)
            cv = cand_ref[...]
            state = {"sel": jnp.zeros(cv.shape, F32)}

            def pick(r, m, hit):
                state["sel"] = jnp.where(hit, 1.0, state["sel"])

            _extract_topk(cv, topk, pick)
            sel = state["sel"]
            sel_ref[...] = sel
            top = v1_ref[0:1, :] + v2_ref[0:1, :]
            z = jnp.sum(sel * jnp.exp(jnp.where(sel > 0.0, cv, top) - top), axis=0, keepdims=True)
            n1 = jnp.zeros(s1.shape, F32)
            off = 0
            for a in range(topk):
                cnt = topk // (a + 1)
                n_a = jnp.sum(sel_ref[off:off + cnt, :], axis=0, keepdims=True)
                n1 = jnp.where(rank1 == float(a), n_a, n1)
                off += cnt
            n1_ref[hd, :, cols] = n1
            e1_ref[hd, :, cols] = jnp.exp(s1 - v1_ref[0:1, :]) / z
            r2_ref[hd, :, cols] = rank2
            e2_ref[hd, :, cols] = jnp.exp(s2 - v2_ref[0:1, :])
            return carry2

        lax.fori_loop(0, tt // LANES, col, 0)
        return carry

    lax.fori_loop(0, nheads, head, 0)


def _peer_route(h2, wq_t, k1, k2):
    t, d = h2.shape
    nkeys, hc = k1.shape
    nheads = wq_t.shape[0] // (2 * hc)
    tt = 512
    npairs = len(_candidate_pairs(PEER_TOPK))
    ncand = -(-npairs // SUBLANES) * SUBLANES
    kern = functools.partial(_route_kernel, nheads=nheads, nkeys=nkeys, topk=PEER_TOPK, tt=tt)
    oblk = pl.BlockSpec((nheads, nkeys, tt), lambda i: (0, 0, i))
    osh = jax.ShapeDtypeStruct((nheads, nkeys, t), F32)
    return pl.pallas_call(
        kern, grid=(t // tt,),
        in_specs=[pl.BlockSpec((tt, d), lambda i: (i, 0)),
                  pl.BlockSpec(wq_t.shape, lambda i: (0, 0)),
                  pl.BlockSpec(k1.shape, lambda i: (0, 0)),
                  pl.BlockSpec(k2.shape, lambda i: (0, 0))],
        out_specs=[oblk] * 4, out_shape=[osh] * 4,
        scratch_shapes=[pltpu.VMEM((wq_t.shape[0], tt), F32),
                        pltpu.VMEM((nkeys, tt), F32), pltpu.VMEM((nkeys, tt), F32),
                        pltpu.VMEM((PEER_TOPK, LANES), F32), pltpu.VMEM((PEER_TOPK, LANES), F32),
                        pltpu.VMEM((ncand, LANES), F32), pltpu.VMEM((ncand, LANES), F32)],
        compiler_params=_cparams(("parallel",)), name="peer_route",
    )(h2, wq_t, k1, k2)


def _ffn_kernel(h_ref, u_ref, vt_ref, r2_ref, e2_ref, n1_ref, e1_ref, o_ref, acc_ref, at_ref, pt_ref,
                *, nheads, nkeys, tt, et):
    j = pl.program_id(1)

    @pl.when(j == 0)
    def _():
        acc_ref[...] = jnp.zeros_like(acc_ref)

    at_ref[...] = _dot_nt(u_ref[...], h_ref[...])
    nsub = et // nkeys

    def lane_chunk(lc, carry):
        cols = pl.ds(pl.multiple_of(lc * LANES, LANES), LANES)
        for jj in range(nsub):
            gate = jnp.zeros((nkeys, LANES), F32)
            for hd in range(nheads):
                n_row = n1_ref[hd, jj:jj + 1, cols]
                e_row = e1_ref[hd, jj:jj + 1, cols]
                gate = gate + jnp.where(r2_ref[hd, :, cols] < n_row, e2_ref[hd, :, cols], 0.0) * e_row
            rows = slice(jj * nkeys, (jj + 1) * nkeys)
            pt_ref[rows, cols] = (gate * _gelu(at_ref[rows, cols])).astype(BF16)
        return carry

    lax.fori_loop(0, tt // LANES, lane_chunk, 0)
    acc_ref[...] += _dot(vt_ref[...], pt_ref[...])

    @pl.when(j == pl.num_programs(1) - 1)
    def _():
        o_ref[...] = acc_ref[...].T


def _peer_experts(h2, u_tab, v_tab_t, r2, e2, n1, e1):
    t, d = h2.shape
    ne = u_tab.shape[0]
    nheads, nkeys, _ = r2.shape
    tt, et = 1024, 1024
    nsub = et // nkeys
    assert nsub == SUBLANES
    kern = functools.partial(_ffn_kernel, nheads=nheads, nkeys=nkeys, tt=tt, et=et)
    full_keys = pl.BlockSpec((nheads, nkeys, tt), lambda i, j: (0, 0, i))
    sub_keys = pl.BlockSpec((nheads, nsub, tt), lambda i, j: (0, j, i))
    return pl.pallas_call(
        kern, grid=(t // tt, ne // et),
        in_specs=[pl.BlockSpec((tt, d), lambda i, j: (i, 0)),
                  pl.BlockSpec((et, d), lambda i, j: (j, 0)),
                  pl.BlockSpec((d, et), lambda i, j: (0, j)),
                  full_keys, full_keys, sub_keys, sub_keys],
        out_specs=pl.BlockSpec((tt, d), lambda i, j: (i, 0)),
        out_shape=jax.ShapeDtypeStruct((t, d), F32),
        scratch_shapes=[pltpu.VMEM((d, tt), F32), pltpu.VMEM((et, tt), F32), pltpu.VMEM((et, tt), BF16)],
        compiler_params=_cparams(("parallel", "arbitrary")), name="peer_experts",
    )(h2, u_tab, v_tab_t, r2, e2, n1, e1)


def _final_kernel(x_ref, dl_ref, g_ref, o_ref):
    o_ref[...] = _rmsnorm(x_ref[...] + dl_ref[...], g_ref[...])


def _final_norm(x, delta, g):
    t, d = x.shape
    tm = 1024
    blk = pl.BlockSpec((tm, d), lambda i: (i, 0))
    return pl.pallas_call(
        _final_kernel, grid=(t // tm,),
        in_specs=[blk, blk, pl.BlockSpec((1, d), lambda i: (0, 0))],
        out_specs=blk, out_shape=jax.ShapeDtypeStruct((t, d), F32),
        compiler_params=_cparams(("parallel",)), name="final_norm",
    )(x, delta, g.reshape(1, d))


def _swap_halves(w):
    d, c = w.shape
    half = HEAD_DIM // 2
    return w.reshape(d, c // HEAD_DIM, 2, half)[:, :, ::-1, :].reshape(d, c)


def kernel(x, positions, norm1_g, w_in, pool_w, pool_scale, pool_proj, ssm_a_re, ssm_a_im, ssm_log_dt,
           ssm_b_re, ssm_b_im, ssm_c_re, ssm_c_im, ssm_d, ssm_glu, ssm_proj, attn_proj, w_out, norm2_g,
           peer_wq, peer_k1, peer_k2, peer_u, peer_v, final_g):
    b, l, d = x.shape
    depth = w_in.shape[0]
    pool_w_cols = pool_proj.shape[1]
    ssm_w_cols = ssm_proj.shape[1]
    attn_w = len(ATTN_DILATIONS) * HEADS_PER_GROUP * HEAD_DIM
    cos, sin = _rope_tables(positions)
    delta = None
    for layer in range(depth):
        w = w_in[layer]
        o0 = pool_w_cols + ssm_w_cols
        wq, wk, wv, wg = (w[:, o0:o0 + attn_w], w[:, o0 + attn_w:o0 + 2 * attn_w],
                          w[:, o0 + 2 * attn_w:o0 + 3 * attn_w], w[:, o0 + 3 * attn_w:])
        w_ext = jnp.concatenate([w[:, :o0], wq, _swap_halves(wq), wk, _swap_halves(wk), wv, wg],
                                axis=1).astype(BF16)
        outs = _in_projection(x, delta, norm1_g[layer], w_ext, cos, sin,
                              pool_w=pool_w_cols, ssm_w=ssm_w_cols, attn_w=attn_w)
        u_pool, u_ssm, q, k, v, gates = outs[:6]
        if delta is not None:
            x = outs[6]
        pm = _pool_mixer(u_pool, pool_w[layer].astype(BF16), pool_scale[layer])
        abar_re, abar_im, bbar_re_t, bbar_im_t = _ssm_params(
            ssm_a_re[layer], ssm_a_im[layer], ssm_log_dt[layer], ssm_b_re[layer], ssm_b_im[layer])
        ys = _ssm_mixer(u_ssm, b, abar_re, abar_im, bbar_re_t, bbar_im_t,
                        ssm_c_re[layer], ssm_c_im[layer], ssm_d[layer])
        ao = _attention(q, k, v)
        x, h2 = _merge(pm, ys, ao, gates, x, pool_proj[layer].astype(BF16), ssm_glu[layer].astype(BF16),
                       ssm_proj[layer].astype(BF16), attn_proj[layer].astype(BF16),
                       w_out[layer].astype(BF16), norm2_g[layer])
        h2f = h2.reshape(b * l, d)
        r2, e2, n1, e1 = _peer_route(h2f, peer_wq[layer].T.astype(BF16), peer_k1[layer].astype(BF16),
                                     peer_k2[layer].astype(BF16))
        delta = _peer_experts(h2f, peer_u[layer].astype(BF16), peer_v[layer].T.astype(BF16),
                              r2, e2, n1, e1).reshape(b, l, d)
    return _final_norm(x.reshape(b * l, d), delta.reshape(b * l, d), final_g).reshape(b, l, d)
```

```python
import functools
import math

import jax
import jax.numpy as jnp
from jax import lax
from jax.experimental import pallas as pl
from jax.experimental.pallas import tpu as pltpu

F32 = jnp.float32
BF16 = jnp.bfloat16

EPS = 1e-6
POOL_WINDOWS = (2, 4, 8, 16)
POOL_GROUP = 128
SSM_GROUP = 16
SSM_GROUPS = 32
SSM_STATE = 64
HEAD_DIM = 64
HEADS_PER_GROUP = 4
ATTN_DILATIONS = (1, 4, 16)
ATTN_BLOCK = 128
ROPE_THETA = 10000.0
PEER_HEADS = 8
PEER_TOPK = 16

LANES = 128
SUBLANES = 8
BF16_ROWS = 16
VMEM_LIMIT = 56 * 1024 * 1024
NEG_BIG = -1e30


def _cparams(sem):
    return pltpu.CompilerParams(dimension_semantics=sem, vmem_limit_bytes=VMEM_LIMIT)


def _gelu(x):
    c = math.sqrt(2.0 / math.pi)
    return x * (0.5 * (1.0 + jnp.tanh(c * (x + 0.044715 * (x * x * x)))))


def _sigmoid(x):
    return 1.0 / (1.0 + jnp.exp(-x))


def _rmsnorm(x, g):
    ms = jnp.mean(x * x, axis=-1, keepdims=True)
    return x * lax.rsqrt(ms + EPS) * g


def _dot(a, b):
    return jnp.dot(a, b, preferred_element_type=F32)


def _dot_nt(a, b):
    return lax.dot_general(a, b, (((1,), (1,)), ((), ())), preferred_element_type=F32)


def _rope_kernel(pos_ref, invf_ref, sign_ref, cos_ref, sin_ref):
    ang = pos_ref[...].astype(F32) * invf_ref[...]
    cos_ref[...] = jnp.cos(ang)
    sin_ref[...] = jnp.sin(ang) * sign_ref[...]


def _rope_tables(positions):
    b, l = positions.shape
    t = b * l
    half = HEAD_DIM // 2
    inv_freq = ROPE_THETA ** (-jnp.arange(half, dtype=F32) / half)
    invf = jnp.tile(inv_freq, LANES // half).reshape(1, LANES)
    sign = jnp.tile(jnp.concatenate([-jnp.ones((half,), F32), jnp.ones((half,), F32)]),
                    LANES // HEAD_DIM).reshape(1, LANES)
    pos_b = jnp.broadcast_to(positions.reshape(t, 1), (t, LANES))
    tr = 2048
    cos, sin = pl.pallas_call(
        _rope_kernel,
        grid=(t // tr,),
        in_specs=[pl.BlockSpec((tr, LANES), lambda i: (i, 0)),
                  pl.BlockSpec((1, LANES), lambda i: (0, 0)),
                  pl.BlockSpec((1, LANES), lambda i: (0, 0))],
        out_specs=[pl.BlockSpec((tr, LANES), lambda i: (i, 0))] * 2,
        out_shape=[jax.ShapeDtypeStruct((t, LANES), F32)] * 2,
        compiler_params=_cparams(("parallel",)),
        name="rope_tables",
    )(pos_b, invf, sign)
    return cos.reshape(b, l, LANES), sin.reshape(b, l, LANES)


def _inproj_kernel(*refs, has_delta, pool_w, ssm_w, attn_w, d_model):
    if has_delta:
        x_ref, dl_ref, g_ref, w_ref, cos_ref, sin_ref = refs[:6]
        outs = refs[6:]
    else:
        x_ref, g_ref, w_ref, cos_ref, sin_ref = refs[:5]
        outs = refs[5:]
    pool_ref, ssm_ref, q_ref, k_ref, v_ref, gate_ref = outs[:6]
    x = x_ref[0]
    if has_delta:
        x = x + dl_ref[0]
        outs[6][0] = x
    h = _rmsnorm(x, g_ref[...]).astype(BF16)
    c0 = 0
    z = _dot(h, w_ref[:, c0:c0 + pool_w + ssm_w])
    pool_ref[0] = z[:, :pool_w]
    ssm_ref[...] = z[:, pool_w:]
    c0 += pool_w + ssm_w
    cos = cos_ref[0]
    sin = sin_ref[0]
    nslab = attn_w // LANES
    for dst in (q_ref, k_ref):
        z = _dot(h, w_ref[:, c0:c0 + 2 * attn_w])
        for s in range(nslab):
            a = z[:, s * LANES:(s + 1) * LANES]
            a_sw = z[:, attn_w + s * LANES:attn_w + (s + 1) * LANES]
            dst[0, s] = a * cos + a_sw * sin
        c0 += 2 * attn_w
    z = _dot(h, w_ref[:, c0:c0 + attn_w])
    for s in range(nslab):
        v_ref[0, s] = z[:, s * LANES:(s + 1) * LANES]
    c0 += attn_w
    z = _dot(h, w_ref[:, c0:c0 + 3 * d_model])
    gate_ref[0] = _sigmoid(z).astype(BF16)


def _in_projection(x, delta, g, w_ext, cos, sin, *, pool_w, ssm_w, attn_w):
    b, l, d = x.shape
    tm = 256
    nslab = attn_w // LANES
    has_delta = delta is not None
    kern = functools.partial(_inproj_kernel, has_delta=has_delta, pool_w=pool_w, ssm_w=ssm_w,
                             attn_w=attn_w, d_model=d)
    row = lambda bi, i: (bi, i, 0)
    slab = lambda bi, i: (bi, 0, i, 0)
    const2 = lambda bi, i: (0, 0)
    in_specs = [pl.BlockSpec((1, tm, d), row)]
    args = [x]
    if has_delta:
        in_specs.append(pl.BlockSpec((1, tm, d), row))
        args.append(delta)
    in_specs += [pl.BlockSpec((1, d), const2),
                 pl.BlockSpec(w_ext.shape, const2, pipeline_mode=pl.Buffered(1)),
                 pl.BlockSpec((1, tm, LANES), row),
                 pl.BlockSpec((1, tm, LANES), row)]
    args += [g.reshape(1, d), w_ext, cos, sin]
    out_specs = [pl.BlockSpec((1, tm, pool_w), row),
                 pl.BlockSpec((tm, ssm_w), lambda bi, i: (i, bi)),
                 pl.BlockSpec((1, nslab, tm, LANES), slab),
                 pl.BlockSpec((1, nslab, tm, LANES), slab),
                 pl.BlockSpec((1, nslab, tm, LANES), slab),
                 pl.BlockSpec((1, tm, 3 * d), row)]
    out_shape = [jax.ShapeDtypeStruct((b, l, pool_w), F32),
                 jax.ShapeDtypeStruct((l, b * ssm_w), F32),
                 jax.ShapeDtypeStruct((b, nslab, l, LANES), F32),
                 jax.ShapeDtypeStruct((b, nslab, l, LANES), F32),
                 jax.ShapeDtypeStruct((b, nslab, l, LANES), F32),
                 jax.ShapeDtypeStruct((b, l, 3 * d), BF16)]
    if has_delta:
        out_specs.append(pl.BlockSpec((1, tm, d), row))
        out_shape.append(jax.ShapeDtypeStruct((b, l, d), F32))
    return pl.pallas_call(
        kern, grid=(b, l // tm), in_specs=in_specs, out_specs=out_specs, out_shape=out_shape,
        compiler_params=_cparams(("parallel", "parallel")), name="in_projection",
    )(*args)


def _pool_kernel(u_ref, halo_ref, w_ref, scale_ref, o_ref, *, tm, halo):
    i = pl.program_id(1)
    cur = u_ref[0]
    prev = jnp.where(i > 0, halo_ref[0], 0.0)
    ext = jnp.concatenate([prev, cur], axis=0)
    t_idx = i * tm + lax.broadcasted_iota(jnp.int32, (tm, POOL_GROUP), 0)
    for gi, win in enumerate(POOL_WINDOWS):
        cs = slice(gi * POOL_GROUP, (gi + 1) * POOL_GROUP)
        s = ext[:, cs]
        sh = 1
        while sh < win:
            s = s + pltpu.roll(s, sh, axis=0)
            sh *= 2
        count = jnp.minimum(t_idx + 1, win).astype(F32)
        pooled = s[halo:, :] / count - cur[:, cs]
        mixed = _dot(pooled.astype(BF16), w_ref[gi]) * scale_ref[:, cs]
        o_ref[0, :, cs] = mixed.astype(BF16)


def _pool_mixer(u, w, scale):
    b, l, c = u.shape
    tm, halo = 512, 16
    assert halo >= max(POOL_WINDOWS) and halo % SUBLANES == 0
    kern = functools.partial(_pool_kernel, tm=tm, halo=halo)
    return pl.pallas_call(
        kern, grid=(b, l // tm),
        in_specs=[pl.BlockSpec((1, tm, c), lambda bi, i: (bi, i, 0)),
                  pl.BlockSpec((1, halo, c), lambda bi, i: (bi, jnp.maximum(i * (tm // halo) - 1, 0), 0)),
                  pl.BlockSpec(w.shape, lambda bi, i: (0, 0, 0)),
                  pl.BlockSpec((1, c), lambda bi, i: (0, 0))],
        out_specs=pl.BlockSpec((1, tm, c), lambda bi, i: (bi, i, 0)),
        out_shape=jax.ShapeDtypeStruct((b, l, c), BF16),
        compiler_params=_cparams(("parallel", "parallel")), name="pool_mixer",
    )(u, u, w, scale.reshape(1, c))


def _ssm_param_kernel(are_ref, aim_ref, ldt_ref, bre_ref, bim_ref, oar_ref, oai_ref, obr_ref, obi_ref):
    ar = are_ref[...]
    ai = aim_ref[...]
    dt = jnp.exp(ldt_ref[...])
    decay = jnp.exp(ar * dt)
    abar_re = decay * jnp.cos(ai * dt)
    abar_im = decay * jnp.sin(ai * dt)
    den = ar * ar + ai * ai
    nr = abar_re - 1.0
    k_re = (nr * ar + abar_im * ai) / den
    k_im = (abar_im * ar - nr * ai) / den
    oar_ref[...] = abar_re
    oai_ref[...] = abar_im
    for gi in range(are_ref.shape[0]):
        kr = k_re[gi:gi + 1, :]
        ki = k_im[gi:gi + 1, :]
        br = bre_ref[gi]
        bi = bim_ref[gi]
        obr_ref[gi] = kr * br - ki * bi
        obi_ref[gi] = kr * bi + ki * br


def _ssm_params(a_re, a_im, log_dt, b_re, b_im):
    g, n, p = b_re.shape
    brt = jnp.transpose(b_re, (0, 2, 1))
    bit = jnp.transpose(b_im, (0, 2, 1))
    return pl.pallas_call(
        _ssm_param_kernel,
        out_shape=[jax.ShapeDtypeStruct((g, n), F32)] * 2 + [jax.ShapeDtypeStruct((g, p, n), F32)] * 2,
        name="ssm_params",
    )(a_re, a_im, log_dt.reshape(g, 1), brt, bit)


def _block_diag(m):
    k, r, c = m.shape
    eye = jnp.eye(k, dtype=m.dtype)
    return (m[:, :, None, :] * eye[:, None, :, None]).reshape(k * r, k * c)


def _ssm_kernel(u_ref, bre_ref, bim_ref, are_ref, aim_ref, cre_ref, cim_ref, d_ref, y_ref,
                st_ref, xre_ref, xim_ref, sre_ref, sim_ref, *, tc, hw):
    rows = tc * SUBLANES

    @pl.when(pl.program_id(0) == 0)
    def _():
        st_ref[...] = jnp.zeros_like(st_ref)

    u = u_ref[...]
    odd = (lax.broadcasted_iota(jnp.int32, (rows, hw), 0) % 2) == 1
    ub = u.astype(BF16)
    zero = jnp.zeros_like(ub)
    u_exp = jnp.concatenate([jnp.where(odd, zero, ub), jnp.where(odd, ub, zero)], axis=1)
    xre_ref[...] = _dot(u_exp, bre_ref[...])
    xim_ref[...] = _dot(u_exp, bim_ref[...])
    a_re = are_ref[...]
    a_im = aim_ref[...]

    def step(t, carry):
        s_re, s_im = carry
        r0 = pl.multiple_of(t * SUBLANES, SUBLANES)
        n_re = a_re * s_re - a_im * s_im + xre_ref[pl.ds(r0, SUBLANES), :]
        n_im = a_re * s_im + a_im * s_re + xim_ref[pl.ds(r0, SUBLANES), :]
        sre_ref[pl.ds(r0, SUBLANES), :] = n_re
        sim_ref[pl.ds(r0, SUBLANES), :] = n_im
        return n_re, n_im

    s_re, s_im = lax.fori_loop(0, tc, step, (st_ref[0], st_ref[1]), unroll=8)
    st_ref[0] = s_re
    st_ref[1] = s_im
    yf = _dot(sre_ref[...].astype(BF16), cre_ref[...]) - _dot(sim_ref[...].astype(BF16), cim_ref[...])
    y = jnp.where(odd, yf[:, hw:], yf[:, :hw]) + pltpu.repeat(d_ref[...], tc, axis=0) * u
    y_ref[...] = _gelu(y).astype(BF16)


def _ssm_mixer(u_scan, batch, abar_re, abar_im, bbar_re_t, bbar_im_t, c_re, c_im, d_skip):
    l = u_scan.shape[0]
    g, p, n = bbar_re_t.shape
    hg = g // 2
    hw = hg * p
    hs = hg * n
    assert batch * 2 == SUBLANES
    rows_total = l * SUBLANES
    u2 = u_scan.reshape(rows_total, hw)

    def stack_b(bt):
        return jnp.concatenate([_block_diag(bt[:hg]), _block_diag(bt[hg:])], axis=0).astype(BF16)

    def cat_c(c):
        ct = jnp.transpose(c, (0, 2, 1))
        return jnp.concatenate([_block_diag(ct[:hg]), _block_diag(ct[hg:])], axis=1).astype(BF16)

    def tile_a(a):
        return jnp.tile(a.reshape(2, hs), (batch, 1))

    d8 = jnp.tile(d_skip.reshape(2, hw), (batch, 1))
    tc = 64
    rows = tc * SUBLANES
    kern = functools.partial(_ssm_kernel, tc=tc, hw=hw)
    const = lambda i: (0, 0)
    y = pl.pallas_call(
        kern, grid=(l // tc,),
        in_specs=[pl.BlockSpec((rows, hw), lambda i: (i, 0)),
                  pl.BlockSpec((2 * hw, hs), const), pl.BlockSpec((2 * hw, hs), const),
                  pl.BlockSpec((SUBLANES, hs), const), pl.BlockSpec((SUBLANES, hs), const),
                  pl.BlockSpec((hs, 2 * hw), const), pl.BlockSpec((hs, 2 * hw), const),
                  pl.BlockSpec((SUBLANES, hw), const)],
        out_specs=pl.BlockSpec((rows, hw), lambda i: (i, 0)),
        out_shape=jax.ShapeDtypeStruct((rows_total, hw), BF16),
        scratch_shapes=[pltpu.VMEM((2, SUBLANES, hs), F32),
                        pltpu.VMEM((rows, hs), F32), pltpu.VMEM((rows, hs), F32),
                        pltpu.VMEM((rows, hs), F32), pltpu.VMEM((rows, hs), F32)],
        compiler_params=_cparams(("arbitrary",)), name="ssm_mixer",
    )(u2, stack_b(bbar_re_t), stack_b(bbar_im_t), tile_a(abar_re), tile_a(abar_im),
      cat_c(c_re), cat_c(c_im), d8)
    return y.reshape(l, batch * 2 * hw)


def _attn_group(q_ref, k_ref, v_ref, kb, vb, acc, mrun, drun, *, d, ta, i):
    p = ATTN_BLOCK * d
    ncombo = ta // ATTN_BLOCK
    nslab = q_ref.shape[1]
    scale = HEAD_DIM ** -0.5
    kb[:, p:p + ta, :] = k_ref[0]
    vb[:, p:p + ta, :] = v_ref[0]

    @pl.when(i == 0)
    def _():
        kb[:, :p, :] = jnp.zeros((nslab, p, LANES), F32)
        vb[:, :p, :] = jnp.zeros((nslab, p, LANES), F32)

    qi = lax.broadcasted_iota(jnp.int32, (ATTN_BLOCK, ATTN_BLOCK), 0)
    kk = lax.broadcasted_iota(jnp.int32, (ATTN_BLOCK, ATTN_BLOCK), 1)
    mask_cur = kk <= qi
    mask_prev = kk >= qi
    low = kk < HEAD_DIM

    def rows_at(start):
        return pl.ds(start, ATTN_BLOCK) if d == 1 else pl.ds(start, ATTN_BLOCK, stride=d)

    def combo(c, carry):
        r = c % d
        n = c // d
        qs = n * p + r
        has_prev = jnp.logical_or(i > 0, n > 0)
        mask_p = jnp.logical_and(mask_prev, has_prev)
        for s in range(nslab):
            q = q_ref[0, s, rows_at(qs), :] * scale
            kc = kb[s, rows_at(p + qs), :].astype(BF16)
            kp = kb[s, rows_at(qs), :].astype(BF16)
            vc = vb[s, rows_at(p + qs), :].astype(BF16)
            vp = vb[s, rows_at(qs), :].astype(BF16)
            parts = []
            for hh in range(LANES // HEAD_DIM):
                mine = low if hh == 0 else jnp.logical_not(low)
                qh = jnp.where(mine, q, 0.0).astype(BF16)
                sc = jnp.where(mask_cur, _dot_nt(qh, kc), -jnp.inf)
                sp = jnp.where(mask_p, _dot_nt(qh, kp), -jnp.inf)
                m = jnp.maximum(jnp.max(sc, axis=1, keepdims=True), jnp.max(sp, axis=1, keepdims=True))
                pc = jnp.exp(sc - m)
                pp = jnp.exp(sp - m)
                den = jnp.sum(pc, axis=1, keepdims=True) + jnp.sum(pp, axis=1, keepdims=True)
                o = _dot(pc.astype(BF16), vc) + _dot(pp.astype(BF16), vp)
                parts.append((m, den, o))
            m_blk = jnp.where(low, parts[0][0], parts[1][0])
            l_blk = jnp.where(low, parts[0][1], parts[1][1])
            o_blk = jnp.where(low, parts[0][2], parts[1][2])
            rows = rows_at(qs)
            m_old = mrun[s, rows, :]
            m_new = jnp.maximum(m_old, m_blk)
            a_old = jnp.exp(m_old - m_new)
            a_blk = jnp.exp(m_blk - m_new)
            acc[s, rows, :] = acc[s, rows, :] * a_old + o_blk * a_blk
            drun[s, rows, :] = drun[s, rows, :] * a_old + l_blk * a_blk
            mrun[s, rows, :] = m_new
        return carry

    lax.fori_loop(0, ncombo, combo, 0)
    kb[:, :p, :] = kb[:, ta:ta + p, :]
    vb[:, :p, :] = vb[:, ta:ta + p, :]


def _attn_kernel(q_ref, k_ref, v_ref, o_ref, *scratch, ta):
    ng = len(ATTN_DILATIONS)
    kbs = scratch[0:2 * ng:2]
    vbs = scratch[1:2 * ng:2]
    acc, mrun, drun = scratch[2 * ng:]
    i = pl.program_id(1)
    g = pl.program_id(2)

    @pl.when(g == 0)
    def _():
        acc[...] = jnp.zeros_like(acc)
        drun[...] = jnp.zeros_like(drun)
        mrun[...] = jnp.full(mrun.shape, NEG_BIG, F32)

    for gi, d in enumerate(ATTN_DILATIONS):
        @pl.when(g == gi)
        def _(gi=gi, d=d):
            _attn_group(q_ref, k_ref, v_ref, kbs[gi], vbs[gi], acc, mrun, drun, d=d, ta=ta, i=i)

    @pl.when(g == ng - 1)
    def _():
        for s in range(acc.shape[0]):
            o_ref[0, :, s * LANES:(s + 1) * LANES] = acc[s] / drun[s]


def _attention(q, k, v):
    b, nslab, l, _ = q.shape
    ng = len(ATTN_DILATIONS)
    gs = nslab // ng
    ta = ATTN_BLOCK * max(ATTN_DILATIONS)
    assert l % ta == 0
    blk = pl.BlockSpec((1, gs, ta, LANES), lambda bi, i, g: (bi, g, i, 0))
    scratch = []
    for d in ATTN_DILATIONS:
        scratch += [pltpu.VMEM((gs, ATTN_BLOCK * d + ta, LANES), F32)] * 2
    scratch += [pltpu.VMEM((gs, ta, LANES), F32)] * 3
    return pl.pallas_call(
        functools.partial(_attn_kernel, ta=ta), grid=(b, l // ta, ng),
        in_specs=[blk, blk, blk],
        out_specs=pl.BlockSpec((1, ta, gs * LANES), lambda bi, i, g: (bi, i, 0)),
        out_shape=jax.ShapeDtypeStruct((b, l, gs * LANES), F32),
        scratch_shapes=scratch,
        compiler_params=_cparams(("arbitrary", "arbitrary", "arbitrary")), name="dilated_attention",
    )(q, k, v)


def _merge_kernel(pm_ref, ys_ref, ao_ref, gate_ref, x_ref, wp_ref, wg_ref, ws_ref, wa_ref, wo_ref, g2_ref,
                  xo_ref, h2_ref, *, d_model, ssm_w):
    y_pool = _dot(pm_ref[0], wp_ref[...])
    glu = _dot(ys_ref[...], wg_ref[...])
    sg = (glu[:, :ssm_w] * _sigmoid(glu[:, ssm_w:])).astype(BF16)
    y_ssm = _dot(sg, ws_ref[...])
    y_attn = _dot(ao_ref[0].astype(BF16), wa_ref[...])
    gate = gate_ref[0].astype(F32)
    merged = (gate[:, :d_model] * y_pool + gate[:, d_model:2 * d_model] * y_ssm
              + gate[:, 2 * d_model:] * y_attn)
    xn = x_ref[0] + _dot(merged.astype(BF16), wo_ref[...])
    xo_ref[0] = xn
    h2_ref[0] = _rmsnorm(xn, g2_ref[...]).astype(BF16)


def _merge(pm, ys, ao, gates, x, w_pool, w_glu, w_ssm, w_attn, w_out, g2):
    b, l, d = x.shape
    tm = 256
    ssm_w = w_glu.shape[0]
    row = lambda bi, i: (bi, i, 0)
    const = lambda bi, i: (0, 0)
    full = lambda a: pl.BlockSpec(a.shape, const)
    return pl.pallas_call(
        functools.partial(_merge_kernel, d_model=d, ssm_w=ssm_w), grid=(b, l // tm),
        in_specs=[pl.BlockSpec((1, tm, pm.shape[-1]), row),
                  pl.BlockSpec((tm, ssm_w), lambda bi, i: (i, bi)),
                  pl.BlockSpec((1, tm, ao.shape[-1]), row),
                  pl.BlockSpec((1, tm, 3 * d), row),
                  pl.BlockSpec((1, tm, d), row),
                  full(w_pool), full(w_glu), full(w_ssm), full(w_attn), full(w_out),
                  pl.BlockSpec((1, d), const)],
        out_specs=[pl.BlockSpec((1, tm, d), row), pl.BlockSpec((1, tm, d), row)],
        out_shape=[jax.ShapeDtypeStruct((b, l, d), F32), jax.ShapeDtypeStruct((b, l, d), BF16)],
        compiler_params=_cparams(("parallel", "parallel")), name="branch_merge",
    )(pm, ys, ao, gates, x, w_pool, w_glu, w_ssm, w_attn, w_out, g2.reshape(1, d))


def _candidate_pairs(k):
    return [(a, b) for a in range(k) for b in range(k) if (a + 1) * (b + 1) <= k]


def _topk_round(work, rows):
    m = jnp.max(work, axis=0, keepdims=True)
    idx = jnp.min(jnp.where(work == m, rows, float(work.shape[0])), axis=0, keepdims=True)
    hit = rows == idx
    return m, hit, jnp.where(hit, -jnp.inf, work)


def _route_kernel(h_ref, wq_ref, k1_ref, k2_ref, r2_ref, e2_ref, n1_ref, e1_ref,
                  qt_ref, s1_ref, s2_ref, v1_ref, v2_ref, cand_ref, sel_ref, *, nheads, nkeys, topk, tt):
    qt_ref[...] = _dot_nt(wq_ref[...], h_ref[...])
    hc = k1_ref.shape[1]
    pairs = _candidate_pairs(topk)
    ncand = cand_ref.shape[0]

    def head(hd, carry):
        base = pl.multiple_of(hd * 2 * hc, 2 * hc)
        q1 = qt_ref[pl.ds(base, hc), :].astype(BF16)
        q2 = qt_ref[pl.ds(base + hc, hc), :].astype(BF16)
        s1_ref[...] = _dot(k1_ref[...], q1)
        s2_ref[...] = _dot(k2_ref[...], q2)

        def col(j, carry2):
            c0 = pl.multiple_of(j * LANES, LANES)
            cols = pl.ds(c0, LANES)
            s1 = s1_ref[:, cols]
            s2 = s2_ref[:, cols]
            key_rows = lax.broadcasted_iota(jnp.int32, s1.shape, 0).astype(F32)
            w1, w2 = s1, s2
            rank1 = jnp.full(s1.shape, float(nkeys), F32)
            rank2 = rank1
            for r in range(topk):
                m1, hit1, w1 = _topk_round(w1, key_rows)
                m2, hit2, w2 = _topk_round(w2, key_rows)
                rank1 = jnp.where(hit1, float(r), rank1)
                rank2 = jnp.where(hit2, float(r), rank2)
                v1_ref[r:r + 1, :] = m1
                v2_ref[r:r + 1, :] = m2
            for ci, (a, bq) in enumerate(pairs):
                cand_ref[ci:ci + 1, :] = v1_ref[a:a + 1, :] + v2_ref[bq:bq + 1, :]
            if ncand > len(pairs):
                cand_ref[len(pairs):, :] = jnp.full((ncand - len(pairs), LANES), -jnp.inf, F32)
            cv = cand_ref[...]
            cand_rows = lax.broadcasted_iota(jnp.int32, cv.shape, 0).astype(F32)
            wc = cv
            sel = jnp.zeros(cv.shape, F32)
            for r in range(topk):
                _, hit, wc = _topk_round(wc, cand_rows)
                sel = jnp.where(hit, 1.0, sel)
            sel_ref[...] = sel
            top = v1_ref[0:1, :] + v2_ref[0:1, :]
            z = jnp.sum(sel * jnp.exp(jnp.where(sel > 0.0, cv, top) - top), axis=0, keepdims=True)
            n1 = jnp.zeros(s1.shape, F32)
            off = 0
            for a in range(topk):
                cnt = topk // (a + 1)
                n_a = jnp.sum(sel_ref[off:off + cnt, :], axis=0, keepdims=True)
                n1 = jnp.where(rank1 == float(a), n_a, n1)
                off += cnt
            n1_ref[hd, :, cols] = n1
            e1_ref[hd, :, cols] = jnp.exp(s1 - v1_ref[0:1, :]) / z
            r2_ref[hd, j] = pltpu.bitcast(rank2.astype(BF16), jnp.uint32)
            e2_ref[hd, j] = pltpu.bitcast(jnp.exp(s2 - v2_ref[0:1, :]).astype(BF16), jnp.uint32)
            return carry2

        lax.fori_loop(0, tt // LANES, col, 0)
        return carry

    lax.fori_loop(0, nheads, head, 0)


def _peer_route(h2, wq_t, k1, k2):
    t, d = h2.shape
    nkeys, hc = k1.shape
    nheads = wq_t.shape[0] // (2 * hc)
    tt = 512
    npairs = len(_candidate_pairs(PEER_TOPK))
    ncand = -(-npairs // SUBLANES) * SUBLANES
    kern = functools.partial(_route_kernel, nheads=nheads, nkeys=nkeys, topk=PEER_TOPK, tt=tt)
    kblk = pl.BlockSpec((nheads, tt // LANES, nkeys // 2, LANES), lambda i: (0, i, 0, 0))
    ksh = jax.ShapeDtypeStruct((nheads, t // LANES, nkeys // 2, LANES), jnp.uint32)
    oblk = pl.BlockSpec((nheads, nkeys, tt), lambda i: (0, 0, i))
    osh = jax.ShapeDtypeStruct((nheads, nkeys, t), F32)
    return pl.pallas_call(
        kern, grid=(t // tt,),
        in_specs=[pl.BlockSpec((tt, d), lambda i: (i, 0)),
                  pl.BlockSpec(wq_t.shape, lambda i: (0, 0)),
                  pl.BlockSpec(k1.shape, lambda i: (0, 0)),
                  pl.BlockSpec(k2.shape, lambda i: (0, 0))],
        out_specs=[kblk, kblk, oblk, oblk], out_shape=[ksh, ksh, osh, osh],
        scratch_shapes=[pltpu.VMEM((wq_t.shape[0], tt), F32),
                        pltpu.VMEM((nkeys, tt), F32), pltpu.VMEM((nkeys, tt), F32),
                        pltpu.VMEM((PEER_TOPK, LANES), F32), pltpu.VMEM((PEER_TOPK, LANES), F32),
                        pltpu.VMEM((ncand, LANES), F32), pltpu.VMEM((ncand, LANES), F32)],
        compiler_params=_cparams(("parallel",)), name="peer_route",
    )(h2, wq_t, k1, k2)


def _ffn_kernel(h_ref, u_ref, vt_ref, r2_ref, e2_ref, n1_ref, e1_ref, o_ref, acc_ref, at_ref, pt_ref,
                *, nheads, nkeys, cw):
    j = pl.program_id(1)
    tt = h_ref.shape[0]
    et = u_ref.shape[0]
    nsub = et // nkeys
    nhalf = cw // LANES

    @pl.when(j == 0)
    def _():
        acc_ref[...] = jnp.zeros_like(acc_ref)

    ktiles = nkeys // BF16_ROWS

    def pre_activations(c):
        at_ref[c] = _dot_nt(u_ref[...], h_ref[c * cw:(c + 1) * cw, :])

    pre_activations(0)
    for c in range(tt // cw):
        if c + 1 < tt // cw:
            pre_activations(c + 1)
        for jj in range(nsub):
            tiles = slice(jj * ktiles, (jj + 1) * ktiles)
            for hf in range(nhalf):
                ck = c * nhalf + hf
                cols = slice(ck * LANES, (ck + 1) * LANES)
                lanes = slice(hf * LANES, (hf + 1) * LANES)
                gate = jnp.zeros((ktiles, BF16_ROWS, LANES), BF16)
                for hd in range(nheads):
                    n_row = jnp.broadcast_to(n1_ref[hd, jj:jj + 1, cols], (BF16_ROWS, LANES)).astype(BF16)
                    e_row = jnp.broadcast_to(e1_ref[hd, jj:jj + 1, cols], (BF16_ROWS, LANES)).astype(BF16)
                    r2 = pltpu.bitcast(r2_ref[hd, ck], BF16).reshape(ktiles, BF16_ROWS, LANES)
                    e2 = pltpu.bitcast(e2_ref[hd, ck], BF16).reshape(ktiles, BF16_ROWS, LANES)
                    gate = gate + jnp.where(r2 < n_row[None], e2, 0) * e_row[None]
                a = at_ref[c, jj * nkeys:(jj + 1) * nkeys, lanes].astype(BF16)
                pt_ref[c, tiles, :, lanes] = gate * _gelu(a.reshape(ktiles, BF16_ROWS, LANES))
        p = pt_ref[c].reshape(et, cw)
        acc_ref[c] += _dot(vt_ref[...], p)

    @pl.when(j == pl.num_programs(1) - 1)
    def _():
        for c in range(tt // cw):
            o_ref[c * cw:(c + 1) * cw, :] = acc_ref[c].T


def _peer_experts(h2, u_tab, v_tab_t, r2, e2, n1, e1):
    t, d = h2.shape
    ne = u_tab.shape[0]
    nheads, nkeys, _ = n1.shape
    tt, et, cw = 1024, 1024, 256
    nsub = et // nkeys
    assert nsub == SUBLANES
    kern = functools.partial(_ffn_kernel, nheads=nheads, nkeys=nkeys, cw=cw)
    full_keys = pl.BlockSpec((nheads, tt // LANES, nkeys // 2, LANES), lambda i, j: (0, i, 0, 0))
    sub_keys = pl.BlockSpec((nheads, nsub, tt), lambda i, j: (0, j, i))
    return pl.pallas_call(
        kern, grid=(t // tt, ne // et),
        in_specs=[pl.BlockSpec((tt, d), lambda i, j: (i, 0)),
                  pl.BlockSpec((et, d), lambda i, j: (j, 0)),
                  pl.BlockSpec((d, et), lambda i, j: (0, j)),
                  full_keys, full_keys, sub_keys, sub_keys],
        out_specs=pl.BlockSpec((tt, d), lambda i, j: (i, 0)),
        out_shape=jax.ShapeDtypeStruct((t, d), F32),
        scratch_shapes=[pltpu.VMEM((tt // cw, d, cw), F32),
                        pltpu.VMEM((tt // cw, et, cw), F32),
                        pltpu.VMEM((tt // cw, et // BF16_ROWS, BF16_ROWS, cw), BF16)],
        compiler_params=_cparams(("parallel", "arbitrary")), name="peer_experts",
    )(h2, u_tab, v_tab_t, r2, e2, n1, e1)


def _final_kernel(x_ref, dl_ref, g_ref, o_ref):
    o_ref[...] = _rmsnorm(x_ref[...] + dl_ref[...], g_ref[...])


def _final_norm(x, delta, g):
    t, d = x.shape
    tm = 1024
    blk = pl.BlockSpec((tm, d), lambda i: (i, 0))
    return pl.pallas_call(
        _final_kernel, grid=(t // tm,),
        in_specs=[blk, blk, pl.BlockSpec((1, d), lambda i: (0, 0))],
        out_specs=blk, out_shape=jax.ShapeDtypeStruct((t, d), F32),
        compiler_params=_cparams(("parallel",)), name="final_norm",
    )(x, delta, g.reshape(1, d))


def _swap_halves(w):
    d, c = w.shape
    half = HEAD_DIM // 2
    return w.reshape(d, c // HEAD_DIM, 2, half)[:, :, ::-1, :].reshape(d, c)


def kernel(x, positions, norm1_g, w_in, pool_w, pool_scale, pool_proj, ssm_a_re, ssm_a_im, ssm_log_dt,
           ssm_b_re, ssm_b_im, ssm_c_re, ssm_c_im, ssm_d, ssm_glu, ssm_proj, attn_proj, w_out, norm2_g,
           peer_wq, peer_k1, peer_k2, peer_u, peer_v, final_g):
    b, l, d = x.shape
    depth = w_in.shape[0]
    pool_w_cols = pool_proj.shape[1]
    ssm_w_cols = ssm_proj.shape[1]
    attn_w = len(ATTN_DILATIONS) * HEADS_PER_GROUP * HEAD_DIM
    cos, sin = _rope_tables(positions)
    delta = None
    for layer in range(depth):
        w = w_in[layer]
        o0 = pool_w_cols + ssm_w_cols
        wq, wk, wv, wg = (w[:, o0:o0 + attn_w], w[:, o0 + attn_w:o0 + 2 * attn_w],
                          w[:, o0 + 2 * attn_w:o0 + 3 * attn_w], w[:, o0 + 3 * attn_w:])
        w_ext = jnp.concatenate([w[:, :o0], wq, _swap_halves(wq), wk, _swap_halves(wk), wv, wg],
                                axis=1).astype(BF16)
        outs = _in_projection(x, delta, norm1_g[layer], w_ext, cos, sin,
                              pool_w=pool_w_cols, ssm_w=ssm_w_cols, attn_w=attn_w)
        u_pool, u_ssm, q, k, v, gates = outs[:6]
        if delta is not None:
            x = outs[6]
        pm = _pool_mixer(u_pool, pool_w[layer].astype(BF16), pool_scale[layer])
        abar_re, abar_im, bbar_re_t, bbar_im_t = _ssm_params(
            ssm_a_re[layer], ssm_a_im[layer], ssm_log_dt[layer], ssm_b_re[layer], ssm_b_im[layer])
        ys = _ssm_mixer(u_ssm, b, abar_re, abar_im, bbar_re_t, bbar_im_t,
                        ssm_c_re[layer], ssm_c_im[layer], ssm_d[layer])
        ao = _attention(q, k, v)
        x, h2 = _merge(pm, ys, ao, gates, x, pool_proj[layer].astype(BF16), ssm_glu[layer].astype(BF16),
                       ssm_proj[layer].astype(BF16), attn_proj[layer].astype(BF16),
                       w_out[layer].astype(BF16), norm2_g[layer])
        h2f = h2.reshape(b * l, d)
        r2, e2, n1, e1 = _peer_route(h2f, peer_wq[layer].T.astype(BF16), peer_k1[layer].astype(BF16),
                                     peer_k2[layer].astype(BF16))
        delta = _peer_experts(h2f, peer_u[layer].astype(BF16), peer_v[layer].T.astype(BF16),
                              r2, e2, n1, e1).reshape(b, l, d)
    return _final_norm(x.reshape(b * l, d), delta.reshape(b * l, d), final_g).reshape(b, l, d)
```

```python
import functools
import math

import jax
import jax.numpy as jnp
from jax import lax
from jax.experimental import pallas as pl
from jax.experimental.pallas import tpu as pltpu

F32 = jnp.float32
BF16 = jnp.bfloat16

EPS = 1e-6
POOL_WINDOWS = (2, 4, 8, 16)
POOL_GROUP = 128
SSM_GROUP = 16
SSM_GROUPS = 32
SSM_STATE = 64
HEAD_DIM = 64
HEADS_PER_GROUP = 4
ATTN_DILATIONS = (1, 4, 16)
ATTN_BLOCK = 128
ROPE_THETA = 10000.0
PEER_HEADS = 8
PEER_TOPK = 16

LANES = 128
SUBLANES = 8
BF16_ROWS = 16
VMEM_LIMIT = 56 * 1024 * 1024
NEG_BIG = -1e30


def _cparams(sem):
    return pltpu.CompilerParams(dimension_semantics=sem, vmem_limit_bytes=VMEM_LIMIT)


def _gelu(x):
    c = math.sqrt(2.0 / math.pi)
    return x * (0.5 * (1.0 + jnp.tanh(c * (x + 0.044715 * (x * x * x)))))


def _sigmoid(x):
    return 1.0 / (1.0 + jnp.exp(-x))


def _rmsnorm(x, g):
    ms = jnp.mean(x * x, axis=-1, keepdims=True)
    return x * lax.rsqrt(ms + EPS) * g


def _dot(a, b):
    return jnp.dot(a, b, preferred_element_type=F32)


def _dot_nt(a, b):
    return lax.dot_general(a, b, (((1,), (1,)), ((), ())), preferred_element_type=F32)


def _rope_kernel(pos_ref, invf_ref, sign_ref, cos_ref, sin_ref):
    ang = pos_ref[...].astype(F32) * invf_ref[...]
    cos_ref[...] = jnp.cos(ang)
    sin_ref[...] = jnp.sin(ang) * sign_ref[...]


def _rope_tables(positions):
    b, l = positions.shape
    t = b * l
    half = HEAD_DIM // 2
    inv_freq = ROPE_THETA ** (-jnp.arange(half, dtype=F32) / half)
    invf = jnp.tile(inv_freq, LANES // half).reshape(1, LANES)
    sign = jnp.tile(jnp.concatenate([-jnp.ones((half,), F32), jnp.ones((half,), F32)]),
                    LANES // HEAD_DIM).reshape(1, LANES)
    pos_b = jnp.broadcast_to(positions.reshape(t, 1), (t, LANES))
    tr = 2048
    cos, sin = pl.pallas_call(
        _rope_kernel,
        grid=(t // tr,),
        in_specs=[pl.BlockSpec((tr, LANES), lambda i: (i, 0)),
                  pl.BlockSpec((1, LANES), lambda i: (0, 0)),
                  pl.BlockSpec((1, LANES), lambda i: (0, 0))],
        out_specs=[pl.BlockSpec((tr, LANES), lambda i: (i, 0))] * 2,
        out_shape=[jax.ShapeDtypeStruct((t, LANES), F32)] * 2,
        compiler_params=_cparams(("parallel",)),
        name="rope_tables",
    )(pos_b, invf, sign)
    return cos.reshape(b, l, LANES), sin.reshape(b, l, LANES)


def _inproj_kernel(*refs, has_delta, pool_w, ssm_w, attn_w, d_model):
    if has_delta:
        x_ref, dl_ref, g_ref, w_ref, cos_ref, sin_ref = refs[:6]
        outs = refs[6:]
    else:
        x_ref, g_ref, w_ref, cos_ref, sin_ref = refs[:5]
        outs = refs[5:]
    pool_ref, ssm_ref, q_ref, k_ref, v_ref, gate_ref = outs[:6]
    x = x_ref[0]
    if has_delta:
        x = x + dl_ref[0]
        outs[6][0] = x
    h = _rmsnorm(x, g_ref[...]).astype(BF16)
    c0 = 0
    z = _dot(h, w_ref[:, c0:c0 + pool_w + ssm_w])
    pool_ref[0] = z[:, :pool_w]
    ssm_ref[...] = z[:, pool_w:]
    c0 += pool_w + ssm_w
    cos = cos_ref[0]
    sin = sin_ref[0]
    nslab = attn_w // LANES
    for dst in (q_ref, k_ref):
        z = _dot(h, w_ref[:, c0:c0 + 2 * attn_w])
        for s in range(nslab):
            a = z[:, s * LANES:(s + 1) * LANES]
            a_sw = z[:, attn_w + s * LANES:attn_w + (s + 1) * LANES]
            dst[0, s] = a * cos + a_sw * sin
        c0 += 2 * attn_w
    z = _dot(h, w_ref[:, c0:c0 + attn_w])
    for s in range(nslab):
        v_ref[0, s] = z[:, s * LANES:(s + 1) * LANES]
    c0 += attn_w
    z = _dot(h, w_ref[:, c0:c0 + 3 * d_model])
    gate_ref[0] = _sigmoid(z).astype(BF16)


def _in_projection(x, delta, g, w_ext, cos, sin, *, pool_w, ssm_w, attn_w):
    b, l, d = x.shape
    tm = 256
    nslab = attn_w // LANES
    has_delta = delta is not None
    kern = functools.partial(_inproj_kernel, has_delta=has_delta, pool_w=pool_w, ssm_w=ssm_w,
                             attn_w=attn_w, d_model=d)
    row = lambda bi, i: (bi, i, 0)
    slab = lambda bi, i: (bi, 0, i, 0)
    const2 = lambda bi, i: (0, 0)
    in_specs = [pl.BlockSpec((1, tm, d), row)]
    args = [x]
    if has_delta:
        in_specs.append(pl.BlockSpec((1, tm, d), row))
        args.append(delta)
    in_specs += [pl.BlockSpec((1, d), const2),
                 pl.BlockSpec(w_ext.shape, const2, pipeline_mode=pl.Buffered(1)),
                 pl.BlockSpec((1, tm, LANES), row),
                 pl.BlockSpec((1, tm, LANES), row)]
    args += [g.reshape(1, d), w_ext, cos, sin]
    out_specs = [pl.BlockSpec((1, tm, pool_w), row),
                 pl.BlockSpec((tm, ssm_w), lambda bi, i: (i, bi)),
                 pl.BlockSpec((1, nslab, tm, LANES), slab),
                 pl.BlockSpec((1, nslab, tm, LANES), slab),
                 pl.BlockSpec((1, nslab, tm, LANES), slab),
                 pl.BlockSpec((1, tm, 3 * d), row)]
    out_shape = [jax.ShapeDtypeStruct((b, l, pool_w), F32),
                 jax.ShapeDtypeStruct((l, b * ssm_w), F32),
                 jax.ShapeDtypeStruct((b, nslab, l, LANES), F32),
                 jax.ShapeDtypeStruct((b, nslab, l, LANES), F32),
                 jax.ShapeDtypeStruct((b, nslab, l, LANES), F32),
                 jax.ShapeDtypeStruct((b, l, 3 * d), BF16)]
    if has_delta:
        out_specs.append(pl.BlockSpec((1, tm, d), row))
        out_shape.append(jax.ShapeDtypeStruct((b, l, d), F32))
    return pl.pallas_call(
        kern, grid=(b, l // tm), in_specs=in_specs, out_specs=out_specs, out_shape=out_shape,
        compiler_params=_cparams(("parallel", "parallel")), name="in_projection",
    )(*args)


def _pool_kernel(u_ref, halo_ref, w_ref, scale_ref, o_ref, *, tm, halo):
    i = pl.program_id(1)
    cur = u_ref[0]
    prev = jnp.where(i > 0, halo_ref[0], 0.0)
    ext = jnp.concatenate([prev, cur], axis=0)
    t_idx = i * tm + lax.broadcasted_iota(jnp.int32, (tm, POOL_GROUP), 0)
    for gi, win in enumerate(POOL_WINDOWS):
        cs = slice(gi * POOL_GROUP, (gi + 1) * POOL_GROUP)
        s = ext[:, cs]
        sh = 1
        while sh < win:
            s = s + pltpu.roll(s, sh, axis=0)
            sh *= 2
        count = jnp.minimum(t_idx + 1, win).astype(F32)
        pooled = s[halo:, :] / count - cur[:, cs]
        mixed = _dot(pooled.astype(BF16), w_ref[gi]) * scale_ref[:, cs]
        o_ref[0, :, cs] = mixed.astype(BF16)


def _pool_mixer(u, w, scale):
    b, l, c = u.shape
    tm, halo = 512, 16
    assert halo >= max(POOL_WINDOWS) and halo % SUBLANES == 0
    kern = functools.partial(_pool_kernel, tm=tm, halo=halo)
    return pl.pallas_call(
        kern, grid=(b, l // tm),
        in_specs=[pl.BlockSpec((1, tm, c), lambda bi, i: (bi, i, 0)),
                  pl.BlockSpec((1, halo, c), lambda bi, i: (bi, jnp.maximum(i * (tm // halo) - 1, 0), 0)),
                  pl.BlockSpec(w.shape, lambda bi, i: (0, 0, 0)),
                  pl.BlockSpec((1, c), lambda bi, i: (0, 0))],
        out_specs=pl.BlockSpec((1, tm, c), lambda bi, i: (bi, i, 0)),
        out_shape=jax.ShapeDtypeStruct((b, l, c), BF16),
        compiler_params=_cparams(("parallel", "parallel")), name="pool_mixer",
    )(u, u, w, scale.reshape(1, c))


def _ssm_param_kernel(are_ref, aim_ref, ldt_ref, bre_ref, bim_ref, oar_ref, oai_ref, obr_ref, obi_ref):
    ar = are_ref[...]
    ai = aim_ref[...]
    dt = jnp.exp(ldt_ref[...])
    decay = jnp.exp(ar * dt)
    abar_re = decay * jnp.cos(ai * dt)
    abar_im = decay * jnp.sin(ai * dt)
    den = ar * ar + ai * ai
    nr = abar_re - 1.0
    k_re = (nr * ar + abar_im * ai) / den
    k_im = (abar_im * ar - nr * ai) / den
    oar_ref[...] = abar_re
    oai_ref[...] = abar_im
    for gi in range(are_ref.shape[0]):
        kr = k_re[gi:gi + 1, :]
        ki = k_im[gi:gi + 1, :]
        br = bre_ref[gi]
        bi = bim_ref[gi]
        obr_ref[gi] = kr * br - ki * bi
        obi_ref[gi] = kr * bi + ki * br


def _ssm_params(a_re, a_im, log_dt, b_re, b_im):
    g, n, p = b_re.shape
    brt = jnp.transpose(b_re, (0, 2, 1))
    bit = jnp.transpose(b_im, (0, 2, 1))
    return pl.pallas_call(
        _ssm_param_kernel,
        out_shape=[jax.ShapeDtypeStruct((g, n), F32)] * 2 + [jax.ShapeDtypeStruct((g, p, n), F32)] * 2,
        name="ssm_params",
    )(a_re, a_im, log_dt.reshape(g, 1), brt, bit)


def _block_diag(m):
    k, r, c = m.shape
    eye = jnp.eye(k, dtype=m.dtype)
    return (m[:, :, None, :] * eye[:, None, :, None]).reshape(k * r, k * c)


def _ssm_kernel(u_ref, bre_ref, bim_ref, are_ref, aim_ref, cre_ref, cim_ref, d_ref, y_ref,
                st_ref, xre_ref, xim_ref, sre_ref, sim_ref, *, tc, hw):
    rows = tc * SUBLANES

    @pl.when(pl.program_id(0) == 0)
    def _():
        st_ref[...] = jnp.zeros_like(st_ref)

    u = u_ref[...]
    odd = (lax.broadcasted_iota(jnp.int32, (rows, hw), 0) % 2) == 1
    ub = u.astype(BF16)
    zero = jnp.zeros_like(ub)
    u_exp = jnp.concatenate([jnp.where(odd, zero, ub), jnp.where(odd, ub, zero)], axis=1)
    xre_ref[...] = _dot(u_exp, bre_ref[...])
    xim_ref[...] = _dot(u_exp, bim_ref[...])
    a_re = are_ref[...]
    a_im = aim_ref[...]

    def step(t, carry):
        s_re, s_im = carry
        r0 = pl.multiple_of(t * SUBLANES, SUBLANES)
        n_re = a_re * s_re - a_im * s_im + xre_ref[pl.ds(r0, SUBLANES), :]
        n_im = a_re * s_im + a_im * s_re + xim_ref[pl.ds(r0, SUBLANES), :]
        sre_ref[pl.ds(r0, SUBLANES), :] = n_re
        sim_ref[pl.ds(r0, SUBLANES), :] = n_im
        return n_re, n_im

    s_re, s_im = lax.fori_loop(0, tc, step, (st_ref[0], st_ref[1]), unroll=8)
    st_ref[0] = s_re
    st_ref[1] = s_im
    yf = _dot(sre_ref[...].astype(BF16), cre_ref[...]) - _dot(sim_ref[...].astype(BF16), cim_ref[...])
    y = jnp.where(odd, yf[:, hw:], yf[:, :hw]) + pltpu.repeat(d_ref[...], tc, axis=0) * u
    y_ref[...] = _gelu(y).astype(BF16)


def _ssm_mixer(u_scan, batch, abar_re, abar_im, bbar_re_t, bbar_im_t, c_re, c_im, d_skip):
    l = u_scan.shape[0]
    g, p, n = bbar_re_t.shape
    hg = g // 2
    hw = hg * p
    hs = hg * n
    assert batch * 2 == SUBLANES
    rows_total = l * SUBLANES
    u2 = u_scan.reshape(rows_total, hw)

    def stack_b(bt):
        return jnp.concatenate([_block_diag(bt[:hg]), _block_diag(bt[hg:])], axis=0).astype(BF16)

    def cat_c(c):
        ct = jnp.transpose(c, (0, 2, 1))
        return jnp.concatenate([_block_diag(ct[:hg]), _block_diag(ct[hg:])], axis=1).astype(BF16)

    def tile_a(a):
        return jnp.tile(a.reshape(2, hs), (batch, 1))

    d8 = jnp.tile(d_skip.reshape(2, hw), (batch, 1))
    tc = 64
    rows = tc * SUBLANES
    kern = functools.partial(_ssm_kernel, tc=tc, hw=hw)
    const = lambda i: (0, 0)
    y = pl.pallas_call(
        kern, grid=(l // tc,),
        in_specs=[pl.BlockSpec((rows, hw), lambda i: (i, 0)),
                  pl.BlockSpec((2 * hw, hs), const), pl.BlockSpec((2 * hw, hs), const),
                  pl.BlockSpec((SUBLANES, hs), const), pl.BlockSpec((SUBLANES, hs), const),
                  pl.BlockSpec((hs, 2 * hw), const), pl.BlockSpec((hs, 2 * hw), const),
                  pl.BlockSpec((SUBLANES, hw), const)],
        out_specs=pl.BlockSpec((rows, hw), lambda i: (i, 0)),
        out_shape=jax.ShapeDtypeStruct((rows_total, hw), BF16),
        scratch_shapes=[pltpu.VMEM((2, SUBLANES, hs), F32),
                        pltpu.VMEM((rows, hs), F32), pltpu.VMEM((rows, hs), F32),
                        pltpu.VMEM((rows, hs), F32), pltpu.VMEM((rows, hs), F32)],
        compiler_params=_cparams(("arbitrary",)), name="ssm_mixer",
    )(u2, stack_b(bbar_re_t), stack_b(bbar_im_t), tile_a(abar_re), tile_a(abar_im),
      cat_c(c_re), cat_c(c_im), d8)
    return y.reshape(l, batch * 2 * hw)


def _attn_group(q_ref, k_ref, v_ref, kb, vb, acc, mrun, drun, og, mg, lg, *, d, ta, i):
    p = ATTN_BLOCK * d
    ncombo = ta // ATTN_BLOCK
    nslab = q_ref.shape[1]
    scale = HEAD_DIM ** -0.5
    kb[:, p:p + ta, :] = k_ref[0]
    vb[:, p:p + ta, :] = v_ref[0]

    @pl.when(i == 0)
    def _():
        kb[:, :p, :] = jnp.zeros((nslab, p, LANES), F32)
        vb[:, :p, :] = jnp.zeros((nslab, p, LANES), F32)

    nh = LANES // HEAD_DIM
    qi = lax.broadcasted_iota(jnp.int32, (nh * ATTN_BLOCK, ATTN_BLOCK), 0) % ATTN_BLOCK
    kk = lax.broadcasted_iota(jnp.int32, (nh * ATTN_BLOCK, ATTN_BLOCK), 1)
    mask_cur = kk <= qi
    mask_prev = kk >= qi
    low = lax.broadcasted_iota(jnp.int32, (ATTN_BLOCK, ATTN_BLOCK), 1) < HEAD_DIM

    def rows_at(start):
        return pl.ds(start, ATTN_BLOCK) if d == 1 else pl.ds(start, ATTN_BLOCK, stride=d)

    def combo(c, carry):
        r = c % d
        n = c // d
        qs = n * p + r
        has_prev = jnp.logical_or(i > 0, n > 0)
        mask_p = jnp.logical_and(mask_prev, has_prev)
        for s in range(nslab):
            q = q_ref[0, s, rows_at(qs), :] * scale
            kc = kb[s, rows_at(p + qs), :].astype(BF16)
            kp = kb[s, rows_at(qs), :].astype(BF16)
            vc = vb[s, rows_at(p + qs), :].astype(BF16)
            vp = vb[s, rows_at(qs), :].astype(BF16)
            qh = jnp.concatenate([jnp.where(low, q, 0.0), jnp.where(low, 0.0, q)], axis=0).astype(BF16)
            sc = jnp.where(mask_cur, _dot_nt(qh, kc), -jnp.inf)
            sp = jnp.where(mask_p, _dot_nt(qh, kp), -jnp.inf)
            m = jnp.maximum(jnp.max(sc, axis=1, keepdims=True), jnp.max(sp, axis=1, keepdims=True))
            pc = jnp.exp(sc - m)
            pp = jnp.exp(sp - m)
            den = jnp.sum(pc, axis=1, keepdims=True) + jnp.sum(pp, axis=1, keepdims=True)
            o = _dot(pc.astype(BF16), vc) + _dot(pp.astype(BF16), vp)
            rows = rows_at(qs)
            og[s, rows, :] = jnp.where(low, o[:ATTN_BLOCK], o[ATTN_BLOCK:])
            mg[s, rows, :] = jnp.where(low, m[:ATTN_BLOCK], m[ATTN_BLOCK:])
            lg[s, rows, :] = jnp.where(low, den[:ATTN_BLOCK], den[ATTN_BLOCK:])
        return carry

    lax.fori_loop(0, ncombo, combo, 0)
    kb[:, :p, :] = kb[:, ta:ta + p, :]
    vb[:, :p, :] = vb[:, ta:ta + p, :]

    rc = 256
    for s in range(nslab):
        for r0 in range(0, ta, rc):
            rs = slice(r0, r0 + rc)
            m_old = mrun[s, rs, :]
            m_blk = mg[s, rs, :]
            m_new = jnp.maximum(m_old, m_blk)
            a_old = jnp.exp(m_old - m_new)
            a_blk = jnp.exp(m_blk - m_new)
            acc[s, rs, :] = acc[s, rs, :] * a_old + og[s, rs, :] * a_blk
            drun[s, rs, :] = drun[s, rs, :] * a_old + lg[s, rs, :] * a_blk
            mrun[s, rs, :] = m_new


def _attn_kernel(q_ref, k_ref, v_ref, o_ref, *scratch, ta):
    ng = len(ATTN_DILATIONS)
    kbs = scratch[0:2 * ng:2]
    vbs = scratch[1:2 * ng:2]
    acc, mrun, drun, og, mg, lg = scratch[2 * ng:]
    i = pl.program_id(1)
    g = pl.program_id(2)

    @pl.when(g == 0)
    def _():
        acc[...] = jnp.zeros_like(acc)
        drun[...] = jnp.zeros_like(drun)
        mrun[...] = jnp.full(mrun.shape, NEG_BIG, F32)

    for gi, d in enumerate(ATTN_DILATIONS):
        @pl.when(g == gi)
        def _(gi=gi, d=d):
            _attn_group(q_ref, k_ref, v_ref, kbs[gi], vbs[gi], acc, mrun, drun, og, mg, lg, d=d, ta=ta, i=i)

    @pl.when(g == ng - 1)
    def _():
        for s in range(acc.shape[0]):
            o_ref[0, :, s * LANES:(s + 1) * LANES] = acc[s] / drun[s]


def _attention(q, k, v):
    b, nslab, l, _ = q.shape
    ng = len(ATTN_DILATIONS)
    gs = nslab // ng
    ta = ATTN_BLOCK * max(ATTN_DILATIONS)
    assert l % ta == 0
    blk = pl.BlockSpec((1, gs, ta, LANES), lambda bi, i, g: (bi, g, i, 0))
    scratch = []
    for d in ATTN_DILATIONS:
        scratch += [pltpu.VMEM((gs, ATTN_BLOCK * d + ta, LANES), F32)] * 2
    scratch += [pltpu.VMEM((gs, ta, LANES), F32)] * 6
    return pl.pallas_call(
        functools.partial(_attn_kernel, ta=ta), grid=(b, l // ta, ng),
        in_specs=[blk, blk, blk],
        out_specs=pl.BlockSpec((1, ta, gs * LANES), lambda bi, i, g: (bi, i, 0)),
        out_shape=jax.ShapeDtypeStruct((b, l, gs * LANES), F32),
        scratch_shapes=scratch,
        compiler_params=_cparams(("arbitrary", "arbitrary", "arbitrary")), name="dilated_attention",
    )(q, k, v)


def _merge_kernel(pm_ref, ys_ref, ao_ref, gate_ref, x_ref, wp_ref, wg_ref, ws_ref, wa_ref, wo_ref, g2_ref,
                  xo_ref, h2_ref, *, d_model, ssm_w):
    y_pool = _dot(pm_ref[0], wp_ref[...])
    glu = _dot(ys_ref[...], wg_ref[...])
    sg = (glu[:, :ssm_w] * _sigmoid(glu[:, ssm_w:])).astype(BF16)
    y_ssm = _dot(sg, ws_ref[...])
    y_attn = _dot(ao_ref[0].astype(BF16), wa_ref[...])
    gate = gate_ref[0].astype(F32)
    merged = (gate[:, :d_model] * y_pool + gate[:, d_model:2 * d_model] * y_ssm
              + gate[:, 2 * d_model:] * y_attn)
    xn = x_ref[0] + _dot(merged.astype(BF16), wo_ref[...])
    xo_ref[0] = xn
    h2_ref[0] = _rmsnorm(xn, g2_ref[...]).astype(BF16)


def _merge(pm, ys, ao, gates, x, w_pool, w_glu, w_ssm, w_attn, w_out, g2):
    b, l, d = x.shape
    tm = 256
    ssm_w = w_glu.shape[0]
    row = lambda bi, i: (bi, i, 0)
    const = lambda bi, i: (0, 0)
    full = lambda a: pl.BlockSpec(a.shape, const)
    return pl.pallas_call(
        functools.partial(_merge_kernel, d_model=d, ssm_w=ssm_w), grid=(b, l // tm),
        in_specs=[pl.BlockSpec((1, tm, pm.shape[-1]), row),
                  pl.BlockSpec((tm, ssm_w), lambda bi, i: (i, bi)),
                  pl.BlockSpec((1, tm, ao.shape[-1]), row),
                  pl.BlockSpec((1, tm, 3 * d), row),
                  pl.BlockSpec((1, tm, d), row),
                  full(w_pool), full(w_glu), full(w_ssm), full(w_attn), full(w_out),
                  pl.BlockSpec((1, d), const)],
        out_specs=[pl.BlockSpec((1, tm, d), row), pl.BlockSpec((1, tm, d), row)],
        out_shape=[jax.ShapeDtypeStruct((b, l, d), F32), jax.ShapeDtypeStruct((b, l, d), BF16)],
        compiler_params=_cparams(("parallel", "parallel")), name="branch_merge",
    )(pm, ys, ao, gates, x, w_pool, w_glu, w_ssm, w_attn, w_out, g2.reshape(1, d))


def _candidate_pairs(k):
    return [(a, b) for a in range(k) for b in range(k) if (a + 1) * (b + 1) <= k]


REMOVED = -(2 ** 31)


def _sort_key(x):
    b = pltpu.bitcast(x, jnp.int32)
    return b ^ ((b >> 31) & 0x7FFFFFFF)


def _from_sort_key(k):
    return pltpu.bitcast(k ^ ((k >> 31) & 0x7FFFFFFF), F32)


def _topk_round(work, rows):
    m = jnp.max(work, axis=0, keepdims=True)
    idx = jnp.min(jnp.where(work == m, rows, float(work.shape[0])), axis=0, keepdims=True)
    hit = rows == idx
    return m, hit, jnp.where(hit, -jnp.inf, work)


def _route_kernel(h_ref, wq_ref, k1_ref, k2_ref, r2_ref, e2_ref, n1_ref, e1_ref,
                  qt_ref, s1_ref, s2_ref, v1_ref, v2_ref, cand_ref, sel_ref, *, nheads, nkeys, topk, tt):
    qt_ref[...] = _dot_nt(wq_ref[...], h_ref[...])
    hc = k1_ref.shape[1]
    pairs = _candidate_pairs(topk)
    ncand = cand_ref.shape[0]

    def head(hd, carry):
        base = pl.multiple_of(hd * 2 * hc, 2 * hc)
        q1 = qt_ref[pl.ds(base, hc), :].astype(BF16)
        q2 = qt_ref[pl.ds(base + hc, hc), :].astype(BF16)
        s1_ref[...] = _dot(k1_ref[...], q1)
        s2_ref[...] = _dot(k2_ref[...], q2)

        def col(j, carry2):
            c0 = pl.multiple_of(j * LANES, LANES)
            cols = pl.ds(c0, LANES)
            s1 = s1_ref[:, cols]
            s2 = s2_ref[:, cols]

            def build_candidates():
                for ci, (a, bq) in enumerate(pairs):
                    cand_ref[ci:ci + 1, :] = v1_ref[a:a + 1, :] + v2_ref[bq:bq + 1, :]
                if ncand > len(pairs):
                    cand_ref[len(pairs):, :] = jnp.full((ncand - len(pairs), LANES), -jnp.inf, F32)
                return cand_ref[...]

            def finish(rank1, rank2, sel, cv):
                sel_ref[...] = sel
                top = v1_ref[0:1, :] + v2_ref[0:1, :]
                z = jnp.sum(sel * jnp.exp(jnp.where(sel > 0.0, cv, top) - top), axis=0, keepdims=True)
                n1 = jnp.zeros(s1.shape, F32)
                off = 0
                for a in range(topk):
                    cnt = topk // (a + 1)
                    n_a = jnp.sum(sel_ref[off:off + cnt, :], axis=0, keepdims=True)
                    n1 = jnp.where(rank1 == float(a), n_a, n1)
                    off += cnt
                n1_ref[hd, :, cols] = n1
                e1_ref[hd, :, cols] = jnp.exp(s1 - v1_ref[0:1, :]) / z
                r2_ref[hd, j] = pltpu.bitcast(rank2.astype(BF16), jnp.uint32)
                e2_ref[hd, j] = pltpu.bitcast(jnp.exp(s2 - v2_ref[0:1, :]).astype(BF16), jnp.uint32)

            w1, w2 = _sort_key(s1), _sort_key(s2)
            for r in range(topk):
                m1 = jnp.max(w1, axis=0, keepdims=True)
                m2 = jnp.max(w2, axis=0, keepdims=True)
                w1 = jnp.where(w1 == m1, REMOVED + r, w1)
                w2 = jnp.where(w2 == m2, REMOVED + r, w2)
                v1_ref[r:r + 1, :] = _from_sort_key(m1)
                v2_ref[r:r + 1, :] = _from_sort_key(m2)
            gone1 = w1 < REMOVED + topk
            gone2 = w2 < REMOVED + topk
            rank1 = jnp.where(gone1, (w1 - REMOVED).astype(F32), float(nkeys))
            rank2 = jnp.where(gone2, (w2 - REMOVED).astype(F32), float(nkeys))
            cv = build_candidates()
            wc = _sort_key(cv)
            for r in range(topk):
                wc = jnp.where(wc == jnp.max(wc, axis=0, keepdims=True), REMOVED + r, wc)
            sel = jnp.where(wc < REMOVED + topk, 1.0, 0.0)
            removed = (jnp.sum(jnp.where(gone1, 1.0, 0.0), axis=0, keepdims=True)
                       + jnp.sum(jnp.where(gone2, 1.0, 0.0), axis=0, keepdims=True)
                       + jnp.sum(sel, axis=0, keepdims=True))
            tied = jnp.max(jnp.where(removed != 3.0 * topk, 1.0, 0.0))
            finish(rank1, rank2, sel, cv)

            @pl.when(tied > 0.0)
            def _():
                key_rows = lax.broadcasted_iota(jnp.int32, s1.shape, 0).astype(F32)
                x1, x2 = s1, s2
                rk1 = jnp.full(s1.shape, float(nkeys), F32)
                rk2 = rk1
                for r in range(topk):
                    m1, hit1, x1 = _topk_round(x1, key_rows)
                    m2, hit2, x2 = _topk_round(x2, key_rows)
                    rk1 = jnp.where(hit1, float(r), rk1)
                    rk2 = jnp.where(hit2, float(r), rk2)
                    v1_ref[r:r + 1, :] = m1
                    v2_ref[r:r + 1, :] = m2
                cvs = build_candidates()
                cand_rows = lax.broadcasted_iota(jnp.int32, cvs.shape, 0).astype(F32)
                xc = cvs
                sl = jnp.zeros(cvs.shape, F32)
                for r in range(topk):
                    _, hit, xc = _topk_round(xc, cand_rows)
                    sl = jnp.where(hit, 1.0, sl)
                finish(rk1, rk2, sl, cvs)

            return carry2

        lax.fori_loop(0, tt // LANES, col, 0)
        return carry

    lax.fori_loop(0, nheads, head, 0)


def _peer_route(h2, wq_t, k1, k2):
    t, d = h2.shape
    nkeys, hc = k1.shape
    nheads = wq_t.shape[0] // (2 * hc)
    tt = 512
    npairs = len(_candidate_pairs(PEER_TOPK))
    ncand = -(-npairs // SUBLANES) * SUBLANES
    kern = functools.partial(_route_kernel, nheads=nheads, nkeys=nkeys, topk=PEER_TOPK, tt=tt)
    kblk = pl.BlockSpec((nheads, tt // LANES, nkeys // 2, LANES), lambda i: (0, i, 0, 0))
    ksh = jax.ShapeDtypeStruct((nheads, t // LANES, nkeys // 2, LANES), jnp.uint32)
    oblk = pl.BlockSpec((nheads, nkeys, tt), lambda i: (0, 0, i))
    osh = jax.ShapeDtypeStruct((nheads, nkeys, t), F32)
    return pl.pallas_call(
        kern, grid=(t // tt,),
        in_specs=[pl.BlockSpec((tt, d), lambda i: (i, 0)),
                  pl.BlockSpec(wq_t.shape, lambda i: (0, 0)),
                  pl.BlockSpec(k1.shape, lambda i: (0, 0)),
                  pl.BlockSpec(k2.shape, lambda i: (0, 0))],
        out_specs=[kblk, kblk, oblk, oblk], out_shape=[ksh, ksh, osh, osh],
        scratch_shapes=[pltpu.VMEM((wq_t.shape[0], tt), F32),
                        pltpu.VMEM((nkeys, tt), F32), pltpu.VMEM((nkeys, tt), F32),
                        pltpu.VMEM((PEER_TOPK, LANES), F32), pltpu.VMEM((PEER_TOPK, LANES), F32),
                        pltpu.VMEM((ncand, LANES), F32), pltpu.VMEM((ncand, LANES), F32)],
        compiler_params=_cparams(("parallel",)), name="peer_route",
    )(h2, wq_t, k1, k2)


def _ffn_kernel(h_ref, u_ref, vt_ref, r2_ref, e2_ref, n1_ref, e1_ref, o_ref, acc_ref, at_ref, pt_ref,
                *, nheads, nkeys, cw):
    j = pl.program_id(1)
    tt = h_ref.shape[0]
    et = u_ref.shape[0]
    nhalf = cw // LANES

    @pl.when(j == 0)
    def _():
        acc_ref[...] = jnp.zeros_like(acc_ref)

    ktiles = nkeys // BF16_ROWS
    es = at_ref.shape[1]
    nsub = es // nkeys
    units = [(eh, c) for eh in range(et // es) for c in range(tt // cw)]
    n_at, n_pt = at_ref.shape[0], pt_ref.shape[0]

    def pre_activations(k):
        eh, c = units[k]
        at_ref[k % n_at] = _dot_nt(u_ref[eh * es:(eh + 1) * es, :], h_ref[c * cw:(c + 1) * cw, :])

    pre_activations(0)
    for k, (eh, c) in enumerate(units):
        if k + 1 < len(units):
            pre_activations(k + 1)
        for jj in range(nsub):
            tiles = slice(jj * ktiles, (jj + 1) * ktiles)
            row = eh * nsub + jj
            for hf in range(nhalf):
                ck = c * nhalf + hf
                cols = slice(ck * LANES, (ck + 1) * LANES)
                lanes = slice(hf * LANES, (hf + 1) * LANES)
                gate = jnp.zeros((ktiles, BF16_ROWS, LANES), BF16)
                for hd in range(nheads):
                    n_row = jnp.broadcast_to(n1_ref[hd, row:row + 1, cols], (BF16_ROWS, LANES)).astype(BF16)
                    e_row = jnp.broadcast_to(e1_ref[hd, row:row + 1, cols], (BF16_ROWS, LANES)).astype(BF16)
                    r2 = pltpu.bitcast(r2_ref[hd, ck], BF16).reshape(ktiles, BF16_ROWS, LANES)
                    e2 = pltpu.bitcast(e2_ref[hd, ck], BF16).reshape(ktiles, BF16_ROWS, LANES)
                    gate = gate + jnp.where(r2 < n_row[None], e2, 0) * e_row[None]
                a = at_ref[k % n_at, jj * nkeys:(jj + 1) * nkeys, lanes].astype(BF16)
                pt_ref[k % n_pt, tiles, :, lanes] = gate * _gelu(a.reshape(ktiles, BF16_ROWS, LANES))
        p = pt_ref[k % n_pt].reshape(es, cw)
        acc_ref[c] += _dot(vt_ref[:, eh * es:(eh + 1) * es], p)

    @pl.when(j == pl.num_programs(1) - 1)
    def _():
        for c in range(tt // cw):
            o_ref[c * cw:(c + 1) * cw, :] = acc_ref[c].T


def _peer_experts(h2, u_tab, v_tab_t, r2, e2, n1, e1):
    t, d = h2.shape
    ne = u_tab.shape[0]
    nheads, nkeys, _ = n1.shape
    tt, et, cw = 1024, 1024, 256
    es = nkeys * SUBLANES
    at_ring, pt_ring = 4, 4
    kern = functools.partial(_ffn_kernel, nheads=nheads, nkeys=nkeys, cw=cw)
    full_keys = pl.BlockSpec((nheads, tt // LANES, nkeys // 2, LANES), lambda i, j: (0, i, 0, 0))
    sub_keys = pl.BlockSpec((nheads, et // nkeys, tt), lambda i, j: (0, j, i))
    return pl.pallas_call(
        kern, grid=(t // tt, ne // et),
        in_specs=[pl.BlockSpec((tt, d), lambda i, j: (i, 0)),
                  pl.BlockSpec((et, d), lambda i, j: (j, 0)),
                  pl.BlockSpec((d, et), lambda i, j: (0, j)),
                  full_keys, full_keys, sub_keys, sub_keys],
        out_specs=pl.BlockSpec((tt, d), lambda i, j: (i, 0)),
        out_shape=jax.ShapeDtypeStruct((t, d), F32),
        scratch_shapes=[pltpu.VMEM((tt // cw, d, cw), F32),
                        pltpu.VMEM((at_ring, es, cw), F32),
                        pltpu.VMEM((pt_ring, es // BF16_ROWS, BF16_ROWS, cw), BF16)],
        compiler_params=_cparams(("parallel", "arbitrary")), name="peer_experts",
    )(h2, u_tab, v_tab_t, r2, e2, n1, e1)


def _final_kernel(x_ref, dl_ref, g_ref, o_ref):
    o_ref[...] = _rmsnorm(x_ref[...] + dl_ref[...], g_ref[...])


def _final_norm(x, delta, g):
    t, d = x.shape
    tm = 1024
    blk = pl.BlockSpec((tm, d), lambda i: (i, 0))
    return pl.pallas_call(
        _final_kernel, grid=(t // tm,),
        in_specs=[blk, blk, pl.BlockSpec((1, d), lambda i: (0, 0))],
        out_specs=blk, out_shape=jax.ShapeDtypeStruct((t, d), F32),
        compiler_params=_cparams(("parallel",)), name="final_norm",
    )(x, delta, g.reshape(1, d))


def _swap_halves(w):
    d, c = w.shape
    half = HEAD_DIM // 2
    return w.reshape(d, c // HEAD_DIM, 2, half)[:, :, ::-1, :].reshape(d, c)


def kernel(x, positions, norm1_g, w_in, pool_w, pool_scale, pool_proj, ssm_a_re, ssm_a_im, ssm_log_dt,
           ssm_b_re, ssm_b_im, ssm_c_re, ssm_c_im, ssm_d, ssm_glu, ssm_proj, attn_proj, w_out, norm2_g,
           peer_wq, peer_k1, peer_k2, peer_u, peer_v, final_g):
    b, l, d = x.shape
    depth = w_in.shape[0]
    pool_w_cols = pool_proj.shape[1]
    ssm_w_cols = ssm_proj.shape[1]
    attn_w = len(ATTN_DILATIONS) * HEADS_PER_GROUP * HEAD_DIM
    cos, sin = _rope_tables(positions)
    delta = None
    for layer in range(depth):
        w = w_in[layer]
        o0 = pool_w_cols + ssm_w_cols
        wq, wk, wv, wg = (w[:, o0:o0 + attn_w], w[:, o0 + attn_w:o0 + 2 * attn_w],
                          w[:, o0 + 2 * attn_w:o0 + 3 * attn_w], w[:, o0 + 3 * attn_w:])
        w_ext = jnp.concatenate([w[:, :o0], wq, _swap_halves(wq), wk, _swap_halves(wk), wv, wg],
                                axis=1).astype(BF16)
        outs = _in_projection(x, delta, norm1_g[layer], w_ext, cos, sin,
                              pool_w=pool_w_cols, ssm_w=ssm_w_cols, attn_w=attn_w)
        u_pool, u_ssm, q, k, v, gates = outs[:6]
        if delta is not None:
            x = outs[6]
        pm = _pool_mixer(u_pool, pool_w[layer].astype(BF16), pool_scale[layer])
        abar_re, abar_im, bbar_re_t, bbar_im_t = _ssm_params(
            ssm_a_re[layer], ssm_a_im[layer], ssm_log_dt[layer], ssm_b_re[layer], ssm_b_im[layer])
        ys = _ssm_mixer(u_ssm, b, abar_re, abar_im, bbar_re_t, bbar_im_t,
                        ssm_c_re[layer], ssm_c_im[layer], ssm_d[layer])
        ao = _attention(q, k, v)
        x, h2 = _merge(pm, ys, ao, gates, x, pool_proj[layer].astype(BF16), ssm_glu[layer].astype(BF16),
                       ssm_proj[layer].astype(BF16), attn_proj[layer].astype(BF16),
                       w_out[layer].astype(BF16), norm2_g[layer])
        h2f = h2.reshape(b * l, d)
        r2, e2, n1, e1 = _peer_route(h2f, peer_wq[layer].T.astype(BF16), peer_k1[layer].astype(BF16),
                                     peer_k2[layer].astype(BF16))
        delta = _peer_experts(h2f, peer_u[layer].astype(BF16), peer_v[layer].T.astype(BF16),
                              r2, e2, n1, e1).reshape(b, l, d)
    return _final_norm(x.reshape(b * l, d), delta.reshape(b * l, d), final_g).reshape(b, l, d)
```

```python
import functools
import math

import jax
import jax.numpy as jnp
from jax import lax
from jax.experimental import pallas as pl
from jax.experimental.pallas import tpu as pltpu

F32 = jnp.float32
BF16 = jnp.bfloat16

EPS = 1e-6
POOL_WINDOWS = (2, 4, 8, 16)
POOL_GROUP = 128
SSM_GROUP = 16
SSM_GROUPS = 32
SSM_STATE = 64
HEAD_DIM = 64
HEADS_PER_GROUP = 4
ATTN_DILATIONS = (1, 4, 16)
ATTN_BLOCK = 128
ROPE_THETA = 10000.0
PEER_HEADS = 8
PEER_TOPK = 16

LANES = 128
SUBLANES = 8
BF16_ROWS = 16
VMEM_LIMIT = 56 * 1024 * 1024
NEG_BIG = -1e30


def _cparams(sem):
    return pltpu.CompilerParams(dimension_semantics=sem, vmem_limit_bytes=VMEM_LIMIT)


def _gelu(x):
    c = math.sqrt(2.0 / math.pi)
    return x * (0.5 * (1.0 + jnp.tanh(c * (x + 0.044715 * (x * x * x)))))


def _sigmoid(x):
    return 1.0 / (1.0 + jnp.exp(-x))


def _rmsnorm(x, g):
    ms = jnp.mean(x * x, axis=-1, keepdims=True)
    return x * lax.rsqrt(ms + EPS) * g


def _dot(a, b):
    return jnp.dot(a, b, preferred_element_type=F32)


def _dot_nt(a, b):
    return lax.dot_general(a, b, (((1,), (1,)), ((), ())), preferred_element_type=F32)


def _rope_kernel(pos_ref, invf_ref, sign_ref, cos_ref, sin_ref):
    ang = pos_ref[...].astype(F32) * invf_ref[...]
    cos_ref[...] = jnp.cos(ang)
    sin_ref[...] = jnp.sin(ang) * sign_ref[...]


def _rope_tables(positions):
    b, l = positions.shape
    t = b * l
    half = HEAD_DIM // 2
    inv_freq = ROPE_THETA ** (-jnp.arange(half, dtype=F32) / half)
    invf = jnp.tile(inv_freq, LANES // half).reshape(1, LANES)
    sign = jnp.tile(jnp.concatenate([-jnp.ones((half,), F32), jnp.ones((half,), F32)]),
                    LANES // HEAD_DIM).reshape(1, LANES)
    pos_b = jnp.broadcast_to(positions.reshape(t, 1), (t, LANES))
    tr = 2048
    cos, sin = pl.pallas_call(
        _rope_kernel,
        grid=(t // tr,),
        in_specs=[pl.BlockSpec((tr, LANES), lambda i: (i, 0)),
                  pl.BlockSpec((1, LANES), lambda i: (0, 0)),
                  pl.BlockSpec((1, LANES), lambda i: (0, 0))],
        out_specs=[pl.BlockSpec((tr, LANES), lambda i: (i, 0))] * 2,
        out_shape=[jax.ShapeDtypeStruct((t, LANES), F32)] * 2,
        compiler_params=_cparams(("parallel",)),
        name="rope_tables",
    )(pos_b, invf, sign)
    return cos.reshape(b, l, LANES), sin.reshape(b, l, LANES)


def _inproj_kernel(*refs, has_delta, pool_w, ssm_w, attn_w, d_model):
    if has_delta:
        x_ref, dl_ref, g_ref, w_ref, cos_ref, sin_ref = refs[:6]
        outs = refs[6:]
    else:
        x_ref, g_ref, w_ref, cos_ref, sin_ref = refs[:5]
        outs = refs[5:]
    pool_ref, ssm_ref, q_ref, k_ref, v_ref, gate_ref = outs[:6]
    x = x_ref[0]
    if has_delta:
        x = x + dl_ref[0]
        outs[6][0] = x
    h = _rmsnorm(x, g_ref[...]).astype(BF16)
    c0 = 0
    z = _dot(h, w_ref[:, c0:c0 + pool_w + ssm_w])
    pool_ref[0] = z[:, :pool_w]
    ssm_ref[...] = z[:, pool_w:]
    c0 += pool_w + ssm_w
    cos = cos_ref[0]
    sin = sin_ref[0]
    nslab = attn_w // LANES
    for dst in (q_ref, k_ref):
        z = _dot(h, w_ref[:, c0:c0 + 2 * attn_w])
        for s in range(nslab):
            a = z[:, s * LANES:(s + 1) * LANES]
            a_sw = z[:, attn_w + s * LANES:attn_w + (s + 1) * LANES]
            dst[0, s] = a * cos + a_sw * sin
        c0 += 2 * attn_w
    z = _dot(h, w_ref[:, c0:c0 + attn_w])
    for s in range(nslab):
        v_ref[0, s] = z[:, s * LANES:(s + 1) * LANES]
    c0 += attn_w
    z = _dot(h, w_ref[:, c0:c0 + 3 * d_model])
    gate_ref[0] = _sigmoid(z).astype(BF16)


def _in_projection(x, delta, g, w_ext, cos, sin, *, pool_w, ssm_w, attn_w):
    b, l, d = x.shape
    tm = 256
    nslab = attn_w // LANES
    has_delta = delta is not None
    kern = functools.partial(_inproj_kernel, has_delta=has_delta, pool_w=pool_w, ssm_w=ssm_w,
                             attn_w=attn_w, d_model=d)
    row = lambda bi, i: (bi, i, 0)
    slab = lambda bi, i: (bi, 0, i, 0)
    const2 = lambda bi, i: (0, 0)
    in_specs = [pl.BlockSpec((1, tm, d), row)]
    args = [x]
    if has_delta:
        in_specs.append(pl.BlockSpec((1, tm, d), row))
        args.append(delta)
    in_specs += [pl.BlockSpec((1, d), const2),
                 pl.BlockSpec(w_ext.shape, const2, pipeline_mode=pl.Buffered(1)),
                 pl.BlockSpec((1, tm, LANES), row),
                 pl.BlockSpec((1, tm, LANES), row)]
    args += [g.reshape(1, d), w_ext, cos, sin]
    out_specs = [pl.BlockSpec((1, tm, pool_w), row),
                 pl.BlockSpec((tm, ssm_w), lambda bi, i: (i, bi)),
                 pl.BlockSpec((1, nslab, tm, LANES), slab),
                 pl.BlockSpec((1, nslab, tm, LANES), slab),
                 pl.BlockSpec((1, nslab, tm, LANES), slab),
                 pl.BlockSpec((1, tm, 3 * d), row)]
    out_shape = [jax.ShapeDtypeStruct((b, l, pool_w), F32),
                 jax.ShapeDtypeStruct((l, b * ssm_w), F32),
                 jax.ShapeDtypeStruct((b, nslab, l, LANES), F32),
                 jax.ShapeDtypeStruct((b, nslab, l, LANES), F32),
                 jax.ShapeDtypeStruct((b, nslab, l, LANES), F32),
                 jax.ShapeDtypeStruct((b, l, 3 * d), BF16)]
    if has_delta:
        out_specs.append(pl.BlockSpec((1, tm, d), row))
        out_shape.append(jax.ShapeDtypeStruct((b, l, d), F32))
    return pl.pallas_call(
        kern, grid=(b, l // tm), in_specs=in_specs, out_specs=out_specs, out_shape=out_shape,
        compiler_params=_cparams(("parallel", "parallel")), name="in_projection",
    )(*args)


def _pool_kernel(u_ref, halo_ref, w_ref, scale_ref, o_ref, *, tm, halo):
    i = pl.program_id(1)
    cur = u_ref[0]
    prev = jnp.where(i > 0, halo_ref[0], 0.0)
    ext = jnp.concatenate([prev, cur], axis=0)
    t_idx = i * tm + lax.broadcasted_iota(jnp.int32, (tm, POOL_GROUP), 0)
    for gi, win in enumerate(POOL_WINDOWS):
        cs = slice(gi * POOL_GROUP, (gi + 1) * POOL_GROUP)
        s = ext[:, cs]
        sh = 1
        while sh < win:
            s = s + pltpu.roll(s, sh, axis=0)
            sh *= 2
        count = jnp.minimum(t_idx + 1, win).astype(F32)
        pooled = s[halo:, :] / count - cur[:, cs]
        mixed = _dot(pooled.astype(BF16), w_ref[gi]) * scale_ref[:, cs]
        o_ref[0, :, cs] = mixed.astype(BF16)


def _pool_mixer(u, w, scale):
    b, l, c = u.shape
    tm, halo = 512, 16
    assert halo >= max(POOL_WINDOWS) and halo % SUBLANES == 0
    kern = functools.partial(_pool_kernel, tm=tm, halo=halo)
    return pl.pallas_call(
        kern, grid=(b, l // tm),
        in_specs=[pl.BlockSpec((1, tm, c), lambda bi, i: (bi, i, 0)),
                  pl.BlockSpec((1, halo, c), lambda bi, i: (bi, jnp.maximum(i * (tm // halo) - 1, 0), 0)),
                  pl.BlockSpec(w.shape, lambda bi, i: (0, 0, 0)),
                  pl.BlockSpec((1, c), lambda bi, i: (0, 0))],
        out_specs=pl.BlockSpec((1, tm, c), lambda bi, i: (bi, i, 0)),
        out_shape=jax.ShapeDtypeStruct((b, l, c), BF16),
        compiler_params=_cparams(("parallel", "parallel")), name="pool_mixer",
    )(u, u, w, scale.reshape(1, c))


def _ssm_param_kernel(are_ref, aim_ref, ldt_ref, bre_ref, bim_ref, oar_ref, oai_ref, obr_ref, obi_ref):
    ar = are_ref[...]
    ai = aim_ref[...]
    dt = jnp.exp(ldt_ref[...])
    decay = jnp.exp(ar * dt)
    abar_re = decay * jnp.cos(ai * dt)
    abar_im = decay * jnp.sin(ai * dt)
    den = ar * ar + ai * ai
    nr = abar_re - 1.0
    k_re = (nr * ar + abar_im * ai) / den
    k_im = (abar_im * ar - nr * ai) / den
    oar_ref[...] = abar_re
    oai_ref[...] = abar_im
    for gi in range(are_ref.shape[0]):
        kr = k_re[gi:gi + 1, :]
        ki = k_im[gi:gi + 1, :]
        br = bre_ref[gi]
        bi = bim_ref[gi]
        obr_ref[gi] = kr * br - ki * bi
        obi_ref[gi] = kr * bi + ki * br


def _ssm_params(a_re, a_im, log_dt, b_re, b_im):
    g, n, p = b_re.shape
    brt = jnp.transpose(b_re, (0, 2, 1))
    bit = jnp.transpose(b_im, (0, 2, 1))
    return pl.pallas_call(
        _ssm_param_kernel,
        out_shape=[jax.ShapeDtypeStruct((g, n), F32)] * 2 + [jax.ShapeDtypeStruct((g, p, n), F32)] * 2,
        name="ssm_params",
    )(a_re, a_im, log_dt.reshape(g, 1), brt, bit)


def _block_diag(m):
    k, r, c = m.shape
    eye = jnp.eye(k, dtype=m.dtype)
    return (m[:, :, None, :] * eye[:, None, :, None]).reshape(k * r, k * c)


def _ssm_kernel(u_ref, bre_ref, bim_ref, are_ref, aim_ref, cre_ref, cim_ref, d_ref, y_ref,
                st_ref, xre_ref, xim_ref, sre_ref, sim_ref, *, tc, hw):
    rows = tc * SUBLANES

    @pl.when(pl.program_id(0) == 0)
    def _():
        st_ref[...] = jnp.zeros_like(st_ref)

    u = u_ref[...]
    odd = (lax.broadcasted_iota(jnp.int32, (rows, hw), 0) % 2) == 1
    ub = u.astype(BF16)
    zero = jnp.zeros_like(ub)
    u_exp = jnp.concatenate([jnp.where(odd, zero, ub), jnp.where(odd, ub, zero)], axis=1)
    xre_ref[...] = _dot(u_exp, bre_ref[...])
    xim_ref[...] = _dot(u_exp, bim_ref[...])
    a_re = are_ref[...]
    a_im = aim_ref[...]

    def step(t, carry):
        s_re, s_im = carry
        r0 = pl.multiple_of(t * SUBLANES, SUBLANES)
        n_re = a_re * s_re - a_im * s_im + xre_ref[pl.ds(r0, SUBLANES), :]
        n_im = a_re * s_im + a_im * s_re + xim_ref[pl.ds(r0, SUBLANES), :]
        sre_ref[pl.ds(r0, SUBLANES), :] = n_re
        sim_ref[pl.ds(r0, SUBLANES), :] = n_im
        return n_re, n_im

    s_re, s_im = lax.fori_loop(0, tc, step, (st_ref[0], st_ref[1]), unroll=8)
    st_ref[0] = s_re
    st_ref[1] = s_im
    yf = _dot(sre_ref[...].astype(BF16), cre_ref[...]) - _dot(sim_ref[...].astype(BF16), cim_ref[...])
    y = jnp.where(odd, yf[:, hw:], yf[:, :hw]) + pltpu.repeat(d_ref[...], tc, axis=0) * u
    y_ref[...] = _gelu(y).astype(BF16)


def _ssm_mixer(u_scan, batch, abar_re, abar_im, bbar_re_t, bbar_im_t, c_re, c_im, d_skip):
    l = u_scan.shape[0]
    g, p, n = bbar_re_t.shape
    hg = g // 2
    hw = hg * p
    hs = hg * n
    assert batch * 2 == SUBLANES
    rows_total = l * SUBLANES
    u2 = u_scan.reshape(rows_total, hw)

    def stack_b(bt):
        return jnp.concatenate([_block_diag(bt[:hg]), _block_diag(bt[hg:])], axis=0).astype(BF16)

    def cat_c(c):
        ct = jnp.transpose(c, (0, 2, 1))
        return jnp.concatenate([_block_diag(ct[:hg]), _block_diag(ct[hg:])], axis=1).astype(BF16)

    def tile_a(a):
        return jnp.tile(a.reshape(2, hs), (batch, 1))

    d8 = jnp.tile(d_skip.reshape(2, hw), (batch, 1))
    tc = 64
    rows = tc * SUBLANES
    kern = functools.partial(_ssm_kernel, tc=tc, hw=hw)
    const = lambda i: (0, 0)
    y = pl.pallas_call(
        kern, grid=(l // tc,),
        in_specs=[pl.BlockSpec((rows, hw), lambda i: (i, 0)),
                  pl.BlockSpec((2 * hw, hs), const), pl.BlockSpec((2 * hw, hs), const),
                  pl.BlockSpec((SUBLANES, hs), const), pl.BlockSpec((SUBLANES, hs), const),
                  pl.BlockSpec((hs, 2 * hw), const), pl.BlockSpec((hs, 2 * hw), const),
                  pl.BlockSpec((SUBLANES, hw), const)],
        out_specs=pl.BlockSpec((rows, hw), lambda i: (i, 0)),
        out_shape=jax.ShapeDtypeStruct((rows_total, hw), BF16),
        scratch_shapes=[pltpu.VMEM((2, SUBLANES, hs), F32),
                        pltpu.VMEM((rows, hs), F32), pltpu.VMEM((rows, hs), F32),
                        pltpu.VMEM((rows, hs), F32), pltpu.VMEM((rows, hs), F32)],
        compiler_params=_cparams(("arbitrary",)), name="ssm_mixer",
    )(u2, stack_b(bbar_re_t), stack_b(bbar_im_t), tile_a(abar_re), tile_a(abar_im),
      cat_c(c_re), cat_c(c_im), d8)
    return y.reshape(l, batch * 2 * hw)


def _attn_group(q_ref, k_ref, v_ref, kb, vb, acc, mrun, drun, og, mg, lg, *, d, ta, i):
    p = ATTN_BLOCK * d
    ncombo = ta // ATTN_BLOCK
    nslab = q_ref.shape[1]
    scale = HEAD_DIM ** -0.5
    kb[:, p:p + ta, :] = k_ref[0]
    vb[:, p:p + ta, :] = v_ref[0]

    @pl.when(i == 0)
    def _():
        kb[:, :p, :] = jnp.zeros((nslab, p, LANES), F32)
        vb[:, :p, :] = jnp.zeros((nslab, p, LANES), F32)

    nh = LANES // HEAD_DIM
    qi = lax.broadcasted_iota(jnp.int32, (nh * ATTN_BLOCK, ATTN_BLOCK), 0) % ATTN_BLOCK
    kk = lax.broadcasted_iota(jnp.int32, (nh * ATTN_BLOCK, ATTN_BLOCK), 1)
    mask_cur = kk <= qi
    mask_prev = kk >= qi
    low = lax.broadcasted_iota(jnp.int32, (ATTN_BLOCK, ATTN_BLOCK), 1) < HEAD_DIM

    def rows_at(start):
        return pl.ds(start, ATTN_BLOCK) if d == 1 else pl.ds(start, ATTN_BLOCK, stride=d)

    def combo(c, carry):
        r = c % d
        n = c // d
        qs = n * p + r
        has_prev = jnp.logical_or(i > 0, n > 0)
        mask_p = jnp.logical_and(mask_prev, has_prev)
        for s in range(nslab):
            q = q_ref[0, s, rows_at(qs), :] * scale
            kc = kb[s, rows_at(p + qs), :].astype(BF16)
            kp = kb[s, rows_at(qs), :].astype(BF16)
            vc = vb[s, rows_at(p + qs), :].astype(BF16)
            vp = vb[s, rows_at(qs), :].astype(BF16)
            qh = jnp.concatenate([jnp.where(low, q, 0.0), jnp.where(low, 0.0, q)], axis=0).astype(BF16)
            sc = jnp.where(mask_cur, _dot_nt(qh, kc), -jnp.inf)
            sp = jnp.where(mask_p, _dot_nt(qh, kp), -jnp.inf)
            m = jnp.maximum(jnp.max(sc, axis=1, keepdims=True), jnp.max(sp, axis=1, keepdims=True))
            pc = jnp.exp(sc - m)
            pp = jnp.exp(sp - m)
            den = jnp.sum(pc, axis=1, keepdims=True) + jnp.sum(pp, axis=1, keepdims=True)
            o = _dot(pc.astype(BF16), vc) + _dot(pp.astype(BF16), vp)
            rows = rows_at(qs)
            og[s, rows, :] = jnp.where(low, o[:ATTN_BLOCK], o[ATTN_BLOCK:])
            mg[s, rows, :] = jnp.where(low, m[:ATTN_BLOCK], m[ATTN_BLOCK:])
            lg[s, rows, :] = jnp.where(low, den[:ATTN_BLOCK], den[ATTN_BLOCK:])
        return carry

    lax.fori_loop(0, ncombo, combo, 0)
    kb[:, :p, :] = kb[:, ta:ta + p, :]
    vb[:, :p, :] = vb[:, ta:ta + p, :]

    rc = 256
    for s in range(nslab):
        for r0 in range(0, ta, rc):
            rs = slice(r0, r0 + rc)
            m_old = mrun[s, rs, :]
            m_blk = mg[s, rs, :]
            m_new = jnp.maximum(m_old, m_blk)
            a_old = jnp.exp(m_old - m_new)
            a_blk = jnp.exp(m_blk - m_new)
            acc[s, rs, :] = acc[s, rs, :] * a_old + og[s, rs, :] * a_blk
            drun[s, rs, :] = drun[s, rs, :] * a_old + lg[s, rs, :] * a_blk
            mrun[s, rs, :] = m_new


def _attn_kernel(q_ref, k_ref, v_ref, o_ref, *scratch, ta):
    ng = len(ATTN_DILATIONS)
    kbs = scratch[0:2 * ng:2]
    vbs = scratch[1:2 * ng:2]
    acc, mrun, drun, og, mg, lg = scratch[2 * ng:]
    i = pl.program_id(1)
    g = pl.program_id(2)

    @pl.when(g == 0)
    def _():
        acc[...] = jnp.zeros_like(acc)
        drun[...] = jnp.zeros_like(drun)
        mrun[...] = jnp.full(mrun.shape, NEG_BIG, F32)

    for gi, d in enumerate(ATTN_DILATIONS):
        @pl.when(g == gi)
        def _(gi=gi, d=d):
            _attn_group(q_ref, k_ref, v_ref, kbs[gi], vbs[gi], acc, mrun, drun, og, mg, lg, d=d, ta=ta, i=i)

    @pl.when(g == ng - 1)
    def _():
        for s in range(acc.shape[0]):
            o_ref[0, :, s * LANES:(s + 1) * LANES] = acc[s] / drun[s]


def _attention(q, k, v):
    b, nslab, l, _ = q.shape
    ng = len(ATTN_DILATIONS)
    gs = nslab // ng
    ta = ATTN_BLOCK * max(ATTN_DILATIONS)
    assert l % ta == 0
    blk = pl.BlockSpec((1, gs, ta, LANES), lambda bi, i, g: (bi, g, i, 0))
    scratch = []
    for d in ATTN_DILATIONS:
        scratch += [pltpu.VMEM((gs, ATTN_BLOCK * d + ta, LANES), F32)] * 2
    scratch += [pltpu.VMEM((gs, ta, LANES), F32)] * 6
    return pl.pallas_call(
        functools.partial(_attn_kernel, ta=ta), grid=(b, l // ta, ng),
        in_specs=[blk, blk, blk],
        out_specs=pl.BlockSpec((1, ta, gs * LANES), lambda bi, i, g: (bi, i, 0)),
        out_shape=jax.ShapeDtypeStruct((b, l, gs * LANES), F32),
        scratch_shapes=scratch,
        compiler_params=_cparams(("arbitrary", "arbitrary", "arbitrary")), name="dilated_attention",
    )(q, k, v)


def _merge_kernel(pm_ref, ys_ref, ao_ref, gate_ref, x_ref, wp_ref, wg_ref, ws_ref, wa_ref, wo_ref, g2_ref,
                  xo_ref, h2_ref, *, d_model, ssm_w):
    y_pool = _dot(pm_ref[0], wp_ref[...])
    glu = _dot(ys_ref[...], wg_ref[...])
    sg = (glu[:, :ssm_w] * _sigmoid(glu[:, ssm_w:])).astype(BF16)
    y_ssm = _dot(sg, ws_ref[...])
    y_attn = _dot(ao_ref[0].astype(BF16), wa_ref[...])
    gate = gate_ref[0].astype(F32)
    merged = (gate[:, :d_model] * y_pool + gate[:, d_model:2 * d_model] * y_ssm
              + gate[:, 2 * d_model:] * y_attn)
    xn = x_ref[0] + _dot(merged.astype(BF16), wo_ref[...])
    xo_ref[0] = xn
    h2_ref[...] = _rmsnorm(xn, g2_ref[...]).T.astype(BF16)


def _merge(pm, ys, ao, gates, x, w_pool, w_glu, w_ssm, w_attn, w_out, g2):
    b, l, d = x.shape
    tm = 256
    ssm_w = w_glu.shape[0]
    row = lambda bi, i: (bi, i, 0)
    const = lambda bi, i: (0, 0)
    full = lambda a: pl.BlockSpec(a.shape, const)
    return pl.pallas_call(
        functools.partial(_merge_kernel, d_model=d, ssm_w=ssm_w), grid=(b, l // tm),
        in_specs=[pl.BlockSpec((1, tm, pm.shape[-1]), row),
                  pl.BlockSpec((tm, ssm_w), lambda bi, i: (i, bi)),
                  pl.BlockSpec((1, tm, ao.shape[-1]), row),
                  pl.BlockSpec((1, tm, 3 * d), row),
                  pl.BlockSpec((1, tm, d), row),
                  full(w_pool), full(w_glu), full(w_ssm), full(w_attn), full(w_out),
                  pl.BlockSpec((1, d), const)],
        out_specs=[pl.BlockSpec((1, tm, d), row),
                   pl.BlockSpec((d, tm), lambda bi, i: (0, bi * (l // tm) + i))],
        out_shape=[jax.ShapeDtypeStruct((b, l, d), F32), jax.ShapeDtypeStruct((d, b * l), BF16)],
        compiler_params=_cparams(("parallel", "parallel")), name="branch_merge",
    )(pm, ys, ao, gates, x, w_pool, w_glu, w_ssm, w_attn, w_out, g2.reshape(1, d))


def _candidate_pairs(k):
    return [(a, b) for a in range(k) for b in range(k) if (a + 1) * (b + 1) <= k]


REMOVED = -(2 ** 31)


def _sort_key(x):
    b = pltpu.bitcast(x, jnp.int32)
    return b ^ ((b >> 31) & 0x7FFFFFFF)


def _from_sort_key(k):
    return pltpu.bitcast(k ^ ((k >> 31) & 0x7FFFFFFF), F32)


def _topk_round(work, rows):
    m = jnp.max(work, axis=0, keepdims=True)
    idx = jnp.min(jnp.where(work == m, rows, float(work.shape[0])), axis=0, keepdims=True)
    hit = rows == idx
    return m, hit, jnp.where(hit, -jnp.inf, work)


def _route_kernel(h_ref, wq_ref, k1_ref, k2_ref, r2_ref, e2_ref, n1_ref, e1_ref,
                  qt_ref, s1_ref, s2_ref, v1_ref, v2_ref, cand_ref, sel_ref, *, nheads, nkeys, topk, tt):
    qt_ref[...] = _dot(wq_ref[...], h_ref[...])
    hc = k1_ref.shape[1]
    pairs = _candidate_pairs(topk)
    ncand = cand_ref.shape[0]

    def head(hd, carry):
        base = pl.multiple_of(hd * 2 * hc, 2 * hc)
        q1 = qt_ref[pl.ds(base, hc), :].astype(BF16)
        q2 = qt_ref[pl.ds(base + hc, hc), :].astype(BF16)
        s1_ref[...] = _dot(k1_ref[...], q1)
        s2_ref[...] = _dot(k2_ref[...], q2)

        def col(j, carry2):
            c0 = pl.multiple_of(j * LANES, LANES)
            cols = pl.ds(c0, LANES)
            s1 = s1_ref[:, cols]
            s2 = s2_ref[:, cols]

            def build_candidates():
                for ci, (a, bq) in enumerate(pairs):
                    cand_ref[ci:ci + 1, :] = v1_ref[a:a + 1, :] + v2_ref[bq:bq + 1, :]
                if ncand > len(pairs):
                    cand_ref[len(pairs):, :] = jnp.full((ncand - len(pairs), LANES), -jnp.inf, F32)
                return cand_ref[...]

            def finish(rank1, rank2, sel, cv):
                sel_ref[...] = sel
                top = v1_ref[0:1, :] + v2_ref[0:1, :]
                z = jnp.sum(sel * jnp.exp(jnp.where(sel > 0.0, cv, top) - top), axis=0, keepdims=True)
                n1 = jnp.zeros(s1.shape, F32)
                off = 0
                for a in range(topk):
                    cnt = topk // (a + 1)
                    n_a = jnp.sum(sel_ref[off:off + cnt, :], axis=0, keepdims=True)
                    n1 = jnp.where(rank1 == float(a), n_a, n1)
                    off += cnt
                n1_ref[hd, :, cols] = n1
                e1_ref[hd, :, cols] = jnp.exp(s1 - v1_ref[0:1, :]) / z
                r2_ref[hd, j] = pltpu.bitcast(rank2.astype(BF16), jnp.uint32)
                e2_ref[hd, j] = pltpu.bitcast(jnp.exp(s2 - v2_ref[0:1, :]).astype(BF16), jnp.uint32)

            w1, w2 = _sort_key(s1), _sort_key(s2)
            for r in range(topk):
                m1 = jnp.max(w1, axis=0, keepdims=True)
                m2 = jnp.max(w2, axis=0, keepdims=True)
                w1 = jnp.where(w1 == m1, REMOVED + r, w1)
                w2 = jnp.where(w2 == m2, REMOVED + r, w2)
                v1_ref[r:r + 1, :] = _from_sort_key(m1)
                v2_ref[r:r + 1, :] = _from_sort_key(m2)
            gone1 = w1 < REMOVED + topk
            gone2 = w2 < REMOVED + topk
            rank1 = jnp.where(gone1, (w1 - REMOVED).astype(F32), float(nkeys))
            rank2 = jnp.where(gone2, (w2 - REMOVED).astype(F32), float(nkeys))
            cv = build_candidates()
            wc = _sort_key(cv)
            for r in range(topk):
                wc = jnp.where(wc == jnp.max(wc, axis=0, keepdims=True), REMOVED + r, wc)
            sel = jnp.where(wc < REMOVED + topk, 1.0, 0.0)
            removed = (jnp.sum(jnp.where(gone1, 1.0, 0.0), axis=0, keepdims=True)
                       + jnp.sum(jnp.where(gone2, 1.0, 0.0), axis=0, keepdims=True)
                       + jnp.sum(sel, axis=0, keepdims=True))
            tied = jnp.max(jnp.where(removed != 3.0 * topk, 1.0, 0.0))
            finish(rank1, rank2, sel, cv)

            @pl.when(tied > 0.0)
            def _():
                key_rows = lax.broadcasted_iota(jnp.int32, s1.shape, 0).astype(F32)
                x1, x2 = s1, s2
                rk1 = jnp.full(s1.shape, float(nkeys), F32)
                rk2 = rk1
                for r in range(topk):
                    m1, hit1, x1 = _topk_round(x1, key_rows)
                    m2, hit2, x2 = _topk_round(x2, key_rows)
                    rk1 = jnp.where(hit1, float(r), rk1)
                    rk2 = jnp.where(hit2, float(r), rk2)
                    v1_ref[r:r + 1, :] = m1
                    v2_ref[r:r + 1, :] = m2
                cvs = build_candidates()
                cand_rows = lax.broadcasted_iota(jnp.int32, cvs.shape, 0).astype(F32)
                xc = cvs
                sl = jnp.zeros(cvs.shape, F32)
                for r in range(topk):
                    _, hit, xc = _topk_round(xc, cand_rows)
                    sl = jnp.where(hit, 1.0, sl)
                finish(rk1, rk2, sl, cvs)

            return carry2

        lax.fori_loop(0, tt // LANES, col, 0)
        return carry

    lax.fori_loop(0, nheads, head, 0)


def _peer_route(h2t, wq_t, k1, k2):
    d, t = h2t.shape
    nkeys, hc = k1.shape
    nheads = wq_t.shape[0] // (2 * hc)
    tt = 512
    npairs = len(_candidate_pairs(PEER_TOPK))
    ncand = -(-npairs // SUBLANES) * SUBLANES
    kern = functools.partial(_route_kernel, nheads=nheads, nkeys=nkeys, topk=PEER_TOPK, tt=tt)
    kblk = pl.BlockSpec((nheads, tt // LANES, nkeys // 2, LANES), lambda i: (0, i, 0, 0))
    ksh = jax.ShapeDtypeStruct((nheads, t // LANES, nkeys // 2, LANES), jnp.uint32)
    oblk = pl.BlockSpec((nheads, nkeys, tt), lambda i: (0, 0, i))
    osh = jax.ShapeDtypeStruct((nheads, nkeys, t), F32)
    return pl.pallas_call(
        kern, grid=(t // tt,),
        in_specs=[pl.BlockSpec((d, tt), lambda i: (0, i)),
                  pl.BlockSpec(wq_t.shape, lambda i: (0, 0)),
                  pl.BlockSpec(k1.shape, lambda i: (0, 0)),
                  pl.BlockSpec(k2.shape, lambda i: (0, 0))],
        out_specs=[kblk, kblk, oblk, oblk], out_shape=[ksh, ksh, osh, osh],
        scratch_shapes=[pltpu.VMEM((wq_t.shape[0], tt), F32),
                        pltpu.VMEM((nkeys, tt), F32), pltpu.VMEM((nkeys, tt), F32),
                        pltpu.VMEM((PEER_TOPK, LANES), F32), pltpu.VMEM((PEER_TOPK, LANES), F32),
                        pltpu.VMEM((ncand, LANES), F32), pltpu.VMEM((ncand, LANES), F32)],
        compiler_params=_cparams(("parallel",)), name="peer_route",
    )(h2t, wq_t, k1, k2)


def _pack_kernel(x_ref, o_ref, *, transpose):
    x = x_ref[...]
    if transpose:
        x = x.T
    o_ref[...] = pltpu.bitcast(x.astype(BF16), jnp.uint32)


def _pack_table(tab, *, transpose):
    r, c = tab.shape
    tb = 512
    if transpose:
        in_spec = pl.BlockSpec((tb, c), lambda i: (i, 0))
        out_spec = pl.BlockSpec((c // 2, tb), lambda i: (0, i))
        out_shape = jax.ShapeDtypeStruct((c // 2, r), jnp.uint32)
    else:
        in_spec = pl.BlockSpec((tb, c), lambda i: (i, 0))
        out_spec = pl.BlockSpec((tb // 2, c), lambda i: (i, 0))
        out_shape = jax.ShapeDtypeStruct((r // 2, c), jnp.uint32)
    return pl.pallas_call(
        functools.partial(_pack_kernel, transpose=transpose), grid=(r // tb,),
        in_specs=[in_spec], out_specs=out_spec, out_shape=out_shape,
        compiler_params=_cparams(("parallel",)), name="pack_table",
    )(tab)


def _ffn_kernel(h_ref, u_ref, vt_ref, r2_ref, e2_ref, n1_ref, e1_ref, o_ref, acc_ref, at_ref, pt_ref,
                *, nheads, nkeys, cw):
    j = pl.program_id(1)
    tt = h_ref.shape[1]
    et = 2 * u_ref.shape[0]
    nhalf = cw // LANES

    @pl.when(j == 0)
    def _():
        acc_ref[...] = jnp.zeros_like(acc_ref)

    ktiles = nkeys // BF16_ROWS
    es = at_ref.shape[1]
    nsub = es // nkeys
    units = [(eh, c) for eh in range(et // es) for c in range(tt // cw)]
    n_at, n_pt = at_ref.shape[0], pt_ref.shape[0]

    def pre_activations(k):
        eh, c = units[k]
        u = pltpu.bitcast(u_ref[eh * es // 2:(eh + 1) * es // 2, :], BF16)
        at_ref[k % n_at] = _dot(u, h_ref[:, c * cw:(c + 1) * cw])

    pre_activations(0)
    for k, (eh, c) in enumerate(units):
        if k + 1 < len(units):
            pre_activations(k + 1)
        for jj in range(nsub):
            tiles = slice(jj * ktiles, (jj + 1) * ktiles)
            row = eh * nsub + jj
            for hf in range(nhalf):
                ck = c * nhalf + hf
                cols = slice(ck * LANES, (ck + 1) * LANES)
                lanes = slice(hf * LANES, (hf + 1) * LANES)
                gate = jnp.zeros((ktiles, BF16_ROWS, LANES), BF16)
                for hd in range(nheads):
                    n_row = jnp.broadcast_to(n1_ref[hd, row:row + 1, cols], (BF16_ROWS, LANES)).astype(BF16)
                    e_row = jnp.broadcast_to(e1_ref[hd, row:row + 1, cols], (BF16_ROWS, LANES)).astype(BF16)
                    r2 = pltpu.bitcast(r2_ref[hd, ck], BF16).reshape(ktiles, BF16_ROWS, LANES)
                    e2 = pltpu.bitcast(e2_ref[hd, ck], BF16).reshape(ktiles, BF16_ROWS, LANES)
                    gate = gate + jnp.where(r2 < n_row[None], e2, 0) * e_row[None]
                a = at_ref[k % n_at, jj * nkeys:(jj + 1) * nkeys, lanes].astype(BF16)
                p = (gate * _gelu(a.reshape(ktiles, BF16_ROWS, LANES))).reshape(nkeys, LANES)
                pt_ref[k % n_pt, jj * nkeys // 2:(jj + 1) * nkeys // 2, lanes] = pltpu.bitcast(p, jnp.uint32)
        p = pltpu.bitcast(pt_ref[k % n_pt], BF16)
        vt = pltpu.bitcast(vt_ref[:, eh * es:(eh + 1) * es], BF16)
        acc_ref[c] += _dot(vt, p)

    @pl.when(j == pl.num_programs(1) - 1)
    def _():
        for c in range(tt // cw):
            o_ref[c * cw:(c + 1) * cw, :] = acc_ref[c].T


def _peer_experts(h2t, u_tab, v_tab_t, r2, e2, n1, e1):
    d, t = h2t.shape
    ne = 2 * u_tab.shape[0]
    nheads, nkeys, _ = n1.shape
    tt, et, cw = 1024, 1024, 256
    es = nkeys * SUBLANES
    at_ring, pt_ring = 4, 4
    kern = functools.partial(_ffn_kernel, nheads=nheads, nkeys=nkeys, cw=cw)
    full_keys = pl.BlockSpec((nheads, tt // LANES, nkeys // 2, LANES), lambda i, j: (0, i, 0, 0))
    sub_keys = pl.BlockSpec((nheads, et // nkeys, tt), lambda i, j: (0, j, i))
    return pl.pallas_call(
        kern, grid=(t // tt, ne // et),
        in_specs=[pl.BlockSpec((d, tt), lambda i, j: (0, i)),
                  pl.BlockSpec((et // 2, d), lambda i, j: (j, 0)),
                  pl.BlockSpec((d // 2, et), lambda i, j: (0, j)),
                  full_keys, full_keys, sub_keys, sub_keys],
        out_specs=pl.BlockSpec((tt, d), lambda i, j: (i, 0)),
        out_shape=jax.ShapeDtypeStruct((t, d), F32),
        scratch_shapes=[pltpu.VMEM((tt // cw, d, cw), F32),
                        pltpu.VMEM((at_ring, es, cw), F32),
                        pltpu.VMEM((pt_ring, es // 2, cw), jnp.uint32)],
        compiler_params=_cparams(("parallel", "arbitrary")), name="peer_experts",
    )(h2t, u_tab, v_tab_t, r2, e2, n1, e1)


def _final_kernel(x_ref, dl_ref, g_ref, o_ref):
    o_ref[...] = _rmsnorm(x_ref[...] + dl_ref[...], g_ref[...])


def _final_norm(x, delta, g):
    t, d = x.shape
    tm = 1024
    blk = pl.BlockSpec((tm, d), lambda i: (i, 0))
    return pl.pallas_call(
        _final_kernel, grid=(t // tm,),
        in_specs=[blk, blk, pl.BlockSpec((1, d), lambda i: (0, 0))],
        out_specs=blk, out_shape=jax.ShapeDtypeStruct((t, d), F32),
        compiler_params=_cparams(("parallel",)), name="final_norm",
    )(x, delta, g.reshape(1, d))


def _swap_halves(w):
    d, c = w.shape
    half = HEAD_DIM // 2
    return w.reshape(d, c // HEAD_DIM, 2, half)[:, :, ::-1, :].reshape(d, c)


def kernel(x, positions, norm1_g, w_in, pool_w, pool_scale, pool_proj, ssm_a_re, ssm_a_im, ssm_log_dt,
           ssm_b_re, ssm_b_im, ssm_c_re, ssm_c_im, ssm_d, ssm_glu, ssm_proj, attn_proj, w_out, norm2_g,
           peer_wq, peer_k1, peer_k2, peer_u, peer_v, final_g):
    b, l, d = x.shape
    depth = w_in.shape[0]
    pool_w_cols = pool_proj.shape[1]
    ssm_w_cols = ssm_proj.shape[1]
    attn_w = len(ATTN_DILATIONS) * HEADS_PER_GROUP * HEAD_DIM
    cos, sin = _rope_tables(positions)
    delta = None
    for layer in range(depth):
        w = w_in[layer]
        o0 = pool_w_cols + ssm_w_cols
        wq, wk, wv, wg = (w[:, o0:o0 + attn_w], w[:, o0 + attn_w:o0 + 2 * attn_w],
                          w[:, o0 + 2 * attn_w:o0 + 3 * attn_w], w[:, o0 + 3 * attn_w:])
        w_ext = jnp.concatenate([w[:, :o0], wq, _swap_halves(wq), wk, _swap_halves(wk), wv, wg],
                                axis=1).astype(BF16)
        outs = _in_projection(x, delta, norm1_g[layer], w_ext, cos, sin,
                              pool_w=pool_w_cols, ssm_w=ssm_w_cols, attn_w=attn_w)
        u_pool, u_ssm, q, k, v, gates = outs[:6]
        if delta is not None:
            x = outs[6]
        pm = _pool_mixer(u_pool, pool_w[layer].astype(BF16), pool_scale[layer])
        abar_re, abar_im, bbar_re_t, bbar_im_t = _ssm_params(
            ssm_a_re[layer], ssm_a_im[layer], ssm_log_dt[layer], ssm_b_re[layer], ssm_b_im[layer])
        ys = _ssm_mixer(u_ssm, b, abar_re, abar_im, bbar_re_t, bbar_im_t,
                        ssm_c_re[layer], ssm_c_im[layer], ssm_d[layer])
        ao = _attention(q, k, v)
        x, h2t = _merge(pm, ys, ao, gates, x, pool_proj[layer].astype(BF16), ssm_glu[layer].astype(BF16),
                        ssm_proj[layer].astype(BF16), attn_proj[layer].astype(BF16),
                        w_out[layer].astype(BF16), norm2_g[layer])
        r2, e2, n1, e1 = _peer_route(h2t, peer_wq[layer].T.astype(BF16), peer_k1[layer].astype(BF16),
                                     peer_k2[layer].astype(BF16))
        delta = _peer_experts(h2t, _pack_table(peer_u[layer], transpose=False),
                              _pack_table(peer_v[layer], transpose=True), r2, e2, n1, e1).reshape(b, l, d)
    return _final_norm(x.reshape(b * l, d), delta.reshape(b * l, d), final_g).reshape(b, l, d)
```

```python
import functools
import math

import jax
import jax.numpy as jnp
from jax import lax
from jax.experimental import pallas as pl
from jax.experimental.pallas import tpu as pltpu

F32 = jnp.float32
BF16 = jnp.bfloat16

EPS = 1e-6
POOL_WINDOWS = (2, 4, 8, 16)
POOL_GROUP = 128
SSM_GROUP = 16
SSM_GROUPS = 32
SSM_STATE = 64
HEAD_DIM = 64
HEADS_PER_GROUP = 4
ATTN_DILATIONS = (1, 4, 16)
ATTN_BLOCK = 128
ROPE_THETA = 10000.0
PEER_HEADS = 8
PEER_TOPK = 16

LANES = 128
SUBLANES = 8
BF16_ROWS = 16
VMEM_LIMIT = 56 * 1024 * 1024
NEG_BIG = -1e30


def _cparams(sem):
    return pltpu.CompilerParams(dimension_semantics=sem, vmem_limit_bytes=VMEM_LIMIT)


def _gelu(x):
    c = math.sqrt(2.0 / math.pi)
    return x * (0.5 * (1.0 + jnp.tanh(c * (x + 0.044715 * (x * x * x)))))


def _sigmoid(x):
    return 1.0 / (1.0 + jnp.exp(-x))


def _rmsnorm(x, g):
    ms = jnp.mean(x * x, axis=-1, keepdims=True)
    return x * lax.rsqrt(ms + EPS) * g


def _dot(a, b):
    return jnp.dot(a, b, preferred_element_type=F32)


def _dot_nt(a, b):
    return lax.dot_general(a, b, (((1,), (1,)), ((), ())), preferred_element_type=F32)


def _rope_kernel(pos_ref, invf_ref, sign_ref, cos_ref, sin_ref):
    ang = pos_ref[...].astype(F32) * invf_ref[...]
    cos_ref[...] = jnp.cos(ang)
    sin_ref[...] = jnp.sin(ang) * sign_ref[...]


def _rope_tables(positions):
    b, l = positions.shape
    t = b * l
    half = HEAD_DIM // 2
    inv_freq = ROPE_THETA ** (-jnp.arange(half, dtype=F32) / half)
    invf = jnp.tile(inv_freq, LANES // half).reshape(1, LANES)
    sign = jnp.tile(jnp.concatenate([-jnp.ones((half,), F32), jnp.ones((half,), F32)]),
                    LANES // HEAD_DIM).reshape(1, LANES)
    pos_b = jnp.broadcast_to(positions.reshape(t, 1), (t, LANES))
    tr = 2048
    cos, sin = pl.pallas_call(
        _rope_kernel,
        grid=(t // tr,),
        in_specs=[pl.BlockSpec((tr, LANES), lambda i: (i, 0)),
                  pl.BlockSpec((1, LANES), lambda i: (0, 0)),
                  pl.BlockSpec((1, LANES), lambda i: (0, 0))],
        out_specs=[pl.BlockSpec((tr, LANES), lambda i: (i, 0))] * 2,
        out_shape=[jax.ShapeDtypeStruct((t, LANES), F32)] * 2,
        compiler_params=_cparams(("parallel",)),
        name="rope_tables",
    )(pos_b, invf, sign)
    return cos.reshape(b, l, LANES), sin.reshape(b, l, LANES)


def _inproj_kernel(*refs, has_delta, pool_w, ssm_w, attn_w, d_model):
    if has_delta:
        x_ref, dl_ref, g_ref, w_ref, cos_ref, sin_ref = refs[:6]
        outs = refs[6:]
    else:
        x_ref, g_ref, w_ref, cos_ref, sin_ref = refs[:5]
        outs = refs[5:]
    pool_ref, ssm_ref, q_ref, k_ref, v_ref, gate_ref = outs[:6]
    x = x_ref[0]
    if has_delta:
        x = x + dl_ref[0]
        outs[6][0] = x
    h = _rmsnorm(x, g_ref[...]).astype(BF16)
    c0 = 0
    z = _dot(h, w_ref[:, c0:c0 + pool_w + ssm_w])
    pool_ref[0] = z[:, :pool_w]
    ssm_ref[...] = z[:, pool_w:]
    c0 += pool_w + ssm_w
    cos = cos_ref[0]
    sin = sin_ref[0]
    nslab = attn_w // LANES
    for dst in (q_ref, k_ref):
        z = _dot(h, w_ref[:, c0:c0 + 2 * attn_w])
        for s in range(nslab):
            a = z[:, s * LANES:(s + 1) * LANES]
            a_sw = z[:, attn_w + s * LANES:attn_w + (s + 1) * LANES]
            dst[0, s] = a * cos + a_sw * sin
        c0 += 2 * attn_w
    z = _dot(h, w_ref[:, c0:c0 + attn_w])
    for s in range(nslab):
        v_ref[0, s] = z[:, s * LANES:(s + 1) * LANES]
    c0 += attn_w
    z = _dot(h, w_ref[:, c0:c0 + 3 * d_model])
    gate_ref[0] = _sigmoid(z).astype(BF16)


def _in_projection(x, delta, g, w_ext, cos, sin, *, pool_w, ssm_w, attn_w):
    b, l, d = x.shape
    tm = 256
    nslab = attn_w // LANES
    has_delta = delta is not None
    kern = functools.partial(_inproj_kernel, has_delta=has_delta, pool_w=pool_w, ssm_w=ssm_w,
                             attn_w=attn_w, d_model=d)
    row = lambda bi, i: (bi, i, 0)
    slab = lambda bi, i: (bi, 0, i, 0)
    const2 = lambda bi, i: (0, 0)
    in_specs = [pl.BlockSpec((1, tm, d), row)]
    args = [x]
    if has_delta:
        in_specs.append(pl.BlockSpec((1, tm, d), row))
        args.append(delta)
    in_specs += [pl.BlockSpec((1, d), const2),
                 pl.BlockSpec(w_ext.shape, const2, pipeline_mode=pl.Buffered(1)),
                 pl.BlockSpec((1, tm, LANES), row),
                 pl.BlockSpec((1, tm, LANES), row)]
    args += [g.reshape(1, d), w_ext, cos, sin]
    out_specs = [pl.BlockSpec((1, tm, pool_w), row),
                 pl.BlockSpec((tm, ssm_w), lambda bi, i: (i, bi)),
                 pl.BlockSpec((1, nslab, tm, LANES), slab),
                 pl.BlockSpec((1, nslab, tm, LANES), slab),
                 pl.BlockSpec((1, nslab, tm, LANES), slab),
                 pl.BlockSpec((1, tm, 3 * d), row)]
    out_shape = [jax.ShapeDtypeStruct((b, l, pool_w), F32),
                 jax.ShapeDtypeStruct((l, b * ssm_w), F32),
                 jax.ShapeDtypeStruct((b, nslab, l, LANES), F32),
                 jax.ShapeDtypeStruct((b, nslab, l, LANES), F32),
                 jax.ShapeDtypeStruct((b, nslab, l, LANES), F32),
                 jax.ShapeDtypeStruct((b, l, 3 * d), BF16)]
    if has_delta:
        out_specs.append(pl.BlockSpec((1, tm, d), row))
        out_shape.append(jax.ShapeDtypeStruct((b, l, d), F32))
    return pl.pallas_call(
        kern, grid=(b, l // tm), in_specs=in_specs, out_specs=out_specs, out_shape=out_shape,
        compiler_params=_cparams(("parallel", "parallel")), name="in_projection",
    )(*args)


def _pool_kernel(u_ref, halo_ref, w_ref, scale_ref, o_ref, *, tm, halo):
    i = pl.program_id(1)
    cur = u_ref[0]
    prev = jnp.where(i > 0, halo_ref[0], 0.0)
    ext = jnp.concatenate([prev, cur], axis=0)
    t_idx = i * tm + lax.broadcasted_iota(jnp.int32, (tm, POOL_GROUP), 0)
    for gi, win in enumerate(POOL_WINDOWS):
        cs = slice(gi * POOL_GROUP, (gi + 1) * POOL_GROUP)
        s = ext[:, cs]
        sh = 1
        while sh < win:
            s = s + pltpu.roll(s, sh, axis=0)
            sh *= 2
        count = jnp.minimum(t_idx + 1, win).astype(F32)
        pooled = s[halo:, :] / count - cur[:, cs]
        mixed = _dot(pooled.astype(BF16), w_ref[gi]) * scale_ref[:, cs]
        o_ref[0, :, cs] = mixed.astype(BF16)


def _pool_mixer(u, w, scale):
    b, l, c = u.shape
    tm, halo = 512, 16
    assert halo >= max(POOL_WINDOWS) and halo % SUBLANES == 0
    kern = functools.partial(_pool_kernel, tm=tm, halo=halo)
    return pl.pallas_call(
        kern, grid=(b, l // tm),
        in_specs=[pl.BlockSpec((1, tm, c), lambda bi, i: (bi, i, 0)),
                  pl.BlockSpec((1, halo, c), lambda bi, i: (bi, jnp.maximum(i * (tm // halo) - 1, 0), 0)),
                  pl.BlockSpec(w.shape, lambda bi, i: (0, 0, 0)),
                  pl.BlockSpec((1, c), lambda bi, i: (0, 0))],
        out_specs=pl.BlockSpec((1, tm, c), lambda bi, i: (bi, i, 0)),
        out_shape=jax.ShapeDtypeStruct((b, l, c), BF16),
        compiler_params=_cparams(("parallel", "parallel")), name="pool_mixer",
    )(u, u, w, scale.reshape(1, c))


def _ssm_param_kernel(are_ref, aim_ref, ldt_ref, bre_ref, bim_ref, oar_ref, oai_ref, obr_ref, obi_ref):
    ar = are_ref[...]
    ai = aim_ref[...]
    dt = jnp.exp(ldt_ref[...])
    decay = jnp.exp(ar * dt)
    abar_re = decay * jnp.cos(ai * dt)
    abar_im = decay * jnp.sin(ai * dt)
    den = ar * ar + ai * ai
    nr = abar_re - 1.0
    k_re = (nr * ar + abar_im * ai) / den
    k_im = (abar_im * ar - nr * ai) / den
    oar_ref[...] = abar_re
    oai_ref[...] = abar_im
    for gi in range(are_ref.shape[0]):
        kr = k_re[gi:gi + 1, :]
        ki = k_im[gi:gi + 1, :]
        br = bre_ref[gi]
        bi = bim_ref[gi]
        obr_ref[gi] = kr * br - ki * bi
        obi_ref[gi] = kr * bi + ki * br


def _ssm_params(a_re, a_im, log_dt, b_re, b_im):
    g, n, p = b_re.shape
    brt = jnp.transpose(b_re, (0, 2, 1))
    bit = jnp.transpose(b_im, (0, 2, 1))
    return pl.pallas_call(
        _ssm_param_kernel,
        out_shape=[jax.ShapeDtypeStruct((g, n), F32)] * 2 + [jax.ShapeDtypeStruct((g, p, n), F32)] * 2,
        name="ssm_params",
    )(a_re, a_im, log_dt.reshape(g, 1), brt, bit)


def _block_diag(m):
    k, r, c = m.shape
    eye = jnp.eye(k, dtype=m.dtype)
    return (m[:, :, None, :] * eye[:, None, :, None]).reshape(k * r, k * c)


def _ssm_kernel(u_ref, bre_ref, bim_ref, are_ref, aim_ref, cre_ref, cim_ref, d_ref, y_ref,
                st_ref, xre_ref, xim_ref, sre_ref, sim_ref, *, tc, hw):
    rows = tc * SUBLANES

    @pl.when(pl.program_id(0) == 0)
    def _():
        st_ref[...] = jnp.zeros_like(st_ref)

    u = u_ref[...]
    odd = (lax.broadcasted_iota(jnp.int32, (rows, hw), 0) % 2) == 1
    ub = u.astype(BF16)
    zero = jnp.zeros_like(ub)
    u_exp = jnp.concatenate([jnp.where(odd, zero, ub), jnp.where(odd, ub, zero)], axis=1)
    xre_ref[...] = _dot(u_exp, bre_ref[...])
    xim_ref[...] = _dot(u_exp, bim_ref[...])
    a_re = are_ref[...]
    a_im = aim_ref[...]

    def step(t, carry):
        s_re, s_im = carry
        r0 = pl.multiple_of(t * SUBLANES, SUBLANES)
        n_re = a_re * s_re - a_im * s_im + xre_ref[pl.ds(r0, SUBLANES), :]
        n_im = a_re * s_im + a_im * s_re + xim_ref[pl.ds(r0, SUBLANES), :]
        sre_ref[pl.ds(r0, SUBLANES), :] = n_re
        sim_ref[pl.ds(r0, SUBLANES), :] = n_im
        return n_re, n_im

    s_re, s_im = lax.fori_loop(0, tc, step, (st_ref[0], st_ref[1]), unroll=8)
    st_ref[0] = s_re
    st_ref[1] = s_im
    yf = _dot(sre_ref[...].astype(BF16), cre_ref[...]) - _dot(sim_ref[...].astype(BF16), cim_ref[...])
    y = jnp.where(odd, yf[:, hw:], yf[:, :hw]) + pltpu.repeat(d_ref[...], tc, axis=0) * u
    y_ref[...] = _gelu(y).astype(BF16)


def _ssm_mixer(u_scan, batch, abar_re, abar_im, bbar_re_t, bbar_im_t, c_re, c_im, d_skip):
    l = u_scan.shape[0]
    g, p, n = bbar_re_t.shape
    hg = g // 2
    hw = hg * p
    hs = hg * n
    assert batch * 2 == SUBLANES
    rows_total = l * SUBLANES
    u2 = u_scan.reshape(rows_total, hw)

    def stack_b(bt):
        return jnp.concatenate([_block_diag(bt[:hg]), _block_diag(bt[hg:])], axis=0).astype(BF16)

    def cat_c(c):
        ct = jnp.transpose(c, (0, 2, 1))
        return jnp.concatenate([_block_diag(ct[:hg]), _block_diag(ct[hg:])], axis=1).astype(BF16)

    def tile_a(a):
        return jnp.tile(a.reshape(2, hs), (batch, 1))

    d8 = jnp.tile(d_skip.reshape(2, hw), (batch, 1))
    tc = 64
    rows = tc * SUBLANES
    kern = functools.partial(_ssm_kernel, tc=tc, hw=hw)
    const = lambda i: (0, 0)
    y = pl.pallas_call(
        kern, grid=(l // tc,),
        in_specs=[pl.BlockSpec((rows, hw), lambda i: (i, 0)),
                  pl.BlockSpec((2 * hw, hs), const), pl.BlockSpec((2 * hw, hs), const),
                  pl.BlockSpec((SUBLANES, hs), const), pl.BlockSpec((SUBLANES, hs), const),
                  pl.BlockSpec((hs, 2 * hw), const), pl.BlockSpec((hs, 2 * hw), const),
                  pl.BlockSpec((SUBLANES, hw), const)],
        out_specs=pl.BlockSpec((rows, hw), lambda i: (i, 0)),
        out_shape=jax.ShapeDtypeStruct((rows_total, hw), BF16),
        scratch_shapes=[pltpu.VMEM((2, SUBLANES, hs), F32),
                        pltpu.VMEM((rows, hs), F32), pltpu.VMEM((rows, hs), F32),
                        pltpu.VMEM((rows, hs), F32), pltpu.VMEM((rows, hs), F32)],
        compiler_params=_cparams(("arbitrary",)), name="ssm_mixer",
    )(u2, stack_b(bbar_re_t), stack_b(bbar_im_t), tile_a(abar_re), tile_a(abar_im),
      cat_c(c_re), cat_c(c_im), d8)
    return y.reshape(l, batch * 2 * hw)


def _attn_group(q_ref, k_ref, v_ref, kb, vb, acc, mrun, drun, og, mg, lg, *, d, ta, i):
    p = ATTN_BLOCK * d
    ncombo = ta // ATTN_BLOCK
    nslab = q_ref.shape[1]
    scale = HEAD_DIM ** -0.5
    kb[:, p:p + ta, :] = k_ref[0]
    vb[:, p:p + ta, :] = v_ref[0]

    @pl.when(i == 0)
    def _():
        kb[:, :p, :] = jnp.zeros((nslab, p, LANES), F32)
        vb[:, :p, :] = jnp.zeros((nslab, p, LANES), F32)

    nh = LANES // HEAD_DIM
    qi = lax.broadcasted_iota(jnp.int32, (nh * ATTN_BLOCK, ATTN_BLOCK), 0) % ATTN_BLOCK
    kk = lax.broadcasted_iota(jnp.int32, (nh * ATTN_BLOCK, ATTN_BLOCK), 1)
    mask_cur = kk <= qi
    mask_prev = kk >= qi
    low = lax.broadcasted_iota(jnp.int32, (ATTN_BLOCK, ATTN_BLOCK), 1) < HEAD_DIM

    def rows_at(start):
        return pl.ds(start, ATTN_BLOCK) if d == 1 else pl.ds(start, ATTN_BLOCK, stride=d)

    def combo(c, carry):
        r = c % d
        n = c // d
        qs = n * p + r
        has_prev = jnp.logical_or(i > 0, n > 0)
        mask_p = jnp.logical_and(mask_prev, has_prev)
        for s in range(nslab):
            q = q_ref[0, s, rows_at(qs), :] * scale
            kc = kb[s, rows_at(p + qs), :].astype(BF16)
            kp = kb[s, rows_at(qs), :].astype(BF16)
            vc = vb[s, rows_at(p + qs), :].astype(BF16)
            vp = vb[s, rows_at(qs), :].astype(BF16)
            qh = jnp.concatenate([jnp.where(low, q, 0.0), jnp.where(low, 0.0, q)], axis=0).astype(BF16)
            sc = jnp.where(mask_cur, _dot_nt(qh, kc), -jnp.inf)
            sp = jnp.where(mask_p, _dot_nt(qh, kp), -jnp.inf)
            m = jnp.maximum(jnp.max(sc, axis=1, keepdims=True), jnp.max(sp, axis=1, keepdims=True))
            pc = jnp.exp(sc - m)
            pp = jnp.exp(sp - m)
            den = jnp.sum(pc, axis=1, keepdims=True) + jnp.sum(pp, axis=1, keepdims=True)
            o = _dot(pc.astype(BF16), vc) + _dot(pp.astype(BF16), vp)
            rows = rows_at(qs)
            og[s, rows, :] = jnp.where(low, o[:ATTN_BLOCK], o[ATTN_BLOCK:])
            mg[s, rows, :] = jnp.where(low, m[:ATTN_BLOCK], m[ATTN_BLOCK:])
            lg[s, rows, :] = jnp.where(low, den[:ATTN_BLOCK], den[ATTN_BLOCK:])
        return carry

    lax.fori_loop(0, ncombo, combo, 0)
    kb[:, :p, :] = kb[:, ta:ta + p, :]
    vb[:, :p, :] = vb[:, ta:ta + p, :]

    rc = 256
    for s in range(nslab):
        for r0 in range(0, ta, rc):
            rs = slice(r0, r0 + rc)
            m_old = mrun[s, rs, :]
            m_blk = mg[s, rs, :]
            m_new = jnp.maximum(m_old, m_blk)
            a_old = jnp.exp(m_old - m_new)
            a_blk = jnp.exp(m_blk - m_new)
            acc[s, rs, :] = acc[s, rs, :] * a_old + og[s, rs, :] * a_blk
            drun[s, rs, :] = drun[s, rs, :] * a_old + lg[s, rs, :] * a_blk
            mrun[s, rs, :] = m_new


def _attn_kernel(q_ref, k_ref, v_ref, o_ref, *scratch, ta):
    ng = len(ATTN_DILATIONS)
    kbs = scratch[0:2 * ng:2]
    vbs = scratch[1:2 * ng:2]
    acc, mrun, drun, og, mg, lg = scratch[2 * ng:]
    i = pl.program_id(1)
    g = pl.program_id(2)

    @pl.when(g == 0)
    def _():
        acc[...] = jnp.zeros_like(acc)
        drun[...] = jnp.zeros_like(drun)
        mrun[...] = jnp.full(mrun.shape, NEG_BIG, F32)

    for gi, d in enumerate(ATTN_DILATIONS):
        @pl.when(g == gi)
        def _(gi=gi, d=d):
            _attn_group(q_ref, k_ref, v_ref, kbs[gi], vbs[gi], acc, mrun, drun, og, mg, lg, d=d, ta=ta, i=i)

    @pl.when(g == ng - 1)
    def _():
        for s in range(acc.shape[0]):
            o_ref[0, :, s * LANES:(s + 1) * LANES] = acc[s] / drun[s]


def _attention(q, k, v):
    b, nslab, l, _ = q.shape
    ng = len(ATTN_DILATIONS)
    gs = nslab // ng
    ta = ATTN_BLOCK * max(ATTN_DILATIONS)
    assert l % ta == 0
    blk = pl.BlockSpec((1, gs, ta, LANES), lambda bi, i, g: (bi, g, i, 0))
    scratch = []
    for d in ATTN_DILATIONS:
        scratch += [pltpu.VMEM((gs, ATTN_BLOCK * d + ta, LANES), F32)] * 2
    scratch += [pltpu.VMEM((gs, ta, LANES), F32)] * 6
    return pl.pallas_call(
        functools.partial(_attn_kernel, ta=ta), grid=(b, l // ta, ng),
        in_specs=[blk, blk, blk],
        out_specs=pl.BlockSpec((1, ta, gs * LANES), lambda bi, i, g: (bi, i, 0)),
        out_shape=jax.ShapeDtypeStruct((b, l, gs * LANES), F32),
        scratch_shapes=scratch,
        compiler_params=_cparams(("arbitrary", "arbitrary", "arbitrary")), name="dilated_attention",
    )(q, k, v)


def _merge_kernel(pm_ref, ys_ref, ao_ref, gate_ref, x_ref, wp_ref, wg_ref, ws_ref, wa_ref, wo_ref, g2_ref,
                  xo_ref, h2_ref, *, d_model, ssm_w):
    y_pool = _dot(pm_ref[0], wp_ref[...])
    glu = _dot(ys_ref[...], wg_ref[...])
    sg = (glu[:, :ssm_w] * _sigmoid(glu[:, ssm_w:])).astype(BF16)
    y_ssm = _dot(sg, ws_ref[...])
    y_attn = _dot(ao_ref[0].astype(BF16), wa_ref[...])
    gate = gate_ref[0].astype(F32)
    merged = (gate[:, :d_model] * y_pool + gate[:, d_model:2 * d_model] * y_ssm
              + gate[:, 2 * d_model:] * y_attn)
    xn = x_ref[0] + _dot(merged.astype(BF16), wo_ref[...])
    xo_ref[0] = xn
    h2_ref[...] = _rmsnorm(xn, g2_ref[...]).T.astype(BF16)


def _merge(pm, ys, ao, gates, x, w_pool, w_glu, w_ssm, w_attn, w_out, g2):
    b, l, d = x.shape
    tm = 256
    ssm_w = w_glu.shape[0]
    row = lambda bi, i: (bi, i, 0)
    const = lambda bi, i: (0, 0)
    full = lambda a: pl.BlockSpec(a.shape, const)
    return pl.pallas_call(
        functools.partial(_merge_kernel, d_model=d, ssm_w=ssm_w), grid=(b, l // tm),
        in_specs=[pl.BlockSpec((1, tm, pm.shape[-1]), row),
                  pl.BlockSpec((tm, ssm_w), lambda bi, i: (i, bi)),
                  pl.BlockSpec((1, tm, ao.shape[-1]), row),
                  pl.BlockSpec((1, tm, 3 * d), row),
                  pl.BlockSpec((1, tm, d), row),
                  full(w_pool), full(w_glu), full(w_ssm), full(w_attn), full(w_out),
                  pl.BlockSpec((1, d), const)],
        out_specs=[pl.BlockSpec((1, tm, d), row),
                   pl.BlockSpec((d, tm), lambda bi, i: (0, bi * (l // tm) + i))],
        out_shape=[jax.ShapeDtypeStruct((b, l, d), F32), jax.ShapeDtypeStruct((d, b * l), BF16)],
        compiler_params=_cparams(("parallel", "parallel")), name="branch_merge",
    )(pm, ys, ao, gates, x, w_pool, w_glu, w_ssm, w_attn, w_out, g2.reshape(1, d))


def _candidate_pairs(k):
    return [(a, b) for a in range(k) for b in range(k) if (a + 1) * (b + 1) <= k]


_CODE_UNIT = 2.0 ** 121
_CODE_BASE = 80


def _removed_code(r):
    return -float(_CODE_BASE + r) * _CODE_UNIT


def _removed_round(w):
    return w * (-1.0 / _CODE_UNIT) - float(_CODE_BASE)


def _vreg_rows(x):
    return x.reshape(x.shape[0] // SUBLANES, SUBLANES, x.shape[1])


def _max_all_rows(w):
    p = jnp.max(w, axis=0)
    for sh in (4, 2, 1):
        p = jnp.maximum(p, pltpu.roll(p, sh, axis=0))
    return p


def _topk_round(work, rows):
    m = jnp.max(work, axis=0, keepdims=True)
    idx = jnp.min(jnp.where(work == m, rows, float(work.shape[0])), axis=0, keepdims=True)
    hit = rows == idx
    return m, hit, jnp.where(hit, -jnp.inf, work)


def _route_kernel(h_ref, wq_ref, k1_ref, k2_ref, r2_ref, e2_ref, n1_ref, e1_ref,
                  qt_ref, s1_ref, s2_ref, v1_ref, v2_ref, cand_ref, sel_ref, tied_ref,
                  *, nheads, nkeys, topk, tt):
    qt_ref[...] = _dot(wq_ref[...], h_ref[...])
    hc = k1_ref.shape[1]
    pairs = _candidate_pairs(topk)
    ncand = cand_ref.shape[1]

    def head(hd, carry):
        base = pl.multiple_of(hd * 2 * hc, 2 * hc)
        q1 = qt_ref[pl.ds(base, hc), :].astype(BF16)
        q2 = qt_ref[pl.ds(base + hc, hc), :].astype(BF16)
        s1_ref[...] = _dot(k1_ref[...], q1)
        s2_ref[...] = _dot(k2_ref[...], q2)

        def column(j, exact):
            if isinstance(j, int):
                cols = slice(j * LANES, (j + 1) * LANES)
            else:
                cols = pl.ds(pl.multiple_of(j * LANES, LANES), LANES)
            s1 = s1_ref[:, cols]
            s2 = s2_ref[:, cols]
            v1, v2, cand, selr = v1_ref.at[j], v2_ref.at[j], cand_ref.at[j], sel_ref.at[j]

            def build_candidates():
                for ci, (a, bq) in enumerate(pairs):
                    cand[ci:ci + 1, :] = v1[a:a + 1, :] + v2[bq:bq + 1, :]
                if ncand > len(pairs):
                    cand[len(pairs):, :] = jnp.full((ncand - len(pairs), LANES), -jnp.inf, F32)
                return cand[...]

            def finish(rank1, rank2, sel, cv):
                selr[...] = sel
                top = v1[0:1, :] + v2[0:1, :]
                z = jnp.sum(sel * jnp.exp(jnp.where(sel > 0.0, cv, top) - top), axis=0, keepdims=True)
                n1 = jnp.zeros(s1.shape, F32)
                off = 0
                for a in range(topk):
                    cnt = topk // (a + 1)
                    n_a = jnp.sum(selr[off:off + cnt, :], axis=0, keepdims=True)
                    n1 = jnp.where(rank1 == float(a), n_a, n1)
                    off += cnt
                n1_ref[hd, :, cols] = n1
                e1_ref[hd, :, cols] = jnp.exp(s1 - v1[0:1, :]) / z
                r2_ref[hd, j] = pltpu.bitcast(rank2.astype(BF16), jnp.uint32)
                e2_ref[hd, j] = pltpu.bitcast(jnp.exp(s2 - v2[0:1, :]).astype(BF16), jnp.uint32)

            if exact:
                key_rows = lax.broadcasted_iota(jnp.int32, s1.shape, 0).astype(F32)
                x1, x2 = s1, s2
                rk1 = jnp.full(s1.shape, float(nkeys), F32)
                rk2 = rk1
                for r in range(topk):
                    m1, hit1, x1 = _topk_round(x1, key_rows)
                    m2, hit2, x2 = _topk_round(x2, key_rows)
                    rk1 = jnp.where(hit1, float(r), rk1)
                    rk2 = jnp.where(hit2, float(r), rk2)
                    v1[r:r + 1, :] = m1
                    v2[r:r + 1, :] = m2
                cvs = build_candidates()
                cand_rows = lax.broadcasted_iota(jnp.int32, cvs.shape, 0).astype(F32)
                xc = cvs
                sl = jnp.zeros(cvs.shape, F32)
                for r in range(topk):
                    _, hit, xc = _topk_round(xc, cand_rows)
                    sl = jnp.where(hit, 1.0, sl)
                finish(rk1, rk2, sl, cvs)
                return None

            w1, w2 = _vreg_rows(s1), _vreg_rows(s2)
            for r in range(topk):
                m1 = _max_all_rows(w1)
                m2 = _max_all_rows(w2)
                w1 = jnp.where(w1 == m1[None], _removed_code(r), w1)
                w2 = jnp.where(w2 == m2[None], _removed_code(r), w2)
                v1[r:r + 1, :] = m1[0:1, :]
                v2[r:r + 1, :] = m2[0:1, :]
            w1, w2 = w1.reshape(s1.shape), w2.reshape(s2.shape)
            gone1 = w1 <= _removed_code(0)
            gone2 = w2 <= _removed_code(0)
            rank1 = jnp.where(gone1, _removed_round(w1), float(nkeys))
            rank2 = jnp.where(gone2, _removed_round(w2), float(nkeys))
            cv = build_candidates()
            wc = _vreg_rows(cv)
            for r in range(topk):
                wc = jnp.where(wc == _max_all_rows(wc)[None], _removed_code(r), wc)
            sel = jnp.where(wc.reshape(cv.shape) <= _removed_code(0), 1.0, 0.0)
            removed = (jnp.sum(jnp.where(gone1, 1.0, 0.0), axis=0, keepdims=True)
                       + jnp.sum(jnp.where(gone2, 1.0, 0.0), axis=0, keepdims=True)
                       + jnp.sum(sel, axis=0, keepdims=True))
            finish(rank1, rank2, sel, cv)
            expected = float(3 * topk + ncand - len(pairs))
            return jnp.max(jnp.where(removed != expected, 1.0, 0.0))

        for j in range(tt // LANES):
            tied_ref[j] = column(j, exact=False)

        def redo(j, carry2):
            @pl.when(tied_ref[j] > 0.0)
            def _():
                column(j, exact=True)
            return carry2

        lax.fori_loop(0, tt // LANES, redo, 0)
        return carry

    lax.fori_loop(0, nheads, head, 0)


def _peer_route(h2t, wq_t, k1, k2):
    d, t = h2t.shape
    nkeys, hc = k1.shape
    nheads = wq_t.shape[0] // (2 * hc)
    tt = 512
    npairs = len(_candidate_pairs(PEER_TOPK))
    ncand = -(-npairs // SUBLANES) * SUBLANES
    ncol = tt // LANES
    kern = functools.partial(_route_kernel, nheads=nheads, nkeys=nkeys, topk=PEER_TOPK, tt=tt)
    kblk = pl.BlockSpec((nheads, tt // LANES, nkeys // 2, LANES), lambda i: (0, i, 0, 0))
    ksh = jax.ShapeDtypeStruct((nheads, t // LANES, nkeys // 2, LANES), jnp.uint32)
    oblk = pl.BlockSpec((nheads, nkeys, tt), lambda i: (0, 0, i))
    osh = jax.ShapeDtypeStruct((nheads, nkeys, t), F32)
    return pl.pallas_call(
        kern, grid=(t // tt,),
        in_specs=[pl.BlockSpec((d, tt), lambda i: (0, i)),
                  pl.BlockSpec(wq_t.shape, lambda i: (0, 0)),
                  pl.BlockSpec(k1.shape, lambda i: (0, 0)),
                  pl.BlockSpec(k2.shape, lambda i: (0, 0))],
        out_specs=[kblk, kblk, oblk, oblk], out_shape=[ksh, ksh, osh, osh],
        scratch_shapes=[pltpu.VMEM((wq_t.shape[0], tt), F32),
                        pltpu.VMEM((nkeys, tt), F32), pltpu.VMEM((nkeys, tt), F32),
                        pltpu.VMEM((ncol, PEER_TOPK, LANES), F32), pltpu.VMEM((ncol, PEER_TOPK, LANES), F32),
                        pltpu.VMEM((ncol, ncand, LANES), F32), pltpu.VMEM((ncol, ncand, LANES), F32),
                        pltpu.SMEM((ncol,), F32)],
        compiler_params=_cparams(("parallel",)), name="peer_route",
    )(h2t, wq_t, k1, k2)


def _pack_kernel(x_ref, o_ref, *, transpose):
    x = x_ref[...]
    if transpose:
        x = x.T
    o_ref[...] = pltpu.bitcast(x.astype(BF16), jnp.uint32)


def _pack_table(tab, *, transpose):
    r, c = tab.shape
    tb = 512
    if transpose:
        in_spec = pl.BlockSpec((tb, c), lambda i: (i, 0))
        out_spec = pl.BlockSpec((c // 2, tb), lambda i: (0, i))
        out_shape = jax.ShapeDtypeStruct((c // 2, r), jnp.uint32)
    else:
        in_spec = pl.BlockSpec((tb, c), lambda i: (i, 0))
        out_spec = pl.BlockSpec((tb // 2, c), lambda i: (i, 0))
        out_shape = jax.ShapeDtypeStruct((r // 2, c), jnp.uint32)
    return pl.pallas_call(
        functools.partial(_pack_kernel, transpose=transpose), grid=(r // tb,),
        in_specs=[in_spec], out_specs=out_spec, out_shape=out_shape,
        compiler_params=_cparams(("parallel",)), name="pack_table",
    )(tab)


def _ffn_kernel(h_ref, u_ref, un_ref, vt_ref, r2_ref, e2_ref, n1_ref, e1_ref, o_ref, acc_ref, at_ref, pt_ref,
                *, nheads, nkeys, cw):
    j = pl.program_id(1)
    tt = h_ref.shape[1]
    et = 2 * u_ref.shape[0]
    nhalf = cw // LANES

    def first_pre_activations(table_ref):
        at_ref[0] = _dot(pltpu.bitcast(table_ref[0:at_ref.shape[1] // 2, :], BF16), h_ref[:, 0:cw])

    @pl.when(j == 0)
    def _():
        acc_ref[...] = jnp.zeros_like(acc_ref)
        first_pre_activations(u_ref)

    ktiles = nkeys // BF16_ROWS
    es = at_ref.shape[1]
    nsub = es // nkeys
    units = [(eh, c) for eh in range(et // es) for c in range(tt // cw)]
    n_at, n_pt = at_ref.shape[0], pt_ref.shape[0]

    def pre_activations(k):
        eh, c = units[k]
        u = pltpu.bitcast(u_ref[eh * es // 2:(eh + 1) * es // 2, :], BF16)
        at_ref[k % n_at] = _dot(u, h_ref[:, c * cw:(c + 1) * cw])

    assert n_at >= len(units)
    for k, (eh, c) in enumerate(units):
        if k + 1 < len(units):
            pre_activations(k + 1)
        else:
            first_pre_activations(un_ref)
        for jj in range(nsub):
            row = eh * nsub + jj
            for hf in range(nhalf):
                ck = c * nhalf + hf
                cols = slice(ck * LANES, (ck + 1) * LANES)
                lanes = slice(hf * LANES, (hf + 1) * LANES)
                gate = jnp.zeros((ktiles, BF16_ROWS, LANES), BF16)
                for hd in range(nheads):
                    n_row = jnp.broadcast_to(n1_ref[hd, row:row + 1, cols], (BF16_ROWS, LANES)).astype(BF16)
                    e_row = jnp.broadcast_to(e1_ref[hd, row:row + 1, cols], (BF16_ROWS, LANES)).astype(BF16)
                    r2 = pltpu.bitcast(r2_ref[hd, ck], BF16).reshape(ktiles, BF16_ROWS, LANES)
                    e2 = pltpu.bitcast(e2_ref[hd, ck], BF16).reshape(ktiles, BF16_ROWS, LANES)
                    gate = gate + jnp.where(r2 < n_row[None], e2, 0) * e_row[None]
                a = at_ref[k % n_at, jj * nkeys:(jj + 1) * nkeys, lanes].astype(BF16)
                p = (gate * _gelu(a.reshape(ktiles, BF16_ROWS, LANES))).reshape(nkeys, LANES)
                pt_ref[k % n_pt, jj * nkeys // 2:(jj + 1) * nkeys // 2, lanes] = pltpu.bitcast(p, jnp.uint32)
        p = pltpu.bitcast(pt_ref[k % n_pt], BF16)
        vt = pltpu.bitcast(vt_ref[:, eh * es:(eh + 1) * es], BF16)
        acc_ref[c] += _dot(vt, p)

    @pl.when(j == pl.num_programs(1) - 1)
    def _():
        for c in range(tt // cw):
            o_ref[c * cw:(c + 1) * cw, :] = acc_ref[c].T


def _peer_experts(h2t, u_tab, v_tab_t, r2, e2, n1, e1):
    d, t = h2t.shape
    ne = 2 * u_tab.shape[0]
    nheads, nkeys, _ = n1.shape
    tt, et, cw = 1024, 1024, 256
    es = nkeys * SUBLANES
    at_ring, pt_ring = 4, 4
    kern = functools.partial(_ffn_kernel, nheads=nheads, nkeys=nkeys, cw=cw)
    full_keys = pl.BlockSpec((nheads, tt // LANES, nkeys // 2, LANES), lambda i, j: (0, i, 0, 0))
    sub_keys = pl.BlockSpec((nheads, et // nkeys, tt), lambda i, j: (0, j, i))
    return pl.pallas_call(
        kern, grid=(t // tt, ne // et),
        in_specs=[pl.BlockSpec((d, tt), lambda i, j: (0, i)),
                  pl.BlockSpec((et // 2, d), lambda i, j: (j, 0)),
                  pl.BlockSpec((es // 2, d), lambda i, j: (jnp.minimum(j + 1, ne // et - 1) * (et // es), 0)),
                  pl.BlockSpec((d // 2, et), lambda i, j: (0, j)),
                  full_keys, full_keys, sub_keys, sub_keys],
        out_specs=pl.BlockSpec((tt, d), lambda i, j: (i, 0)),
        out_shape=jax.ShapeDtypeStruct((t, d), F32),
        scratch_shapes=[pltpu.VMEM((tt // cw, d, cw), F32),
                        pltpu.VMEM((at_ring, es, cw), F32),
                        pltpu.VMEM((pt_ring, es // 2, cw), jnp.uint32)],
        compiler_params=_cparams(("parallel", "arbitrary")), name="peer_experts",
    )(h2t, u_tab, u_tab, v_tab_t, r2, e2, n1, e1)


def _final_kernel(x_ref, dl_ref, g_ref, o_ref):
    o_ref[...] = _rmsnorm(x_ref[...] + dl_ref[...], g_ref[...])


def _final_norm(x, delta, g):
    t, d = x.shape
    tm = 1024
    blk = pl.BlockSpec((tm, d), lambda i: (i, 0))
    return pl.pallas_call(
        _final_kernel, grid=(t // tm,),
        in_specs=[blk, blk, pl.BlockSpec((1, d), lambda i: (0, 0))],
        out_specs=blk, out_shape=jax.ShapeDtypeStruct((t, d), F32),
        compiler_params=_cparams(("parallel",)), name="final_norm",
    )(x, delta, g.reshape(1, d))


def _swap_halves(w):
    d, c = w.shape
    half = HEAD_DIM // 2
    return w.reshape(d, c // HEAD_DIM, 2, half)[:, :, ::-1, :].reshape(d, c)


def kernel(x, positions, norm1_g, w_in, pool_w, pool_scale, pool_proj, ssm_a_re, ssm_a_im, ssm_log_dt,
           ssm_b_re, ssm_b_im, ssm_c_re, ssm_c_im, ssm_d, ssm_glu, ssm_proj, attn_proj, w_out, norm2_g,
           peer_wq, peer_k1, peer_k2, peer_u, peer_v, final_g):
    b, l, d = x.shape
    depth = w_in.shape[0]
    pool_w_cols = pool_proj.shape[1]
    ssm_w_cols = ssm_proj.shape[1]
    attn_w = len(ATTN_DILATIONS) * HEADS_PER_GROUP * HEAD_DIM
    cos, sin = _rope_tables(positions)
    delta = None
    for layer in range(depth):
        w = w_in[layer]
        o0 = pool_w_cols + ssm_w_cols
        wq, wk, wv, wg = (w[:, o0:o0 + attn_w], w[:, o0 + attn_w:o0 + 2 * attn_w],
                          w[:, o0 + 2 * attn_w:o0 + 3 * attn_w], w[:, o0 + 3 * attn_w:])
        w_ext = jnp.concatenate([w[:, :o0], wq, _swap_halves(wq), wk, _swap_halves(wk), wv, wg],
                                axis=1).astype(BF16)
        outs = _in_projection(x, delta, norm1_g[layer], w_ext, cos, sin,
                              pool_w=pool_w_cols, ssm_w=ssm_w_cols, attn_w=attn_w)
        u_pool, u_ssm, q, k, v, gates = outs[:6]
        if delta is not None:
            x = outs[6]
        pm = _pool_mixer(u_pool, pool_w[layer].astype(BF16), pool_scale[layer])
        abar_re, abar_im, bbar_re_t, bbar_im_t = _ssm_params(
            ssm_a_re[layer], ssm_a_im[layer], ssm_log_dt[layer], ssm_b_re[layer], ssm_b_im[layer])
        ys = _ssm_mixer(u_ssm, b, abar_re, abar_im, bbar_re_t, bbar_im_t,
                        ssm_c_re[layer], ssm_c_im[layer], ssm_d[layer])
        ao = _attention(q, k, v)
        x, h2t = _merge(pm, ys, ao, gates, x, pool_proj[layer].astype(BF16), ssm_glu[layer].astype(BF16),
                        ssm_proj[layer].astype(BF16), attn_proj[layer].astype(BF16),
                        w_out[layer].astype(BF16), norm2_g[layer])
        r2, e2, n1, e1 = _peer_route(h2t, peer_wq[layer].T.astype(BF16), peer_k1[layer].astype(BF16),
                                     peer_k2[layer].astype(BF16))
        delta = _peer_experts(h2t, _pack_table(peer_u[layer], transpose=False),
                              _pack_table(peer_v[layer], transpose=True), r2, e2, n1, e1).reshape(b, l, d)
    return _final_norm(x.reshape(b * l, d), delta.reshape(b * l, d), final_g).reshape(b, l, d)
```

```python
import functools
import math

import jax
import jax.numpy as jnp
from jax import lax
from jax.experimental import pallas as pl
from jax.experimental.pallas import tpu as pltpu

F32 = jnp.float32
BF16 = jnp.bfloat16

EPS = 1e-6
POOL_WINDOWS = (2, 4, 8, 16)
POOL_GROUP = 128
SSM_GROUP = 16
SSM_GROUPS = 32
SSM_STATE = 64
HEAD_DIM = 64
HEADS_PER_GROUP = 4
ATTN_DILATIONS = (1, 4, 16)
ATTN_BLOCK = 128
ROPE_THETA = 10000.0
PEER_HEADS = 8
PEER_TOPK = 16

LANES = 128
SUBLANES = 8
BF16_ROWS = 16
VMEM_LIMIT = 56 * 1024 * 1024
NEG_BIG = -1e30


def _cparams(sem):
    return pltpu.CompilerParams(dimension_semantics=sem, vmem_limit_bytes=VMEM_LIMIT)


def _gelu(x):
    c = math.sqrt(2.0 / math.pi)
    return x * (0.5 * (1.0 + jnp.tanh(c * (x + 0.044715 * (x * x * x)))))


def _sigmoid(x):
    return 1.0 / (1.0 + jnp.exp(-x))


def _rmsnorm(x, g):
    ms = jnp.mean(x * x, axis=-1, keepdims=True)
    return x * lax.rsqrt(ms + EPS) * g


def _dot(a, b):
    return jnp.dot(a, b, preferred_element_type=F32)


def _dot_nt(a, b):
    return lax.dot_general(a, b, (((1,), (1,)), ((), ())), preferred_element_type=F32)


def _rope_kernel(pos_ref, invf_ref, sign_ref, cos_ref, sin_ref):
    ang = pos_ref[...].astype(F32) * invf_ref[...]
    cos_ref[...] = jnp.cos(ang)
    sin_ref[...] = jnp.sin(ang) * sign_ref[...]


def _rope_tables(positions):
    b, l = positions.shape
    t = b * l
    half = HEAD_DIM // 2
    inv_freq = ROPE_THETA ** (-jnp.arange(half, dtype=F32) / half)
    invf = jnp.tile(inv_freq, LANES // half).reshape(1, LANES)
    sign = jnp.tile(jnp.concatenate([-jnp.ones((half,), F32), jnp.ones((half,), F32)]),
                    LANES // HEAD_DIM).reshape(1, LANES)
    pos_b = jnp.broadcast_to(positions.reshape(t, 1), (t, LANES))
    tr = 2048
    cos, sin = pl.pallas_call(
        _rope_kernel,
        grid=(t // tr,),
        in_specs=[pl.BlockSpec((tr, LANES), lambda i: (i, 0)),
                  pl.BlockSpec((1, LANES), lambda i: (0, 0)),
                  pl.BlockSpec((1, LANES), lambda i: (0, 0))],
        out_specs=[pl.BlockSpec((tr, LANES), lambda i: (i, 0))] * 2,
        out_shape=[jax.ShapeDtypeStruct((t, LANES), F32)] * 2,
        compiler_params=_cparams(("parallel",)),
        name="rope_tables",
    )(pos_b, invf, sign)
    return cos.reshape(b, l, LANES), sin.reshape(b, l, LANES)


def _inproj_kernel(*refs, has_delta, pool_w, ssm_w, attn_w, d_model):
    if has_delta:
        x_ref, dl_ref, g_ref, w_ref, cos_ref, sin_ref = refs[:6]
        outs = refs[6:]
    else:
        x_ref, g_ref, w_ref, cos_ref, sin_ref = refs[:5]
        outs = refs[5:]
    pool_ref, ssm_ref, q_ref, k_ref, v_ref, gate_ref = outs[:6]
    x = x_ref[0]
    if has_delta:
        x = x + dl_ref[0]
        outs[6][0] = x
    h = _rmsnorm(x, g_ref[...]).astype(BF16)
    c0 = 0
    def w_cols(lo, hi):
        return pltpu.bitcast(w_ref[:, lo:hi], BF16)

    z = _dot(h, w_cols(c0, c0 + pool_w + ssm_w))
    pool_ref[0] = z[:, :pool_w]
    ssm_ref[...] = z[:, pool_w:]
    c0 += pool_w + ssm_w
    cos = cos_ref[0]
    sin = sin_ref[0]
    nslab = attn_w // LANES
    for dst in (q_ref, k_ref):
        z = _dot(h, w_cols(c0, c0 + 2 * attn_w))
        for s in range(nslab):
            a = z[:, s * LANES:(s + 1) * LANES]
            a_sw = z[:, attn_w + s * LANES:attn_w + (s + 1) * LANES]
            dst[0, s] = a * cos + a_sw * sin
        c0 += 2 * attn_w
    z = _dot(h, w_cols(c0, c0 + attn_w))
    for s in range(nslab):
        v_ref[0, s] = z[:, s * LANES:(s + 1) * LANES]
    c0 += attn_w
    z = _dot(h, w_cols(c0, c0 + 3 * d_model))
    gate_ref[0] = pltpu.bitcast(_sigmoid(z).astype(BF16), jnp.uint32)


def _in_projection(x, delta, g, w_ext, cos, sin, *, pool_w, ssm_w, attn_w):
    b, l, d = x.shape
    tm = 256
    nslab = attn_w // LANES
    has_delta = delta is not None
    kern = functools.partial(_inproj_kernel, has_delta=has_delta, pool_w=pool_w, ssm_w=ssm_w,
                             attn_w=attn_w, d_model=d)
    row = lambda bi, i: (bi, i, 0)
    slab = lambda bi, i: (bi, 0, i, 0)
    const2 = lambda bi, i: (0, 0)
    in_specs = [pl.BlockSpec((1, tm, d), row)]
    args = [x]
    if has_delta:
        in_specs.append(pl.BlockSpec((1, tm, d), row))
        args.append(delta)
    in_specs += [pl.BlockSpec((1, d), const2),
                 pl.BlockSpec(w_ext.shape, const2, pipeline_mode=pl.Buffered(1)),
                 pl.BlockSpec((1, tm, LANES), row),
                 pl.BlockSpec((1, tm, LANES), row)]
    args += [g.reshape(1, d), w_ext, cos, sin]
    out_specs = [pl.BlockSpec((1, tm, pool_w), row),
                 pl.BlockSpec((tm, ssm_w), lambda bi, i: (i, bi)),
                 pl.BlockSpec((1, nslab, tm, LANES), slab),
                 pl.BlockSpec((1, nslab, tm, LANES), slab),
                 pl.BlockSpec((1, nslab, tm, LANES), slab),
                 pl.BlockSpec((1, tm // 2, 3 * d), row)]
    out_shape = [jax.ShapeDtypeStruct((b, l, pool_w), F32),
                 jax.ShapeDtypeStruct((l, b * ssm_w), F32),
                 jax.ShapeDtypeStruct((b, nslab, l, LANES), F32),
                 jax.ShapeDtypeStruct((b, nslab, l, LANES), F32),
                 jax.ShapeDtypeStruct((b, nslab, l, LANES), F32),
                 jax.ShapeDtypeStruct((b, l // 2, 3 * d), jnp.uint32)]
    if has_delta:
        out_specs.append(pl.BlockSpec((1, tm, d), row))
        out_shape.append(jax.ShapeDtypeStruct((b, l, d), F32))
    return pl.pallas_call(
        kern, grid=(b, l // tm), in_specs=in_specs, out_specs=out_specs, out_shape=out_shape,
        compiler_params=_cparams(("parallel", "parallel")), name="in_projection",
    )(*args)


def _pool_kernel(u_ref, halo_ref, w_ref, scale_ref, o_ref, *, tm, halo):
    i = pl.program_id(1)
    cur = u_ref[0]
    prev = jnp.where(i > 0, halo_ref[0], 0.0)
    ext = jnp.concatenate([prev, cur], axis=0)
    t_idx = i * tm + lax.broadcasted_iota(jnp.int32, (tm, POOL_GROUP), 0)
    for gi, win in enumerate(POOL_WINDOWS):
        cs = slice(gi * POOL_GROUP, (gi + 1) * POOL_GROUP)
        s = ext[:, cs]
        sh = 1
        while sh < win:
            s = s + pltpu.roll(s, sh, axis=0)
            sh *= 2
        count = jnp.minimum(t_idx + 1, win).astype(F32)
        pooled = s[halo:, :] / count - cur[:, cs]
        mixed = _dot(pooled.astype(BF16), w_ref[gi]) * scale_ref[:, cs]
        o_ref[0, :, cs] = mixed.astype(BF16)


def _pool_mixer(u, w, scale):
    b, l, c = u.shape
    tm, halo = 512, 16
    assert halo >= max(POOL_WINDOWS) and halo % SUBLANES == 0
    kern = functools.partial(_pool_kernel, tm=tm, halo=halo)
    return pl.pallas_call(
        kern, grid=(b, l // tm),
        in_specs=[pl.BlockSpec((1, tm, c), lambda bi, i: (bi, i, 0)),
                  pl.BlockSpec((1, halo, c), lambda bi, i: (bi, jnp.maximum(i * (tm // halo) - 1, 0), 0)),
                  pl.BlockSpec(w.shape, lambda bi, i: (0, 0, 0)),
                  pl.BlockSpec((1, c), lambda bi, i: (0, 0))],
        out_specs=pl.BlockSpec((1, tm, c), lambda bi, i: (bi, i, 0)),
        out_shape=jax.ShapeDtypeStruct((b, l, c), BF16),
        compiler_params=_cparams(("parallel", "parallel")), name="pool_mixer",
    )(u, u, w, scale.reshape(1, c))


def _ssm_param_kernel(are_ref, aim_ref, ldt_ref, bre_ref, bim_ref, oar_ref, oai_ref, obr_ref, obi_ref):
    ar = are_ref[...]
    ai = aim_ref[...]
    dt = jnp.exp(ldt_ref[...])
    decay = jnp.exp(ar * dt)
    abar_re = decay * jnp.cos(ai * dt)
    abar_im = decay * jnp.sin(ai * dt)
    den = ar * ar + ai * ai
    nr = abar_re - 1.0
    k_re = (nr * ar + abar_im * ai) / den
    k_im = (abar_im * ar - nr * ai) / den
    oar_ref[...] = abar_re
    oai_ref[...] = abar_im
    for gi in range(are_ref.shape[0]):
        kr = k_re[gi:gi + 1, :]
        ki = k_im[gi:gi + 1, :]
        br = bre_ref[gi]
        bi = bim_ref[gi]
        obr_ref[gi] = kr * br - ki * bi
        obi_ref[gi] = kr * bi + ki * br


def _ssm_params(a_re, a_im, log_dt, b_re, b_im):
    g, n, p = b_re.shape
    brt = jnp.transpose(b_re, (0, 2, 1))
    bit = jnp.transpose(b_im, (0, 2, 1))
    return pl.pallas_call(
        _ssm_param_kernel,
        out_shape=[jax.ShapeDtypeStruct((g, n), F32)] * 2 + [jax.ShapeDtypeStruct((g, p, n), F32)] * 2,
        name="ssm_params",
    )(a_re, a_im, log_dt.reshape(g, 1), brt, bit)


def _block_diag(m):
    k, r, c = m.shape
    eye = jnp.eye(k, dtype=m.dtype)
    return (m[:, :, None, :] * eye[:, None, :, None]).reshape(k * r, k * c)


def _ssm_kernel(u_ref, bre_ref, bim_ref, are_ref, aim_ref, cre_ref, cim_ref, d_ref, y_ref,
                st_ref, xre_ref, xim_ref, sre_ref, sim_ref, *, tc, hw):
    rows = tc * SUBLANES

    @pl.when(pl.program_id(0) == 0)
    def _():
        st_ref[...] = jnp.zeros_like(st_ref)

    u = u_ref[...]
    odd = (lax.broadcasted_iota(jnp.int32, (rows, hw), 0) % 2) == 1
    ub = u.astype(BF16)
    zero = jnp.zeros_like(ub)
    u_exp = jnp.concatenate([jnp.where(odd, zero, ub), jnp.where(odd, ub, zero)], axis=1)
    xre_ref[...] = _dot(u_exp, bre_ref[...])
    xim_ref[...] = _dot(u_exp, bim_ref[...])
    a_re = are_ref[...]
    a_im = aim_ref[...]

    def step(t, carry):
        s_re, s_im = carry
        r0 = pl.multiple_of(t * SUBLANES, SUBLANES)
        n_re = a_re * s_re - a_im * s_im + xre_ref[pl.ds(r0, SUBLANES), :]
        n_im = a_re * s_im + a_im * s_re + xim_ref[pl.ds(r0, SUBLANES), :]
        sre_ref[pl.ds(r0, SUBLANES), :] = n_re
        sim_ref[pl.ds(r0, SUBLANES), :] = n_im
        return n_re, n_im

    s_re, s_im = lax.fori_loop(0, tc, step, (st_ref[0], st_ref[1]), unroll=8)
    st_ref[0] = s_re
    st_ref[1] = s_im
    yf = _dot(sre_ref[...].astype(BF16), cre_ref[...]) - _dot(sim_ref[...].astype(BF16), cim_ref[...])
    y = jnp.where(odd, yf[:, hw:], yf[:, :hw]) + pltpu.repeat(d_ref[...], tc, axis=0) * u
    y_ref[...] = _gelu(y).astype(BF16)


def _ssm_mixer(u_scan, batch, abar_re, abar_im, bbar_re_t, bbar_im_t, c_re, c_im, d_skip):
    l = u_scan.shape[0]
    g, p, n = bbar_re_t.shape
    hg = g // 2
    hw = hg * p
    hs = hg * n
    assert batch * 2 == SUBLANES
    rows_total = l * SUBLANES
    u2 = u_scan.reshape(rows_total, hw)

    def stack_b(bt):
        return jnp.concatenate([_block_diag(bt[:hg]), _block_diag(bt[hg:])], axis=0).astype(BF16)

    def cat_c(c):
        ct = jnp.transpose(c, (0, 2, 1))
        return jnp.concatenate([_block_diag(ct[:hg]), _block_diag(ct[hg:])], axis=1).astype(BF16)

    def tile_a(a):
        return jnp.tile(a.reshape(2, hs), (batch, 1))

    d8 = jnp.tile(d_skip.reshape(2, hw), (batch, 1))
    tc = 64
    rows = tc * SUBLANES
    kern = functools.partial(_ssm_kernel, tc=tc, hw=hw)
    const = lambda i: (0, 0)
    y = pl.pallas_call(
        kern, grid=(l // tc,),
        in_specs=[pl.BlockSpec((rows, hw), lambda i: (i, 0)),
                  pl.BlockSpec((2 * hw, hs), const), pl.BlockSpec((2 * hw, hs), const),
                  pl.BlockSpec((SUBLANES, hs), const), pl.BlockSpec((SUBLANES, hs), const),
                  pl.BlockSpec((hs, 2 * hw), const), pl.BlockSpec((hs, 2 * hw), const),
                  pl.BlockSpec((SUBLANES, hw), const)],
        out_specs=pl.BlockSpec((rows, hw), lambda i: (i, 0)),
        out_shape=jax.ShapeDtypeStruct((rows_total, hw), BF16),
        scratch_shapes=[pltpu.VMEM((2, SUBLANES, hs), F32),
                        pltpu.VMEM((rows, hs), F32), pltpu.VMEM((rows, hs), F32),
                        pltpu.VMEM((rows, hs), F32), pltpu.VMEM((rows, hs), F32)],
        compiler_params=_cparams(("arbitrary",)), name="ssm_mixer",
    )(u2, stack_b(bbar_re_t), stack_b(bbar_im_t), tile_a(abar_re), tile_a(abar_im),
      cat_c(c_re), cat_c(c_im), d8)
    return y.reshape(l, batch * 2 * hw)


def _attn_group(q_ref, k_ref, v_ref, kb, vb, acc, mrun, drun, og, mg, lg, *, d, ta, i):
    p = ATTN_BLOCK * d
    ncombo = ta // ATTN_BLOCK
    nslab = q_ref.shape[1]
    scale = HEAD_DIM ** -0.5
    kb[:, p:p + ta, :] = k_ref[0]
    vb[:, p:p + ta, :] = v_ref[0]

    @pl.when(i == 0)
    def _():
        kb[:, :p, :] = jnp.zeros((nslab, p, LANES), F32)
        vb[:, :p, :] = jnp.zeros((nslab, p, LANES), F32)

    nh = LANES // HEAD_DIM
    qi = lax.broadcasted_iota(jnp.int32, (nh * ATTN_BLOCK, ATTN_BLOCK), 0) % ATTN_BLOCK
    kk = lax.broadcasted_iota(jnp.int32, (nh * ATTN_BLOCK, ATTN_BLOCK), 1)
    mask_cur = kk <= qi
    mask_prev = kk >= qi
    low = lax.broadcasted_iota(jnp.int32, (ATTN_BLOCK, ATTN_BLOCK), 1) < HEAD_DIM

    def rows_at(start):
        return pl.ds(start, ATTN_BLOCK) if d == 1 else pl.ds(start, ATTN_BLOCK, stride=d)

    def combo(c, carry):
        r = c % d
        n = c // d
        qs = n * p + r
        has_prev = jnp.logical_or(i > 0, n > 0)
        mask_p = jnp.logical_and(mask_prev, has_prev)
        for s in range(nslab):
            q = q_ref[0, s, rows_at(qs), :] * scale
            kc = kb[s, rows_at(p + qs), :].astype(BF16)
            kp = kb[s, rows_at(qs), :].astype(BF16)
            vc = vb[s, rows_at(p + qs), :].astype(BF16)
            vp = vb[s, rows_at(qs), :].astype(BF16)
            qh = jnp.concatenate([jnp.where(low, q, 0.0), jnp.where(low, 0.0, q)], axis=0).astype(BF16)
            sc = jnp.where(mask_cur, _dot_nt(qh, kc), -jnp.inf)
            sp = jnp.where(mask_p, _dot_nt(qh, kp), -jnp.inf)
            m = jnp.max(jnp.maximum(sc, sp), axis=1, keepdims=True)
            pc = jnp.exp(sc - m)
            pp = jnp.exp(sp - m)
            den = jnp.sum(pc + pp, axis=1, keepdims=True)
            o = _dot(pc.astype(BF16), vc) + _dot(pp.astype(BF16), vp)
            rows = rows_at(qs)
            og[s, rows, :] = jnp.where(low, o[:ATTN_BLOCK], o[ATTN_BLOCK:])
            mg[s, rows, :] = jnp.where(low, m[:ATTN_BLOCK], m[ATTN_BLOCK:])
            lg[s, rows, :] = jnp.where(low, den[:ATTN_BLOCK], den[ATTN_BLOCK:])
        return carry

    lax.fori_loop(0, ncombo, combo, 0, unroll=2)
    kb[:, :p, :] = kb[:, ta:ta + p, :]
    vb[:, :p, :] = vb[:, ta:ta + p, :]

    rc = 256
    for s in range(nslab):
        for r0 in range(0, ta, rc):
            rs = slice(r0, r0 + rc)
            m_old = mrun[s, rs, :]
            m_blk = mg[s, rs, :]
            m_new = jnp.maximum(m_old, m_blk)
            a_old = jnp.exp(m_old - m_new)
            a_blk = jnp.exp(m_blk - m_new)
            acc[s, rs, :] = acc[s, rs, :] * a_old + og[s, rs, :] * a_blk
            drun[s, rs, :] = drun[s, rs, :] * a_old + lg[s, rs, :] * a_blk
            mrun[s, rs, :] = m_new


def _attn_kernel(q_ref, k_ref, v_ref, o_ref, *scratch, ta):
    ng = len(ATTN_DILATIONS)
    kbs = scratch[0:2 * ng:2]
    vbs = scratch[1:2 * ng:2]
    acc, mrun, drun, og, mg, lg = scratch[2 * ng:]
    i = pl.program_id(1)
    g = pl.program_id(2)

    @pl.when(g == 0)
    def _():
        acc[...] = jnp.zeros_like(acc)
        drun[...] = jnp.zeros_like(drun)
        mrun[...] = jnp.full(mrun.shape, NEG_BIG, F32)

    for gi, d in enumerate(ATTN_DILATIONS):
        @pl.when(g == gi)
        def _(gi=gi, d=d):
            _attn_group(q_ref, k_ref, v_ref, kbs[gi], vbs[gi], acc, mrun, drun, og, mg, lg, d=d, ta=ta, i=i)

    @pl.when(g == ng - 1)
    def _():
        for s in range(acc.shape[0]):
            o_ref[0, :, s * LANES:(s + 1) * LANES] = acc[s] / drun[s]


def _attention(q, k, v):
    b, nslab, l, _ = q.shape
    ng = len(ATTN_DILATIONS)
    gs = nslab // ng
    ta = ATTN_BLOCK * max(ATTN_DILATIONS)
    assert l % ta == 0
    blk = pl.BlockSpec((1, gs, ta, LANES), lambda bi, i, g: (bi, g, i, 0))
    scratch = []
    for d in ATTN_DILATIONS:
        scratch += [pltpu.VMEM((gs, ATTN_BLOCK * d + ta, LANES), F32)] * 2
    scratch += [pltpu.VMEM((gs, ta, LANES), F32)] * 6
    return pl.pallas_call(
        functools.partial(_attn_kernel, ta=ta), grid=(b, l // ta, ng),
        in_specs=[blk, blk, blk],
        out_specs=pl.BlockSpec((1, ta, gs * LANES), lambda bi, i, g: (bi, i, 0)),
        out_shape=jax.ShapeDtypeStruct((b, l, gs * LANES), F32),
        scratch_shapes=scratch,
        compiler_params=_cparams(("arbitrary", "arbitrary", "arbitrary")), name="dilated_attention",
    )(q, k, v)


def _merge_kernel(pm_ref, ys_ref, ao_ref, gate_ref, x_ref, wp_ref, wg_ref, ws_ref, wa_ref, wo_ref, g2_ref,
                  xo_ref, h2_ref, *, d_model, ssm_w):
    def weight(ref):
        return pltpu.bitcast(ref[...], BF16)

    y_pool = _dot(pm_ref[0], weight(wp_ref))
    glu = _dot(ys_ref[...], weight(wg_ref))
    sg = (glu[:, :ssm_w] * _sigmoid(glu[:, ssm_w:])).astype(BF16)
    y_ssm = _dot(sg, weight(ws_ref))
    y_attn = _dot(ao_ref[0].astype(BF16), weight(wa_ref))
    gate = pltpu.bitcast(gate_ref[0], BF16).astype(F32)
    merged = (gate[:, :d_model] * y_pool + gate[:, d_model:2 * d_model] * y_ssm
              + gate[:, 2 * d_model:] * y_attn)
    xn = x_ref[0] + _dot(merged.astype(BF16), weight(wo_ref))
    xo_ref[0] = xn
    h2_ref[...] = _rmsnorm(xn, g2_ref[...]).T.astype(BF16)


def _merge(pm, ys, ao, gates, x, w_pool, w_glu, w_ssm, w_attn, w_out, g2):
    b, l, d = x.shape
    tm = 256
    ssm_w = 2 * w_glu.shape[0]
    row = lambda bi, i: (bi, i, 0)
    const = lambda bi, i: (0, 0)
    full = lambda a: pl.BlockSpec(a.shape, const)
    return pl.pallas_call(
        functools.partial(_merge_kernel, d_model=d, ssm_w=ssm_w), grid=(b, l // tm),
        in_specs=[pl.BlockSpec((1, tm, pm.shape[-1]), row),
                  pl.BlockSpec((tm, ssm_w), lambda bi, i: (i, bi)),
                  pl.BlockSpec((1, tm, ao.shape[-1]), row),
                  pl.BlockSpec((1, tm // 2, 3 * d), row),
                  pl.BlockSpec((1, tm, d), row),
                  full(w_pool), full(w_glu), full(w_ssm), full(w_attn), full(w_out),
                  pl.BlockSpec((1, d), const)],
        out_specs=[pl.BlockSpec((1, tm, d), row),
                   pl.BlockSpec((d, tm), lambda bi, i: (0, bi * (l // tm) + i))],
        out_shape=[jax.ShapeDtypeStruct((b, l, d), F32), jax.ShapeDtypeStruct((d, b * l), BF16)],
        compiler_params=_cparams(("parallel", "parallel")), name="branch_merge",
    )(pm, ys, ao, gates, x, w_pool, w_glu, w_ssm, w_attn, w_out, g2.reshape(1, d))


def _candidate_pairs(k):
    return [(a, b) for a in range(k) for b in range(k) if (a + 1) * (b + 1) <= k]


_CODE_UNIT = 2.0 ** 121
_CODE_BASE = 80


def _removed_code(r):
    return -float(_CODE_BASE + r) * _CODE_UNIT


def _removed_round(w):
    return w * (-1.0 / _CODE_UNIT) - float(_CODE_BASE)


def _vreg_rows(x):
    return x.reshape(x.shape[0] // SUBLANES, SUBLANES, x.shape[1])


def _max_all_rows(w):
    p = jnp.max(w, axis=0)
    for sh in (4, 2, 1):
        p = jnp.maximum(p, pltpu.roll(p, sh, axis=0))
    return p


def _topk_round(work, rows):
    m = jnp.max(work, axis=0, keepdims=True)
    idx = jnp.min(jnp.where(work == m, rows, float(work.shape[0])), axis=0, keepdims=True)
    hit = rows == idx
    return m, hit, jnp.where(hit, -jnp.inf, work)


def _route_kernel(h_ref, wq_ref, k1_ref, k2_ref, r2_ref, e2_ref, n1_ref, e1_ref,
                  qt_ref, s1_ref, s2_ref, v1_ref, v2_ref, cand_ref, sel_ref, tied_ref,
                  *, nheads, nkeys, topk, tt):
    qt_ref[...] = _dot(wq_ref[...], h_ref[...])
    hc = k1_ref.shape[1]
    pairs = _candidate_pairs(topk)
    ncand = cand_ref.shape[1]

    def head(hd, carry):
        base = pl.multiple_of(hd * 2 * hc, 2 * hc)
        q1 = qt_ref[pl.ds(base, hc), :].astype(BF16)
        q2 = qt_ref[pl.ds(base + hc, hc), :].astype(BF16)
        s1_ref[...] = _dot(k1_ref[...], q1)
        s2_ref[...] = _dot(k2_ref[...], q2)

        def column(j, exact):
            if isinstance(j, int):
                cols = slice(j * LANES, (j + 1) * LANES)
            else:
                cols = pl.ds(pl.multiple_of(j * LANES, LANES), LANES)
            s1 = s1_ref[:, cols]
            s2 = s2_ref[:, cols]
            v1, v2, cand, selr = v1_ref.at[j], v2_ref.at[j], cand_ref.at[j], sel_ref.at[j]

            def build_candidates():
                for ci, (a, bq) in enumerate(pairs):
                    cand[ci:ci + 1, :] = v1[a:a + 1, :] + v2[bq:bq + 1, :]
                if ncand > len(pairs):
                    cand[len(pairs):, :] = jnp.full((ncand - len(pairs), LANES), -jnp.inf, F32)
                return cand[...]

            def finish(rank1, rank2, sel, cv):
                selr[...] = sel
                top = v1[0:1, :] + v2[0:1, :]
                z = jnp.sum(sel * jnp.exp(jnp.where(sel > 0.0, cv, top) - top), axis=0, keepdims=True)
                n1 = jnp.zeros(s1.shape, F32)
                off = 0
                for a in range(topk):
                    cnt = topk // (a + 1)
                    n_a = jnp.sum(selr[off:off + cnt, :], axis=0, keepdims=True)
                    n1 = jnp.where(rank1 == float(a), n_a, n1)
                    off += cnt
                n1_ref[hd, :, cols] = n1
                e1_ref[hd, :, cols] = jnp.exp(s1 - v1[0:1, :]) / z
                r2_ref[hd, j] = pltpu.bitcast(rank2.astype(BF16), jnp.uint32)
                e2_ref[hd, j] = pltpu.bitcast(jnp.exp(s2 - v2[0:1, :]).astype(BF16), jnp.uint32)

            if exact:
                key_rows = lax.broadcasted_iota(jnp.int32, s1.shape, 0).astype(F32)
                x1, x2 = s1, s2
                rk1 = jnp.full(s1.shape, float(nkeys), F32)
                rk2 = rk1
                for r in range(topk):
                    m1, hit1, x1 = _topk_round(x1, key_rows)
                    m2, hit2, x2 = _topk_round(x2, key_rows)
                    rk1 = jnp.where(hit1, float(r), rk1)
                    rk2 = jnp.where(hit2, float(r), rk2)
                    v1[r:r + 1, :] = m1
                    v2[r:r + 1, :] = m2
                cvs = build_candidates()
                cand_rows = lax.broadcasted_iota(jnp.int32, cvs.shape, 0).astype(F32)
                xc = cvs
                sl = jnp.zeros(cvs.shape, F32)
                for r in range(topk):
                    _, hit, xc = _topk_round(xc, cand_rows)
                    sl = jnp.where(hit, 1.0, sl)
                finish(rk1, rk2, sl, cvs)
                return None

            w1, w2 = _vreg_rows(s1), _vreg_rows(s2)
            for r in range(topk):
                m1 = _max_all_rows(w1)
                m2 = _max_all_rows(w2)
                w1 = jnp.where(w1 == m1[None], _removed_code(r), w1)
                w2 = jnp.where(w2 == m2[None], _removed_code(r), w2)
                v1[r:r + 1, :] = m1[0:1, :]
                v2[r:r + 1, :] = m2[0:1, :]
            w1, w2 = w1.reshape(s1.shape), w2.reshape(s2.shape)
            gone1 = w1 <= _removed_code(0)
            gone2 = w2 <= _removed_code(0)
            rank1 = jnp.where(gone1, _removed_round(w1), float(nkeys))
            rank2 = jnp.where(gone2, _removed_round(w2), float(nkeys))
            cv = build_candidates()
            wc = _vreg_rows(cv)
            for r in range(topk):
                wc = jnp.where(wc == _max_all_rows(wc)[None], _removed_code(r), wc)
            sel = jnp.where(wc.reshape(cv.shape) <= _removed_code(0), 1.0, 0.0)
            removed = (jnp.sum(jnp.where(gone1, 1.0, 0.0), axis=0, keepdims=True)
                       + jnp.sum(jnp.where(gone2, 1.0, 0.0), axis=0, keepdims=True)
                       + jnp.sum(sel, axis=0, keepdims=True))
            finish(rank1, rank2, sel, cv)
            expected = float(3 * topk + ncand - len(pairs))
            return jnp.max(jnp.where(removed != expected, 1.0, 0.0))

        for j in range(tt // LANES):
            tied_ref[j] = column(j, exact=False)

        def redo(j, carry2):
            @pl.when(tied_ref[j] > 0.0)
            def _():
                column(j, exact=True)
            return carry2

        lax.fori_loop(0, tt // LANES, redo, 0)
        return carry

    lax.fori_loop(0, nheads, head, 0)


def _peer_route(h2t, wq_t, k1, k2):
    d, t = h2t.shape
    nkeys, hc = k1.shape
    nheads = wq_t.shape[0] // (2 * hc)
    tt = 512
    npairs = len(_candidate_pairs(PEER_TOPK))
    ncand = -(-npairs // SUBLANES) * SUBLANES
    ncol = tt // LANES
    kern = functools.partial(_route_kernel, nheads=nheads, nkeys=nkeys, topk=PEER_TOPK, tt=tt)
    kblk = pl.BlockSpec((nheads, tt // LANES, nkeys // 2, LANES), lambda i: (0, i, 0, 0))
    ksh = jax.ShapeDtypeStruct((nheads, t // LANES, nkeys // 2, LANES), jnp.uint32)
    oblk = pl.BlockSpec((nheads, nkeys, tt), lambda i: (0, 0, i))
    osh = jax.ShapeDtypeStruct((nheads, nkeys, t), F32)
    return pl.pallas_call(
        kern, grid=(t // tt,),
        in_specs=[pl.BlockSpec((d, tt), lambda i: (0, i)),
                  pl.BlockSpec(wq_t.shape, lambda i: (0, 0)),
                  pl.BlockSpec(k1.shape, lambda i: (0, 0)),
                  pl.BlockSpec(k2.shape, lambda i: (0, 0))],
        out_specs=[kblk, kblk, oblk, oblk], out_shape=[ksh, ksh, osh, osh],
        scratch_shapes=[pltpu.VMEM((wq_t.shape[0], tt), F32),
                        pltpu.VMEM((nkeys, tt), F32), pltpu.VMEM((nkeys, tt), F32),
                        pltpu.VMEM((ncol, PEER_TOPK, LANES), F32), pltpu.VMEM((ncol, PEER_TOPK, LANES), F32),
                        pltpu.VMEM((ncol, ncand, LANES), F32), pltpu.VMEM((ncol, ncand, LANES), F32),
                        pltpu.SMEM((ncol,), F32)],
        compiler_params=_cparams(("parallel",)), name="peer_route",
    )(h2t, wq_t, k1, k2)


def _pack_kernel(x_ref, o_ref, *, transpose):
    x = x_ref[...]
    if transpose:
        x = x.T
    o_ref[...] = pltpu.bitcast(x.astype(BF16), jnp.uint32)


def _pack_table(tab, *, transpose):
    r, c = tab.shape
    tb = min(r, 512 if c <= 1024 else 128)
    if transpose:
        in_spec = pl.BlockSpec((tb, c), lambda i: (i, 0))
        out_spec = pl.BlockSpec((c // 2, tb), lambda i: (0, i))
        out_shape = jax.ShapeDtypeStruct((c // 2, r), jnp.uint32)
    else:
        in_spec = pl.BlockSpec((tb, c), lambda i: (i, 0))
        out_spec = pl.BlockSpec((tb // 2, c), lambda i: (i, 0))
        out_shape = jax.ShapeDtypeStruct((r // 2, c), jnp.uint32)
    return pl.pallas_call(
        functools.partial(_pack_kernel, transpose=transpose), grid=(r // tb,),
        in_specs=[in_spec], out_specs=out_spec, out_shape=out_shape,
        compiler_params=_cparams(("parallel",)), name="pack_table",
    )(tab)


def _ffn_kernel(h_ref, u_ref, un_ref, vt_ref, r2_ref, e2_ref, n1_ref, e1_ref, o_ref, acc_ref, at_ref, pt_ref,
                *, nheads, nkeys, cw):
    j = pl.program_id(1)
    tt = h_ref.shape[1]
    et = 2 * u_ref.shape[0]
    nhalf = cw // LANES

    def first_pre_activations(table_ref):
        at_ref[0] = _dot(pltpu.bitcast(table_ref[0:at_ref.shape[1] // 2, :], BF16), h_ref[:, 0:cw])

    @pl.when(j == 0)
    def _():
        acc_ref[...] = jnp.zeros_like(acc_ref)
        first_pre_activations(u_ref)

    ktiles = nkeys // BF16_ROWS
    es = at_ref.shape[1]
    nsub = es // nkeys
    units = [(eh, c) for eh in range(et // es) for c in range(tt // cw)]
    n_at, n_pt = at_ref.shape[0], pt_ref.shape[0]

    def pre_activations(k):
        eh, c = units[k]
        u = pltpu.bitcast(u_ref[eh * es // 2:(eh + 1) * es // 2, :], BF16)
        at_ref[k % n_at] = _dot(u, h_ref[:, c * cw:(c + 1) * cw])

    assert n_at >= len(units)
    for k, (eh, c) in enumerate(units):
        if k + 1 < len(units):
            pre_activations(k + 1)
        else:
            first_pre_activations(un_ref)
        for jj in range(nsub):
            row = eh * nsub + jj
            for hf in range(nhalf):
                ck = c * nhalf + hf
                cols = slice(ck * LANES, (ck + 1) * LANES)
                lanes = slice(hf * LANES, (hf + 1) * LANES)
                gate = jnp.zeros((ktiles, BF16_ROWS, LANES), BF16)
                for hd in range(nheads):
                    n_row = jnp.broadcast_to(n1_ref[hd, row:row + 1, cols], (BF16_ROWS, LANES)).astype(BF16)
                    e_row = jnp.broadcast_to(e1_ref[hd, row:row + 1, cols], (BF16_ROWS, LANES)).astype(BF16)
                    r2 = pltpu.bitcast(r2_ref[hd, ck], BF16).reshape(ktiles, BF16_ROWS, LANES)
                    e2 = pltpu.bitcast(e2_ref[hd, ck], BF16).reshape(ktiles, BF16_ROWS, LANES)
                    gate = gate + jnp.where(r2 < n_row[None], e2, 0) * e_row[None]
                a = at_ref[k % n_at, jj * nkeys:(jj + 1) * nkeys, lanes].astype(BF16)
                p = (gate * _gelu(a.reshape(ktiles, BF16_ROWS, LANES))).reshape(nkeys, LANES)
                pt_ref[k % n_pt, jj * nkeys // 2:(jj + 1) * nkeys // 2, lanes] = pltpu.bitcast(p, jnp.uint32)
        p = pltpu.bitcast(pt_ref[k % n_pt], BF16)
        vt = pltpu.bitcast(vt_ref[:, eh * es:(eh + 1) * es], BF16)
        acc_ref[c] += _dot(vt, p)

    @pl.when(j == pl.num_programs(1) - 1)
    def _():
        for c in range(tt // cw):
            o_ref[c * cw:(c + 1) * cw, :] = acc_ref[c].T


def _peer_experts(h2t, u_tab, v_tab_t, r2, e2, n1, e1):
    d, t = h2t.shape
    ne = 2 * u_tab.shape[0]
    nheads, nkeys, _ = n1.shape
    tt, et, cw = 1024, 1024, 256
    es = nkeys * SUBLANES
    at_ring, pt_ring = 4, 4
    kern = functools.partial(_ffn_kernel, nheads=nheads, nkeys=nkeys, cw=cw)
    full_keys = pl.BlockSpec((nheads, tt // LANES, nkeys // 2, LANES), lambda i, j: (0, i, 0, 0))
    sub_keys = pl.BlockSpec((nheads, et // nkeys, tt), lambda i, j: (0, j, i))
    return pl.pallas_call(
        kern, grid=(t // tt, ne // et),
        in_specs=[pl.BlockSpec((d, tt), lambda i, j: (0, i)),
                  pl.BlockSpec((et // 2, d), lambda i, j: (j, 0)),
                  pl.BlockSpec((es // 2, d), lambda i, j: (jnp.minimum(j + 1, ne // et - 1) * (et // es), 0)),
                  pl.BlockSpec((d // 2, et), lambda i, j: (0, j)),
                  full_keys, full_keys, sub_keys, sub_keys],
        out_specs=pl.BlockSpec((tt, d), lambda i, j: (i, 0)),
        out_shape=jax.ShapeDtypeStruct((t, d), F32),
        scratch_shapes=[pltpu.VMEM((tt // cw, d, cw), F32),
                        pltpu.VMEM((at_ring, es, cw), F32),
                        pltpu.VMEM((pt_ring, es // 2, cw), jnp.uint32)],
        compiler_params=_cparams(("parallel", "arbitrary")), name="peer_experts",
    )(h2t, u_tab, u_tab, v_tab_t, r2, e2, n1, e1)


def _final_kernel(x_ref, dl_ref, g_ref, o_ref):
    o_ref[...] = _rmsnorm(x_ref[...] + dl_ref[...], g_ref[...])


def _final_norm(x, delta, g):
    t, d = x.shape
    tm = 1024
    blk = pl.BlockSpec((tm, d), lambda i: (i, 0))
    return pl.pallas_call(
        _final_kernel, grid=(t // tm,),
        in_specs=[blk, blk, pl.BlockSpec((1, d), lambda i: (0, 0))],
        out_specs=blk, out_shape=jax.ShapeDtypeStruct((t, d), F32),
        compiler_params=_cparams(("parallel",)), name="final_norm",
    )(x, delta, g.reshape(1, d))


def _swap_halves(w):
    d, c = w.shape
    half = HEAD_DIM // 2
    return w.reshape(d, c // HEAD_DIM, 2, half)[:, :, ::-1, :].reshape(d, c)


def kernel(x, positions, norm1_g, w_in, pool_w, pool_scale, pool_proj, ssm_a_re, ssm_a_im, ssm_log_dt,
           ssm_b_re, ssm_b_im, ssm_c_re, ssm_c_im, ssm_d, ssm_glu, ssm_proj, attn_proj, w_out, norm2_g,
           peer_wq, peer_k1, peer_k2, peer_u, peer_v, final_g):
    b, l, d = x.shape
    depth = w_in.shape[0]
    pool_w_cols = pool_proj.shape[1]
    ssm_w_cols = ssm_proj.shape[1]
    attn_w = len(ATTN_DILATIONS) * HEADS_PER_GROUP * HEAD_DIM
    cos, sin = _rope_tables(positions)
    delta = None
    for layer in range(depth):
        w = w_in[layer]
        o0 = pool_w_cols + ssm_w_cols
        wq, wk, wv, wg = (w[:, o0:o0 + attn_w], w[:, o0 + attn_w:o0 + 2 * attn_w],
                          w[:, o0 + 2 * attn_w:o0 + 3 * attn_w], w[:, o0 + 3 * attn_w:])
        w_ext = _pack_table(jnp.concatenate([w[:, :o0], wq, _swap_halves(wq), wk, _swap_halves(wk), wv, wg],
                                            axis=1), transpose=False)
        outs = _in_projection(x, delta, norm1_g[layer], w_ext, cos, sin,
                              pool_w=pool_w_cols, ssm_w=ssm_w_cols, attn_w=attn_w)
        u_pool, u_ssm, q, k, v, gates = outs[:6]
        if delta is not None:
            x = outs[6]
        pm = _pool_mixer(u_pool, pool_w[layer].astype(BF16), pool_scale[layer])
        abar_re, abar_im, bbar_re_t, bbar_im_t = _ssm_params(
            ssm_a_re[layer], ssm_a_im[layer], ssm_log_dt[layer], ssm_b_re[layer], ssm_b_im[layer])
        ys = _ssm_mixer(u_ssm, b, abar_re, abar_im, bbar_re_t, bbar_im_t,
                        ssm_c_re[layer], ssm_c_im[layer], ssm_d[layer])
        ao = _attention(q, k, v)
        packed = [_pack_table(wt[layer], transpose=False)
                  for wt in (pool_proj, ssm_glu, ssm_proj, attn_proj, w_out)]
        x, h2t = _merge(pm, ys, ao, gates, x, *packed, norm2_g[layer])
        r2, e2, n1, e1 = _peer_route(h2t, peer_wq[layer].T.astype(BF16), peer_k1[layer].astype(BF16),
                                     peer_k2[layer].astype(BF16))
        delta = _peer_experts(h2t, _pack_table(peer_u[layer], transpose=False),
                              _pack_table(peer_v[layer], transpose=True), r2, e2, n1, e1).reshape(b, l, d)
    return _final_norm(x.reshape(b * l, d), delta.reshape(b * l, d), final_g).reshape(b, l, d)
```

```python
import functools
import math

import jax
import jax.numpy as jnp
from jax import lax
from jax.experimental import pallas as pl
from jax.experimental.pallas import tpu as pltpu

F32 = jnp.float32
BF16 = jnp.bfloat16

EPS = 1e-6
POOL_WINDOWS = (2, 4, 8, 16)
POOL_GROUP = 128
SSM_GROUP = 16
SSM_GROUPS = 32
SSM_STATE = 64
HEAD_DIM = 64
HEADS_PER_GROUP = 4
ATTN_DILATIONS = (1, 4, 16)
ATTN_BLOCK = 128
ROPE_THETA = 10000.0
PEER_HEADS = 8
PEER_TOPK = 16

LANES = 128
SUBLANES = 8
BF16_ROWS = 16
VMEM_LIMIT = 56 * 1024 * 1024
NEG_BIG = -1e30


def _cparams(sem):
    return pltpu.CompilerParams(dimension_semantics=sem, vmem_limit_bytes=VMEM_LIMIT)


def _gelu(x):
    c = math.sqrt(2.0 / math.pi)
    return x * (0.5 * (1.0 + jnp.tanh(c * (x + 0.044715 * (x * x * x)))))


def _sigmoid(x):
    return 1.0 / (1.0 + jnp.exp(-x))


def _rmsnorm(x, g):
    ms = jnp.mean(x * x, axis=-1, keepdims=True)
    return x * lax.rsqrt(ms + EPS) * g


def _dot(a, b):
    return jnp.dot(a, b, preferred_element_type=F32)


def _dot_nt(a, b):
    return lax.dot_general(a, b, (((1,), (1,)), ((), ())), preferred_element_type=F32)


def _rope_kernel(pos_ref, invf_ref, sign_ref, cos_ref, sin_ref):
    ang = pos_ref[...].astype(F32) * invf_ref[...]
    cos_ref[...] = jnp.cos(ang)
    sin_ref[...] = jnp.sin(ang) * sign_ref[...]


def _rope_tables(positions):
    b, l = positions.shape
    t = b * l
    half = HEAD_DIM // 2
    inv_freq = ROPE_THETA ** (-jnp.arange(half, dtype=F32) / half)
    invf = jnp.tile(inv_freq, LANES // half).reshape(1, LANES)
    sign = jnp.tile(jnp.concatenate([-jnp.ones((half,), F32), jnp.ones((half,), F32)]),
                    LANES // HEAD_DIM).reshape(1, LANES)
    pos_b = jnp.broadcast_to(positions.reshape(t, 1), (t, LANES))
    tr = 2048
    cos, sin = pl.pallas_call(
        _rope_kernel,
        grid=(t // tr,),
        in_specs=[pl.BlockSpec((tr, LANES), lambda i: (i, 0)),
                  pl.BlockSpec((1, LANES), lambda i: (0, 0)),
                  pl.BlockSpec((1, LANES), lambda i: (0, 0))],
        out_specs=[pl.BlockSpec((tr, LANES), lambda i: (i, 0))] * 2,
        out_shape=[jax.ShapeDtypeStruct((t, LANES), F32)] * 2,
        compiler_params=_cparams(("parallel",)),
        name="rope_tables",
    )(pos_b, invf, sign)
    return cos.reshape(b, l, LANES), sin.reshape(b, l, LANES)


def _inproj_kernel(x_ref, g_ref, w_ref, cos_ref, sin_ref, pool_ref, ssm_ref, q_ref, k_ref, v_ref, gate_ref,
                   *, pool_w, ssm_w, attn_w, d_model):
    h = _rmsnorm(x_ref[0], g_ref[...]).astype(BF16)
    c0 = 0

    def w_cols(lo, hi):
        return pltpu.bitcast(w_ref[:, lo:hi], BF16)

    z = _dot(h, w_cols(c0, c0 + pool_w + ssm_w))
    pool_ref[0] = z[:, :pool_w]
    ssm_ref[...] = z[:, pool_w:]
    c0 += pool_w + ssm_w
    cos = cos_ref[0]
    sin = sin_ref[0]
    nslab = attn_w // LANES
    half = HEAD_DIM // 2
    first_half = lax.broadcasted_iota(jnp.int32, cos.shape, 1) % HEAD_DIM < half
    for dst in (q_ref, k_ref):
        z = _dot(h, w_cols(c0, c0 + attn_w))
        for s in range(nslab):
            a = z[:, s * LANES:(s + 1) * LANES]
            a_sw = jnp.where(first_half, pltpu.roll(a, LANES - half, axis=1), pltpu.roll(a, half, axis=1))
            dst[0, s] = a * cos + a_sw * sin
        c0 += attn_w
    z = _dot(h, w_cols(c0, c0 + attn_w))
    for s in range(nslab):
        v_ref[0, s] = z[:, s * LANES:(s + 1) * LANES]
    c0 += attn_w
    z = _dot(h, w_cols(c0, c0 + 3 * d_model))
    gate_ref[0] = pltpu.bitcast(_sigmoid(z).astype(BF16), jnp.uint32)


def _in_projection(x, g, w_packed, cos, sin, *, pool_w, ssm_w, attn_w):
    b, l, d = x.shape
    tm = 256
    nslab = attn_w // LANES
    kern = functools.partial(_inproj_kernel, pool_w=pool_w, ssm_w=ssm_w, attn_w=attn_w, d_model=d)
    row = lambda bi, i: (bi, i, 0)
    slab = lambda bi, i: (bi, 0, i, 0)
    const2 = lambda bi, i: (0, 0)
    in_specs = [pl.BlockSpec((1, tm, d), row),
                pl.BlockSpec((1, d), const2),
                pl.BlockSpec(w_packed.shape, const2, pipeline_mode=pl.Buffered(1)),
                pl.BlockSpec((1, tm, LANES), row),
                pl.BlockSpec((1, tm, LANES), row)]
    args = [x, g.reshape(1, d), w_packed, cos, sin]
    out_specs = [pl.BlockSpec((1, tm, pool_w), row),
                 pl.BlockSpec((tm, ssm_w), lambda bi, i: (i, bi)),
                 pl.BlockSpec((1, nslab, tm, LANES), slab),
                 pl.BlockSpec((1, nslab, tm, LANES), slab),
                 pl.BlockSpec((1, nslab, tm, LANES), slab),
                 pl.BlockSpec((1, tm // 2, 3 * d), row)]
    out_shape = [jax.ShapeDtypeStruct((b, l, pool_w), F32),
                 jax.ShapeDtypeStruct((l, b * ssm_w), F32),
                 jax.ShapeDtypeStruct((b, nslab, l, LANES), F32),
                 jax.ShapeDtypeStruct((b, nslab, l, LANES), F32),
                 jax.ShapeDtypeStruct((b, nslab, l, LANES), F32),
                 jax.ShapeDtypeStruct((b, l // 2, 3 * d), jnp.uint32)]
    return pl.pallas_call(
        kern, grid=(b, l // tm), in_specs=in_specs, out_specs=out_specs, out_shape=out_shape,
        compiler_params=_cparams(("parallel", "parallel")), name="in_projection",
    )(*args)


def _pool_kernel(u_ref, halo_ref, w_ref, scale_ref, o_ref, *, tm, halo):
    i = pl.program_id(1)
    cur = u_ref[0]
    prev = jnp.where(i > 0, halo_ref[0], 0.0)
    ext = jnp.concatenate([prev, cur], axis=0)
    t_idx = i * tm + lax.broadcasted_iota(jnp.int32, (tm, POOL_GROUP), 0)
    for gi, win in enumerate(POOL_WINDOWS):
        cs = slice(gi * POOL_GROUP, (gi + 1) * POOL_GROUP)
        s = ext[:, cs]
        sh = 1
        while sh < win:
            s = s + pltpu.roll(s, sh, axis=0)
            sh *= 2
        count = jnp.minimum(t_idx + 1, win).astype(F32)
        pooled = s[halo:, :] / count - cur[:, cs]
        mixed = _dot(pooled.astype(BF16), w_ref[gi]) * scale_ref[:, cs]
        o_ref[0, :, cs] = mixed.astype(BF16)


def _pool_mixer(u, w, scale):
    b, l, c = u.shape
    tm, halo = 512, 16
    assert halo >= max(POOL_WINDOWS) and halo % SUBLANES == 0
    kern = functools.partial(_pool_kernel, tm=tm, halo=halo)
    return pl.pallas_call(
        kern, grid=(b, l // tm),
        in_specs=[pl.BlockSpec((1, tm, c), lambda bi, i: (bi, i, 0)),
                  pl.BlockSpec((1, halo, c), lambda bi, i: (bi, jnp.maximum(i * (tm // halo) - 1, 0), 0)),
                  pl.BlockSpec(w.shape, lambda bi, i: (0, 0, 0)),
                  pl.BlockSpec((1, c), lambda bi, i: (0, 0))],
        out_specs=pl.BlockSpec((1, tm, c), lambda bi, i: (bi, i, 0)),
        out_shape=jax.ShapeDtypeStruct((b, l, c), BF16),
        compiler_params=_cparams(("parallel", "parallel")), name="pool_mixer",
    )(u, u, w, scale.reshape(1, c))


def _ssm_param_kernel(are_ref, aim_ref, ldt_ref, bre_ref, bim_ref, oar_ref, oai_ref, obr_ref, obi_ref):
    ar = are_ref[...]
    ai = aim_ref[...]
    dt = jnp.exp(ldt_ref[...])
    decay = jnp.exp(ar * dt)
    abar_re = decay * jnp.cos(ai * dt)
    abar_im = decay * jnp.sin(ai * dt)
    den = ar * ar + ai * ai
    nr = abar_re - 1.0
    k_re = (nr * ar + abar_im * ai) / den
    k_im = (abar_im * ar - nr * ai) / den
    oar_ref[...] = abar_re
    oai_ref[...] = abar_im
    for gi in range(are_ref.shape[0]):
        kr = k_re[gi:gi + 1, :]
        ki = k_im[gi:gi + 1, :]
        br = bre_ref[gi]
        bi = bim_ref[gi]
        obr_ref[gi] = kr * br - ki * bi
        obi_ref[gi] = kr * bi + ki * br


def _ssm_params(a_re, a_im, log_dt, b_re, b_im):
    g, n, p = b_re.shape
    brt = jnp.transpose(b_re, (0, 2, 1))
    bit = jnp.transpose(b_im, (0, 2, 1))
    return pl.pallas_call(
        _ssm_param_kernel,
        out_shape=[jax.ShapeDtypeStruct((g, n), F32)] * 2 + [jax.ShapeDtypeStruct((g, p, n), F32)] * 2,
        name="ssm_params",
    )(a_re, a_im, log_dt.reshape(g, 1), brt, bit)


def _block_diag(m):
    k, r, c = m.shape
    eye = jnp.eye(k, dtype=m.dtype)
    return (m[:, :, None, :] * eye[:, None, :, None]).reshape(k * r, k * c)


def _ssm_kernel(u_ref, bre_ref, bim_ref, are_ref, aim_ref, cre_ref, cim_ref, d_ref, y_ref,
                st_ref, xre_ref, xim_ref, sre_ref, sim_ref, *, tc, hw):
    rows = tc * SUBLANES

    @pl.when(pl.program_id(0) == 0)
    def _():
        st_ref[...] = jnp.zeros_like(st_ref)

    u = u_ref[...]
    odd = (lax.broadcasted_iota(jnp.int32, (rows, hw), 0) % 2) == 1
    ub = u.astype(BF16)
    zero = jnp.zeros_like(ub)
    u_exp = jnp.concatenate([jnp.where(odd, zero, ub), jnp.where(odd, ub, zero)], axis=1)
    xre_ref[...] = _dot(u_exp, bre_ref[...])
    xim_ref[...] = _dot(u_exp, bim_ref[...])
    a_re = are_ref[...]
    a_im = aim_ref[...]

    def step(t, carry):
        s_re, s_im = carry
        r0 = pl.multiple_of(t * SUBLANES, SUBLANES)
        n_re = a_re * s_re - a_im * s_im + xre_ref[pl.ds(r0, SUBLANES), :]
        n_im = a_re * s_im + a_im * s_re + xim_ref[pl.ds(r0, SUBLANES), :]
        sre_ref[pl.ds(r0, SUBLANES), :] = n_re
        sim_ref[pl.ds(r0, SUBLANES), :] = n_im
        return n_re, n_im

    s_re, s_im = lax.fori_loop(0, tc, step, (st_ref[0], st_ref[1]), unroll=8)
    st_ref[0] = s_re
    st_ref[1] = s_im
    yf = _dot(sre_ref[...].astype(BF16), cre_ref[...]) - _dot(sim_ref[...].astype(BF16), cim_ref[...])
    y = jnp.where(odd, yf[:, hw:], yf[:, :hw]) + pltpu.repeat(d_ref[...], tc, axis=0) * u
    y_ref[...] = _gelu(y).astype(BF16)


def _ssm_mixer(u_scan, batch, abar_re, abar_im, bbar_re_t, bbar_im_t, c_re, c_im, d_skip):
    l = u_scan.shape[0]
    g, p, n = bbar_re_t.shape
    hg = g // 2
    hw = hg * p
    hs = hg * n
    assert batch * 2 == SUBLANES
    rows_total = l * SUBLANES
    u2 = u_scan.reshape(rows_total, hw)

    def stack_b(bt):
        return jnp.concatenate([_block_diag(bt[:hg]), _block_diag(bt[hg:])], axis=0).astype(BF16)

    def cat_c(c):
        ct = jnp.transpose(c, (0, 2, 1))
        return jnp.concatenate([_block_diag(ct[:hg]), _block_diag(ct[hg:])], axis=1).astype(BF16)

    def tile_a(a):
        return jnp.tile(a.reshape(2, hs), (batch, 1))

    d8 = jnp.tile(d_skip.reshape(2, hw), (batch, 1))
    tc = 64
    rows = tc * SUBLANES
    kern = functools.partial(_ssm_kernel, tc=tc, hw=hw)
    const = lambda i: (0, 0)
    y = pl.pallas_call(
        kern, grid=(l // tc,),
        in_specs=[pl.BlockSpec((rows, hw), lambda i: (i, 0)),
                  pl.BlockSpec((2 * hw, hs), const), pl.BlockSpec((2 * hw, hs), const),
                  pl.BlockSpec((SUBLANES, hs), const), pl.BlockSpec((SUBLANES, hs), const),
                  pl.BlockSpec((hs, 2 * hw), const), pl.BlockSpec((hs, 2 * hw), const),
                  pl.BlockSpec((SUBLANES, hw), const)],
        out_specs=pl.BlockSpec((rows, hw), lambda i: (i, 0)),
        out_shape=jax.ShapeDtypeStruct((rows_total, hw), BF16),
        scratch_shapes=[pltpu.VMEM((2, SUBLANES, hs), F32),
                        pltpu.VMEM((rows, hs), F32), pltpu.VMEM((rows, hs), F32),
                        pltpu.VMEM((rows, hs), F32), pltpu.VMEM((rows, hs), F32)],
        compiler_params=_cparams(("arbitrary",)), name="ssm_mixer",
    )(u2, stack_b(bbar_re_t), stack_b(bbar_im_t), tile_a(abar_re), tile_a(abar_im),
      cat_c(c_re), cat_c(c_im), d8)
    return y.reshape(l, batch * 2 * hw)


def _attn_group(q_ref, k_ref, v_ref, kb, vb, acc, mrun, drun, og, mg, lg, *, d, ta, i):
    p = ATTN_BLOCK * d
    ncombo = ta // ATTN_BLOCK
    nslab = q_ref.shape[1]
    scale = HEAD_DIM ** -0.5
    kb[:, p:p + ta, :] = k_ref[0]
    vb[:, p:p + ta, :] = v_ref[0]

    @pl.when(i == 0)
    def _():
        kb[:, :p, :] = jnp.zeros((nslab, p, LANES), F32)
        vb[:, :p, :] = jnp.zeros((nslab, p, LANES), F32)

    nh = LANES // HEAD_DIM
    qi = lax.broadcasted_iota(jnp.int32, (nh * ATTN_BLOCK, ATTN_BLOCK), 0) % ATTN_BLOCK
    kk = lax.broadcasted_iota(jnp.int32, (nh * ATTN_BLOCK, ATTN_BLOCK), 1)
    mask_cur = kk <= qi
    mask_prev = kk >= qi
    low = lax.broadcasted_iota(jnp.int32, (ATTN_BLOCK, ATTN_BLOCK), 1) < HEAD_DIM

    def rows_at(start):
        return pl.ds(start, ATTN_BLOCK) if d == 1 else pl.ds(start, ATTN_BLOCK, stride=d)

    def combo(c, carry):
        r = c % d
        n = c // d
        qs = n * p + r
        has_prev = jnp.logical_or(i > 0, n > 0)
        mask_p = jnp.logical_and(mask_prev, has_prev)
        for s in range(nslab):
            q = q_ref[0, s, rows_at(qs), :] * scale
            kc = kb[s, rows_at(p + qs), :].astype(BF16)
            kp = kb[s, rows_at(qs), :].astype(BF16)
            vc = vb[s, rows_at(p + qs), :].astype(BF16)
            vp = vb[s, rows_at(qs), :].astype(BF16)
            qh = jnp.concatenate([jnp.where(low, q, 0.0), jnp.where(low, 0.0, q)], axis=0).astype(BF16)
            sc = jnp.where(mask_cur, _dot_nt(qh, kc), -jnp.inf)
            sp = jnp.where(mask_p, _dot_nt(qh, kp), -jnp.inf)
            m = jnp.max(jnp.maximum(sc, sp), axis=1, keepdims=True)
            pc = jnp.exp(sc - m)
            pp = jnp.exp(sp - m)
            den = jnp.sum(pc + pp, axis=1, keepdims=True)
            o = _dot(pc.astype(BF16), vc) + _dot(pp.astype(BF16), vp)
            rows = rows_at(qs)
            og[s, rows, :] = jnp.where(low, o[:ATTN_BLOCK], o[ATTN_BLOCK:])
            mg[s, rows, :] = jnp.where(low, m[:ATTN_BLOCK], m[ATTN_BLOCK:])
            lg[s, rows, :] = jnp.where(low, den[:ATTN_BLOCK], den[ATTN_BLOCK:])
        return carry

    lax.fori_loop(0, ncombo, combo, 0, unroll=2)
    kb[:, :p, :] = kb[:, ta:ta + p, :]
    vb[:, :p, :] = vb[:, ta:ta + p, :]

    rc = 256
    for s in range(nslab):
        for r0 in range(0, ta, rc):
            rs = slice(r0, r0 + rc)
            m_old = mrun[s, rs, :]
            m_blk = mg[s, rs, :]
            m_new = jnp.maximum(m_old, m_blk)
            a_old = jnp.exp(m_old - m_new)
            a_blk = jnp.exp(m_blk - m_new)
            acc[s, rs, :] = acc[s, rs, :] * a_old + og[s, rs, :] * a_blk
            drun[s, rs, :] = drun[s, rs, :] * a_old + lg[s, rs, :] * a_blk
            mrun[s, rs, :] = m_new


def _attn_kernel(q_ref, k_ref, v_ref, o_ref, *scratch, ta):
    ng = len(ATTN_DILATIONS)
    kbs = scratch[0:2 * ng:2]
    vbs = scratch[1:2 * ng:2]
    acc, mrun, drun, og, mg, lg = scratch[2 * ng:]
    i = pl.program_id(1)
    g = pl.program_id(2)

    @pl.when(g == 0)
    def _():
        acc[...] = jnp.zeros_like(acc)
        drun[...] = jnp.zeros_like(drun)
        mrun[...] = jnp.full(mrun.shape, NEG_BIG, F32)

    for gi, d in enumerate(ATTN_DILATIONS):
        @pl.when(g == gi)
        def _(gi=gi, d=d):
            _attn_group(q_ref, k_ref, v_ref, kbs[gi], vbs[gi], acc, mrun, drun, og, mg, lg, d=d, ta=ta, i=i)

    @pl.when(g == ng - 1)
    def _():
        for s in range(acc.shape[0]):
            o_ref[0, :, s * LANES:(s + 1) * LANES] = acc[s] / drun[s]


def _attention(q, k, v):
    b, nslab, l, _ = q.shape
    ng = len(ATTN_DILATIONS)
    gs = nslab // ng
    ta = ATTN_BLOCK * max(ATTN_DILATIONS)
    assert l % ta == 0
    blk = pl.BlockSpec((1, gs, ta, LANES), lambda bi, i, g: (bi, g, i, 0))
    scratch = []
    for d in ATTN_DILATIONS:
        scratch += [pltpu.VMEM((gs, ATTN_BLOCK * d + ta, LANES), F32)] * 2
    scratch += [pltpu.VMEM((gs, ta, LANES), F32)] * 6
    return pl.pallas_call(
        functools.partial(_attn_kernel, ta=ta), grid=(b, l // ta, ng),
        in_specs=[blk, blk, blk],
        out_specs=pl.BlockSpec((1, ta, gs * LANES), lambda bi, i, g: (bi, i, 0)),
        out_shape=jax.ShapeDtypeStruct((b, l, gs * LANES), F32),
        scratch_shapes=scratch,
        compiler_params=_cparams(("arbitrary", "arbitrary", "arbitrary")), name="dilated_attention",
    )(q, k, v)


def _merge_kernel(pm_ref, ys_ref, ao_ref, gate_ref, x_ref, wp_ref, wg_ref, ws_ref, wa_ref, wo_ref, g2_ref,
                  xo_ref, h2_ref, *, d_model, ssm_w):
    def weight(ref):
        return pltpu.bitcast(ref[...], BF16)

    y_pool = _dot(pm_ref[0], weight(wp_ref))
    glu = _dot(ys_ref[...], weight(wg_ref))
    sg = (glu[:, :ssm_w] * _sigmoid(glu[:, ssm_w:])).astype(BF16)
    y_ssm = _dot(sg, weight(ws_ref))
    y_attn = _dot(ao_ref[0].astype(BF16), weight(wa_ref))
    gate = pltpu.bitcast(gate_ref[0], BF16).astype(F32)
    merged = (gate[:, :d_model] * y_pool + gate[:, d_model:2 * d_model] * y_ssm
              + gate[:, 2 * d_model:] * y_attn)
    xn = x_ref[0] + _dot(merged.astype(BF16), weight(wo_ref))
    xo_ref[0] = xn
    h2_ref[...] = _rmsnorm(xn, g2_ref[...]).T.astype(BF16)


def _merge(pm, ys, ao, gates, x, w_pool, w_glu, w_ssm, w_attn, w_out, g2):
    b, l, d = x.shape
    tm = 256
    ssm_w = 2 * w_glu.shape[0]
    row = lambda bi, i: (bi, i, 0)
    const = lambda bi, i: (0, 0)
    full = lambda a: pl.BlockSpec(a.shape, const)
    return pl.pallas_call(
        functools.partial(_merge_kernel, d_model=d, ssm_w=ssm_w), grid=(b, l // tm),
        in_specs=[pl.BlockSpec((1, tm, pm.shape[-1]), row),
                  pl.BlockSpec((tm, ssm_w), lambda bi, i: (i, bi)),
                  pl.BlockSpec((1, tm, ao.shape[-1]), row),
                  pl.BlockSpec((1, tm // 2, 3 * d), row),
                  pl.BlockSpec((1, tm, d), row),
                  full(w_pool), full(w_glu), full(w_ssm), full(w_attn), full(w_out),
                  pl.BlockSpec((1, d), const)],
        out_specs=[pl.BlockSpec((1, tm, d), row),
                   pl.BlockSpec((d, tm), lambda bi, i: (0, bi * (l // tm) + i))],
        out_shape=[jax.ShapeDtypeStruct((b, l, d), F32), jax.ShapeDtypeStruct((d, b * l), BF16)],
        compiler_params=_cparams(("parallel", "parallel")), name="branch_merge",
    )(pm, ys, ao, gates, x, w_pool, w_glu, w_ssm, w_attn, w_out, g2.reshape(1, d))


def _candidate_pairs(k):
    return [(a, b) for a in range(k) for b in range(k) if (a + 1) * (b + 1) <= k]


_CODE_UNIT = 2.0 ** 121
_CODE_BASE = 80


def _removed_code(r):
    return -float(_CODE_BASE + r) * _CODE_UNIT


def _removed_round(w):
    return w * (-1.0 / _CODE_UNIT) - float(_CODE_BASE)


def _vreg_rows(x):
    return x.reshape(x.shape[0] // SUBLANES, SUBLANES, x.shape[1])


def _max_all_rows(w):
    p = jnp.max(w, axis=0)
    for sh in (4, 2, 1):
        p = jnp.maximum(p, pltpu.roll(p, sh, axis=0))
    return p


def _topk_round(work, rows):
    m = jnp.max(work, axis=0, keepdims=True)
    idx = jnp.min(jnp.where(work == m, rows, float(work.shape[0])), axis=0, keepdims=True)
    hit = rows == idx
    return m, hit, jnp.where(hit, -jnp.inf, work)


def _route_kernel(h_ref, wq_ref, k1_ref, k2_ref, r2_ref, e2_ref, n1_ref, e1_ref,
                  qt_ref, s1_ref, s2_ref, v1_ref, v2_ref, cand_ref, sel_ref, tied_ref,
                  *, nheads, nkeys, topk, tt):
    qt_ref[...] = _dot(wq_ref[...], h_ref[...])
    hc = k1_ref.shape[1]
    pairs = _candidate_pairs(topk)
    ncand = cand_ref.shape[1]

    def head(hd, carry):
        base = pl.multiple_of(hd * 2 * hc, 2 * hc)
        q1 = qt_ref[pl.ds(base, hc), :].astype(BF16)
        q2 = qt_ref[pl.ds(base + hc, hc), :].astype(BF16)
        s1_ref[...] = _dot(k1_ref[...], q1)
        s2_ref[...] = _dot(k2_ref[...], q2)

        def column(j, exact):
            if isinstance(j, int):
                cols = slice(j * LANES, (j + 1) * LANES)
            else:
                cols = pl.ds(pl.multiple_of(j * LANES, LANES), LANES)
            s1 = s1_ref[:, cols]
            s2 = s2_ref[:, cols]
            v1, v2, cand, selr = v1_ref.at[j], v2_ref.at[j], cand_ref.at[j], sel_ref.at[j]

            def build_candidates():
                for ci, (a, bq) in enumerate(pairs):
                    cand[ci:ci + 1, :] = v1[a:a + 1, :] + v2[bq:bq + 1, :]
                if ncand > len(pairs):
                    cand[len(pairs):, :] = jnp.full((ncand - len(pairs), LANES), -jnp.inf, F32)
                return cand[...]

            def finish(rank1, rank2, sel, cv):
                selr[...] = sel
                top = v1[0:1, :] + v2[0:1, :]
                z = jnp.sum(sel * jnp.exp(jnp.where(sel > 0.0, cv, top) - top), axis=0, keepdims=True)
                n1 = jnp.zeros(s1.shape, F32)
                off = 0
                for a in range(topk):
                    cnt = topk // (a + 1)
                    n_a = jnp.sum(selr[off:off + cnt, :], axis=0, keepdims=True)
                    n1 = jnp.where(rank1 == float(a), n_a, n1)
                    off += cnt
                n1_ref[hd, :, cols] = n1
                e1_ref[hd, :, cols] = jnp.exp(s1 - v1[0:1, :]) / z
                r2_ref[hd, j] = pltpu.bitcast(rank2.astype(BF16), jnp.uint32)
                e2_ref[hd, j] = pltpu.bitcast(jnp.exp(s2 - v2[0:1, :]).astype(BF16), jnp.uint32)

            if exact:
                key_rows = lax.broadcasted_iota(jnp.int32, s1.shape, 0).astype(F32)
                x1, x2 = s1, s2
                rk1 = jnp.full(s1.shape, float(nkeys), F32)
                rk2 = rk1
                for r in range(topk):
                    m1, hit1, x1 = _topk_round(x1, key_rows)
                    m2, hit2, x2 = _topk_round(x2, key_rows)
                    rk1 = jnp.where(hit1, float(r), rk1)
                    rk2 = jnp.where(hit2, float(r), rk2)
                    v1[r:r + 1, :] = m1
                    v2[r:r + 1, :] = m2
                cvs = build_candidates()
                cand_rows = lax.broadcasted_iota(jnp.int32, cvs.shape, 0).astype(F32)
                xc = cvs
                sl = jnp.zeros(cvs.shape, F32)
                for r in range(topk):
                    _, hit, xc = _topk_round(xc, cand_rows)
                    sl = jnp.where(hit, 1.0, sl)
                finish(rk1, rk2, sl, cvs)
                return None

            w1, w2 = _vreg_rows(s1), _vreg_rows(s2)
            for r in range(topk):
                m1 = _max_all_rows(w1)
                m2 = _max_all_rows(w2)
                w1 = jnp.where(w1 == m1[None], _removed_code(r), w1)
                w2 = jnp.where(w2 == m2[None], _removed_code(r), w2)
                v1[r:r + 1, :] = m1[0:1, :]
                v2[r:r + 1, :] = m2[0:1, :]
            w1, w2 = w1.reshape(s1.shape), w2.reshape(s2.shape)
            gone1 = w1 <= _removed_code(0)
            gone2 = w2 <= _removed_code(0)
            rank1 = jnp.where(gone1, _removed_round(w1), float(nkeys))
            rank2 = jnp.where(gone2, _removed_round(w2), float(nkeys))
            cv = build_candidates()
            wc = _vreg_rows(cv)
            for r in range(topk):
                wc = jnp.where(wc == _max_all_rows(wc)[None], _removed_code(r), wc)
            sel = jnp.where(wc.reshape(cv.shape) <= _removed_code(0), 1.0, 0.0)
            removed = (jnp.sum(jnp.where(gone1, 1.0, 0.0), axis=0, keepdims=True)
                       + jnp.sum(jnp.where(gone2, 1.0, 0.0), axis=0, keepdims=True)
                       + jnp.sum(sel, axis=0, keepdims=True))
            finish(rank1, rank2, sel, cv)
            expected = float(3 * topk + ncand - len(pairs))
            return jnp.max(jnp.where(removed != expected, 1.0, 0.0))

        for j in range(tt // LANES):
            tied_ref[j] = column(j, exact=False)

        def redo(j, carry2):
            @pl.when(tied_ref[j] > 0.0)
            def _():
                column(j, exact=True)
            return carry2

        lax.fori_loop(0, tt // LANES, redo, 0)
        return carry

    lax.fori_loop(0, nheads, head, 0)


def _peer_route(h2t, wq_t, k1, k2):
    d, t = h2t.shape
    nkeys, hc = k1.shape
    nheads = wq_t.shape[0] // (2 * hc)
    tt = 512
    npairs = len(_candidate_pairs(PEER_TOPK))
    ncand = -(-npairs // SUBLANES) * SUBLANES
    ncol = tt // LANES
    kern = functools.partial(_route_kernel, nheads=nheads, nkeys=nkeys, topk=PEER_TOPK, tt=tt)
    kblk = pl.BlockSpec((nheads, tt // LANES, nkeys // 2, LANES), lambda i: (0, i, 0, 0))
    ksh = jax.ShapeDtypeStruct((nheads, t // LANES, nkeys // 2, LANES), jnp.uint32)
    oblk = pl.BlockSpec((nheads, nkeys, tt), lambda i: (0, 0, i))
    osh = jax.ShapeDtypeStruct((nheads, nkeys, t), F32)
    return pl.pallas_call(
        kern, grid=(t // tt,),
        in_specs=[pl.BlockSpec((d, tt), lambda i: (0, i)),
                  pl.BlockSpec(wq_t.shape, lambda i: (0, 0)),
                  pl.BlockSpec(k1.shape, lambda i: (0, 0)),
                  pl.BlockSpec(k2.shape, lambda i: (0, 0))],
        out_specs=[kblk, kblk, oblk, oblk], out_shape=[ksh, ksh, osh, osh],
        scratch_shapes=[pltpu.VMEM((wq_t.shape[0], tt), F32),
                        pltpu.VMEM((nkeys, tt), F32), pltpu.VMEM((nkeys, tt), F32),
                        pltpu.VMEM((ncol, PEER_TOPK, LANES), F32), pltpu.VMEM((ncol, PEER_TOPK, LANES), F32),
                        pltpu.VMEM((ncol, ncand, LANES), F32), pltpu.VMEM((ncol, ncand, LANES), F32),
                        pltpu.SMEM((ncol,), F32)],
        compiler_params=_cparams(("parallel",)), name="peer_route",
    )(h2t, wq_t, k1, k2)


def _pack_kernel(x_ref, o_ref, *, transpose):
    x = x_ref[...]
    if transpose:
        x = x.T
    o_ref[...] = pltpu.bitcast(x.astype(BF16), jnp.uint32)


def _pack_table(tab, *, transpose):
    r, c = tab.shape
    tb = min(r, 512 if c <= 1024 else 128)
    if transpose:
        in_spec = pl.BlockSpec((tb, c), lambda i: (i, 0))
        out_spec = pl.BlockSpec((c // 2, tb), lambda i: (0, i))
        out_shape = jax.ShapeDtypeStruct((c // 2, r), jnp.uint32)
    else:
        in_spec = pl.BlockSpec((tb, c), lambda i: (i, 0))
        out_spec = pl.BlockSpec((tb // 2, c), lambda i: (i, 0))
        out_shape = jax.ShapeDtypeStruct((r // 2, c), jnp.uint32)
    return pl.pallas_call(
        functools.partial(_pack_kernel, transpose=transpose), grid=(r // tb,),
        in_specs=[in_spec], out_specs=out_spec, out_shape=out_shape,
        compiler_params=_cparams(("parallel",)), name="pack_table",
    )(tab)


def _ffn_kernel(h_ref, x_ref, g_ref, u_ref, un_ref, vt_ref, r2_ref, e2_ref, n1_ref, e1_ref, o_ref,
                acc_ref, at_ref, pt_ref, *, nheads, nkeys, cw, final_norm):
    j = pl.program_id(1)
    tt = h_ref.shape[1]
    et = 2 * u_ref.shape[0]
    nhalf = cw // LANES

    def first_pre_activations(table_ref):
        at_ref[0] = _dot(pltpu.bitcast(table_ref[0:at_ref.shape[1] // 2, :], BF16), h_ref[:, 0:cw])

    @pl.when(j == 0)
    def _():
        acc_ref[...] = jnp.zeros_like(acc_ref)
        first_pre_activations(u_ref)

    ktiles = nkeys // BF16_ROWS
    es = at_ref.shape[1]
    nsub = es // nkeys
    units = [(eh, c) for eh in range(et // es) for c in range(tt // cw)]
    n_at, n_pt = at_ref.shape[0], pt_ref.shape[0]

    def pre_activations(k):
        eh, c = units[k]
        u = pltpu.bitcast(u_ref[eh * es // 2:(eh + 1) * es // 2, :], BF16)
        at_ref[k % n_at] = _dot(u, h_ref[:, c * cw:(c + 1) * cw])

    assert n_at >= len(units)
    for k, (eh, c) in enumerate(units):
        if k + 1 < len(units):
            pre_activations(k + 1)
        else:
            first_pre_activations(un_ref)
        for jj in range(nsub):
            row = eh * nsub + jj
            for hf in range(nhalf):
                ck = c * nhalf + hf
                cols = slice(ck * LANES, (ck + 1) * LANES)
                lanes = slice(hf * LANES, (hf + 1) * LANES)
                gate = jnp.zeros((ktiles, BF16_ROWS, LANES), BF16)
                for hd in range(nheads):
                    n_row = jnp.broadcast_to(n1_ref[hd, row:row + 1, cols], (BF16_ROWS, LANES)).astype(BF16)
                    e_row = jnp.broadcast_to(e1_ref[hd, row:row + 1, cols], (BF16_ROWS, LANES)).astype(BF16)
                    r2 = pltpu.bitcast(r2_ref[hd, ck], BF16).reshape(ktiles, BF16_ROWS, LANES)
                    e2 = pltpu.bitcast(e2_ref[hd, ck], BF16).reshape(ktiles, BF16_ROWS, LANES)
                    gate = gate + jnp.where(r2 < n_row[None], e2, 0) * e_row[None]
                a = at_ref[k % n_at, jj * nkeys:(jj + 1) * nkeys, lanes].astype(BF16)
                p = (gate * _gelu(a.reshape(ktiles, BF16_ROWS, LANES))).reshape(nkeys, LANES)
                pt_ref[k % n_pt, jj * nkeys // 2:(jj + 1) * nkeys // 2, lanes] = pltpu.bitcast(p, jnp.uint32)
        p = pltpu.bitcast(pt_ref[k % n_pt], BF16)
        vt = pltpu.bitcast(vt_ref[:, eh * es:(eh + 1) * es], BF16)
        acc_ref[c] += _dot(vt, p)

    @pl.when(j == pl.num_programs(1) - 1)
    def _():
        for c in range(tt // cw):
            rows = slice(c * cw, (c + 1) * cw)
            y = x_ref[rows, :] + acc_ref[c].T
            o_ref[rows, :] = _rmsnorm(y, g_ref[...]) if final_norm else y


def _peer_experts(h2t, x, u_tab, v_tab_t, r2, e2, n1, e1, final_g):
    d, t = h2t.shape
    ne = 2 * u_tab.shape[0]
    nheads, nkeys, _ = n1.shape
    tt, et, cw = 1024, 1024, 256
    es = nkeys * SUBLANES
    at_ring, pt_ring = 4, 4
    final_norm = final_g is not None
    g = (final_g if final_norm else jnp.ones((d,), F32)).reshape(1, d)
    kern = functools.partial(_ffn_kernel, nheads=nheads, nkeys=nkeys, cw=cw, final_norm=final_norm)
    full_keys = pl.BlockSpec((nheads, tt // LANES, nkeys // 2, LANES), lambda i, j: (0, i, 0, 0))
    sub_keys = pl.BlockSpec((nheads, et // nkeys, tt), lambda i, j: (0, j, i))
    return pl.pallas_call(
        kern, grid=(t // tt, ne // et),
        in_specs=[pl.BlockSpec((d, tt), lambda i, j: (0, i)),
                  pl.BlockSpec((tt, d), lambda i, j: (i, 0), pipeline_mode=pl.Buffered(1)),
                  pl.BlockSpec((1, d), lambda i, j: (0, 0)),
                  pl.BlockSpec((et // 2, d), lambda i, j: (j, 0)),
                  pl.BlockSpec((es // 2, d), lambda i, j: (jnp.minimum(j + 1, ne // et - 1) * (et // es), 0)),
                  pl.BlockSpec((d // 2, et), lambda i, j: (0, j)),
                  full_keys, full_keys, sub_keys, sub_keys],
        out_specs=pl.BlockSpec((tt, d), lambda i, j: (i, 0)),
        out_shape=jax.ShapeDtypeStruct((t, d), F32),
        scratch_shapes=[pltpu.VMEM((tt // cw, d, cw), F32),
                        pltpu.VMEM((at_ring, es, cw), F32),
                        pltpu.VMEM((pt_ring, es // 2, cw), jnp.uint32)],
        compiler_params=_cparams(("parallel", "arbitrary")), name="peer_experts",
    )(h2t, x, g, u_tab, u_tab, v_tab_t, r2, e2, n1, e1)


def kernel(x, positions, norm1_g, w_in, pool_w, pool_scale, pool_proj, ssm_a_re, ssm_a_im, ssm_log_dt,
           ssm_b_re, ssm_b_im, ssm_c_re, ssm_c_im, ssm_d, ssm_glu, ssm_proj, attn_proj, w_out, norm2_g,
           peer_wq, peer_k1, peer_k2, peer_u, peer_v, final_g):
    b, l, d = x.shape
    depth = w_in.shape[0]
    pool_w_cols = pool_proj.shape[1]
    ssm_w_cols = ssm_proj.shape[1]
    attn_w = len(ATTN_DILATIONS) * HEADS_PER_GROUP * HEAD_DIM
    cos, sin = _rope_tables(positions)
    for layer in range(depth):
        u_pool, u_ssm, q, k, v, gates = _in_projection(
            x, norm1_g[layer], _pack_table(w_in[layer], transpose=False), cos, sin,
            pool_w=pool_w_cols, ssm_w=ssm_w_cols, attn_w=attn_w)
        pm = _pool_mixer(u_pool, pool_w[layer].astype(BF16), pool_scale[layer])
        abar_re, abar_im, bbar_re_t, bbar_im_t = _ssm_params(
            ssm_a_re[layer], ssm_a_im[layer], ssm_log_dt[layer], ssm_b_re[layer], ssm_b_im[layer])
        ys = _ssm_mixer(u_ssm, b, abar_re, abar_im, bbar_re_t, bbar_im_t,
                        ssm_c_re[layer], ssm_c_im[layer], ssm_d[layer])
        ao = _attention(q, k, v)
        packed = [_pack_table(wt[layer], transpose=False)
                  for wt in (pool_proj, ssm_glu, ssm_proj, attn_proj, w_out)]
        x, h2t = _merge(pm, ys, ao, gates, x, *packed, norm2_g[layer])
        r2, e2, n1, e1 = _peer_route(h2t, peer_wq[layer].T.astype(BF16), peer_k1[layer].astype(BF16),
                                     peer_k2[layer].astype(BF16))
        x = _peer_experts(h2t, x.reshape(b * l, d), _pack_table(peer_u[layer], transpose=False),
                          _pack_table(peer_v[layer], transpose=True), r2, e2, n1, e1,
                          final_g if layer == depth - 1 else None).reshape(b, l, d)
    return x
```

```python
import functools
import math

import jax
import jax.numpy as jnp
from jax import lax
from jax.experimental import pallas as pl
from jax.experimental.pallas import tpu as pltpu

F32 = jnp.float32
BF16 = jnp.bfloat16

EPS = 1e-6
POOL_WINDOWS = (2, 4, 8, 16)
POOL_GROUP = 128
SSM_GROUP = 16
SSM_GROUPS = 32
SSM_STATE = 64
HEAD_DIM = 64
HEADS_PER_GROUP = 4
ATTN_DILATIONS = (1, 4, 16)
ATTN_BLOCK = 128
ROPE_THETA = 10000.0
PEER_HEADS = 8
PEER_TOPK = 16

LANES = 128
SUBLANES = 8
BF16_ROWS = 16
VMEM_LIMIT = 56 * 1024 * 1024
NEG_BIG = -1e30


def _cparams(sem):
    return pltpu.CompilerParams(dimension_semantics=sem, vmem_limit_bytes=VMEM_LIMIT)


def _gelu(x):
    c = math.sqrt(2.0 / math.pi)
    return x * (0.5 * (1.0 + jnp.tanh(c * (x + 0.044715 * (x * x * x)))))


def _sigmoid(x):
    return 1.0 / (1.0 + jnp.exp(-x))


def _rmsnorm(x, g):
    ms = jnp.mean(x * x, axis=-1, keepdims=True)
    return x * lax.rsqrt(ms + EPS) * g


def _dot(a, b):
    return jnp.dot(a, b, preferred_element_type=F32)


def _dot_nt(a, b):
    return lax.dot_general(a, b, (((1,), (1,)), ((), ())), preferred_element_type=F32)


def _rope_kernel(pos_ref, invf_ref, sign_ref, cos_ref, sin_ref):
    ang = pos_ref[...].astype(F32) * invf_ref[...]
    cos_ref[...] = jnp.cos(ang)
    sin_ref[...] = jnp.sin(ang) * sign_ref[...]


def _rope_tables(positions):
    b, l = positions.shape
    t = b * l
    half = HEAD_DIM // 2
    inv_freq = ROPE_THETA ** (-jnp.arange(half, dtype=F32) / half)
    invf = jnp.tile(inv_freq, LANES // half).reshape(1, LANES)
    sign = jnp.tile(jnp.concatenate([-jnp.ones((half,), F32), jnp.ones((half,), F32)]),
                    LANES // HEAD_DIM).reshape(1, LANES)
    pos_b = jnp.broadcast_to(positions.reshape(t, 1), (t, LANES))
    tr = 2048
    cos, sin = pl.pallas_call(
        _rope_kernel,
        grid=(t // tr,),
        in_specs=[pl.BlockSpec((tr, LANES), lambda i: (i, 0)),
                  pl.BlockSpec((1, LANES), lambda i: (0, 0)),
                  pl.BlockSpec((1, LANES), lambda i: (0, 0))],
        out_specs=[pl.BlockSpec((tr, LANES), lambda i: (i, 0))] * 2,
        out_shape=[jax.ShapeDtypeStruct((t, LANES), F32)] * 2,
        compiler_params=_cparams(("parallel",)),
        name="rope_tables",
    )(pos_b, invf, sign)
    return cos.reshape(b, l, LANES), sin.reshape(b, l, LANES)


def _inproj_kernel(x_ref, g_ref, w_ref, cos_ref, sin_ref, pool_ref, ssm_ref, q_ref, k_ref, v_ref, gate_ref,
                   *, pool_w, ssm_w, attn_w, d_model):
    h = _rmsnorm(x_ref[0], g_ref[...]).astype(BF16)
    c0 = 0

    def w_cols(lo, hi):
        return pltpu.bitcast(w_ref[:, lo:hi], BF16)

    z = _dot(h, w_cols(c0, c0 + pool_w + ssm_w))
    pool_ref[0] = z[:, :pool_w]
    ssm_ref[...] = z[:, pool_w:]
    c0 += pool_w + ssm_w
    cos = cos_ref[0]
    sin = sin_ref[0]
    nslab = attn_w // LANES
    half = HEAD_DIM // 2
    first_half = lax.broadcasted_iota(jnp.int32, cos.shape, 1) % HEAD_DIM < half
    for dst in (q_ref, k_ref):
        z = _dot(h, w_cols(c0, c0 + attn_w))
        for s in range(nslab):
            a = z[:, s * LANES:(s + 1) * LANES]
            a_sw = jnp.where(first_half, pltpu.roll(a, LANES - half, axis=1), pltpu.roll(a, half, axis=1))
            dst[0, s] = a * cos + a_sw * sin
        c0 += attn_w
    z = _dot(h, w_cols(c0, c0 + attn_w))
    for s in range(nslab):
        v_ref[0, s] = z[:, s * LANES:(s + 1) * LANES]
    c0 += attn_w
    z = _dot(h, w_cols(c0, c0 + 3 * d_model))
    gate_ref[0] = pltpu.bitcast(_sigmoid(z).astype(BF16), jnp.uint32)


def _in_projection(x, g, w_packed, cos, sin, *, pool_w, ssm_w, attn_w):
    b, l, d = x.shape
    tm = 256
    nslab = attn_w // LANES
    kern = functools.partial(_inproj_kernel, pool_w=pool_w, ssm_w=ssm_w, attn_w=attn_w, d_model=d)
    row = lambda bi, i: (bi, i, 0)
    slab = lambda bi, i: (bi, 0, i, 0)
    const2 = lambda bi, i: (0, 0)
    in_specs = [pl.BlockSpec((1, tm, d), row),
                pl.BlockSpec((1, d), const2),
                pl.BlockSpec(w_packed.shape, const2, pipeline_mode=pl.Buffered(1)),
                pl.BlockSpec((1, tm, LANES), row),
                pl.BlockSpec((1, tm, LANES), row)]
    args = [x, g.reshape(1, d), w_packed, cos, sin]
    out_specs = [pl.BlockSpec((1, tm, pool_w), row),
                 pl.BlockSpec((tm, ssm_w), lambda bi, i: (i, bi)),
                 pl.BlockSpec((1, nslab, tm, LANES), slab),
                 pl.BlockSpec((1, nslab, tm, LANES), slab),
                 pl.BlockSpec((1, nslab, tm, LANES), slab),
                 pl.BlockSpec((1, tm // 2, 3 * d), row)]
    out_shape = [jax.ShapeDtypeStruct((b, l, pool_w), F32),
                 jax.ShapeDtypeStruct((l, b * ssm_w), F32),
                 jax.ShapeDtypeStruct((b, nslab, l, LANES), F32),
                 jax.ShapeDtypeStruct((b, nslab, l, LANES), F32),
                 jax.ShapeDtypeStruct((b, nslab, l, LANES), F32),
                 jax.ShapeDtypeStruct((b, l // 2, 3 * d), jnp.uint32)]
    return pl.pallas_call(
        kern, grid=(b, l // tm), in_specs=in_specs, out_specs=out_specs, out_shape=out_shape,
        compiler_params=_cparams(("parallel", "parallel")), name="in_projection",
    )(*args)


def _pool_kernel(u_ref, halo_ref, w_ref, scale_ref, o_ref, *, tm, halo):
    i = pl.program_id(1)
    cur = u_ref[0]
    prev = jnp.where(i > 0, halo_ref[0], 0.0)
    ext = jnp.concatenate([prev, cur], axis=0)
    t_idx = i * tm + lax.broadcasted_iota(jnp.int32, (tm, POOL_GROUP), 0)
    for gi, win in enumerate(POOL_WINDOWS):
        cs = slice(gi * POOL_GROUP, (gi + 1) * POOL_GROUP)
        s = ext[:, cs]
        sh = 1
        while sh < win:
            s = s + pltpu.roll(s, sh, axis=0)
            sh *= 2
        count = jnp.minimum(t_idx + 1, win).astype(F32)
        pooled = s[halo:, :] / count - cur[:, cs]
        mixed = _dot(pooled.astype(BF16), w_ref[gi]) * scale_ref[:, cs]
        o_ref[0, :, cs] = mixed.astype(BF16)


def _pool_mixer(u, w, scale):
    b, l, c = u.shape
    tm, halo = 512, 16
    assert halo >= max(POOL_WINDOWS) and halo % SUBLANES == 0
    kern = functools.partial(_pool_kernel, tm=tm, halo=halo)
    return pl.pallas_call(
        kern, grid=(b, l // tm),
        in_specs=[pl.BlockSpec((1, tm, c), lambda bi, i: (bi, i, 0)),
                  pl.BlockSpec((1, halo, c), lambda bi, i: (bi, jnp.maximum(i * (tm // halo) - 1, 0), 0)),
                  pl.BlockSpec(w.shape, lambda bi, i: (0, 0, 0)),
                  pl.BlockSpec((1, c), lambda bi, i: (0, 0))],
        out_specs=pl.BlockSpec((1, tm, c), lambda bi, i: (bi, i, 0)),
        out_shape=jax.ShapeDtypeStruct((b, l, c), BF16),
        compiler_params=_cparams(("parallel", "parallel")), name="pool_mixer",
    )(u, u, w, scale.reshape(1, c))


def _ssm_param_kernel(are_ref, aim_ref, ldt_ref, bre_ref, bim_ref, oar_ref, oai_ref, obr_ref, obi_ref):
    ar = are_ref[...]
    ai = aim_ref[...]
    dt = jnp.exp(ldt_ref[...])
    decay = jnp.exp(ar * dt)
    abar_re = decay * jnp.cos(ai * dt)
    abar_im = decay * jnp.sin(ai * dt)
    den = ar * ar + ai * ai
    nr = abar_re - 1.0
    k_re = (nr * ar + abar_im * ai) / den
    k_im = (abar_im * ar - nr * ai) / den
    oar_ref[...] = abar_re
    oai_ref[...] = abar_im
    for gi in range(are_ref.shape[0]):
        kr = k_re[gi:gi + 1, :]
        ki = k_im[gi:gi + 1, :]
        br = bre_ref[gi]
        bi = bim_ref[gi]
        obr_ref[gi] = kr * br - ki * bi
        obi_ref[gi] = kr * bi + ki * br


def _ssm_params(a_re, a_im, log_dt, b_re, b_im):
    g, n, p = b_re.shape
    brt = jnp.transpose(b_re, (0, 2, 1))
    bit = jnp.transpose(b_im, (0, 2, 1))
    return pl.pallas_call(
        _ssm_param_kernel,
        out_shape=[jax.ShapeDtypeStruct((g, n), F32)] * 2 + [jax.ShapeDtypeStruct((g, p, n), F32)] * 2,
        name="ssm_params",
    )(a_re, a_im, log_dt.reshape(g, 1), brt, bit)


def _block_diag(m):
    k, r, c = m.shape
    eye = jnp.eye(k, dtype=m.dtype)
    return (m[:, :, None, :] * eye[:, None, :, None]).reshape(k * r, k * c)


def _ssm_kernel(u_ref, bre_ref, bim_ref, are_ref, aim_ref, cre_ref, cim_ref, d_ref, y_ref,
                st_ref, xre_ref, xim_ref, sre_ref, sim_ref, us_ref, yo_ref, *, tc, hw):
    rows = tc * SUBLANES
    nslab = hw // LANES

    @pl.when(pl.program_id(0) == 0)
    def _():
        st_ref[...] = jnp.zeros_like(st_ref)

    for s in range(SUBLANES):
        for hf in range(nslab):
            lanes = slice(s * hw + hf * LANES, s * hw + (hf + 1) * LANES)
            us_ref[hf, pl.ds(s, tc, stride=SUBLANES), :] = u_ref[:, lanes]
    u = jnp.concatenate([us_ref[hf] for hf in range(nslab)], axis=1)
    odd = (lax.broadcasted_iota(jnp.int32, (rows, hw), 0) % 2) == 1
    ub = u.astype(BF16)
    zero = jnp.zeros_like(ub)
    u_exp = jnp.concatenate([jnp.where(odd, zero, ub), jnp.where(odd, ub, zero)], axis=1)
    xre_ref[...] = _dot(u_exp, bre_ref[...])
    xim_ref[...] = _dot(u_exp, bim_ref[...])
    a_re = are_ref[...]
    a_im = aim_ref[...]

    def step(t, carry):
        s_re, s_im = carry
        r0 = pl.multiple_of(t * SUBLANES, SUBLANES)
        n_re = a_re * s_re - a_im * s_im + xre_ref[pl.ds(r0, SUBLANES), :]
        n_im = a_re * s_im + a_im * s_re + xim_ref[pl.ds(r0, SUBLANES), :]
        sre_ref[pl.ds(r0, SUBLANES), :] = n_re
        sim_ref[pl.ds(r0, SUBLANES), :] = n_im
        return n_re, n_im

    s_re, s_im = lax.fori_loop(0, tc, step, (st_ref[0], st_ref[1]), unroll=8)
    st_ref[0] = s_re
    st_ref[1] = s_im
    yf = _dot(sre_ref[...].astype(BF16), cre_ref[...]) - _dot(sim_ref[...].astype(BF16), cim_ref[...])
    y = _gelu(jnp.where(odd, yf[:, hw:], yf[:, :hw]) + pltpu.repeat(d_ref[...], tc, axis=0) * u)
    for hf in range(nslab):
        yo_ref[hf] = y[:, hf * LANES:(hf + 1) * LANES]
    for s in range(SUBLANES):
        for hf in range(nslab):
            lanes = slice(s * hw + hf * LANES, s * hw + (hf + 1) * LANES)
            y_ref[:, lanes] = yo_ref[hf, pl.ds(s, tc, stride=SUBLANES), :].astype(BF16)


def _ssm_mixer(u_scan, batch, abar_re, abar_im, bbar_re_t, bbar_im_t, c_re, c_im, d_skip):
    l = u_scan.shape[0]
    g, p, n = bbar_re_t.shape
    hg = g // 2
    hw = hg * p
    hs = hg * n
    assert batch * 2 == SUBLANES

    def stack_b(bt):
        return jnp.concatenate([_block_diag(bt[:hg]), _block_diag(bt[hg:])], axis=0).astype(BF16)

    def cat_c(c):
        ct = jnp.transpose(c, (0, 2, 1))
        return jnp.concatenate([_block_diag(ct[:hg]), _block_diag(ct[hg:])], axis=1).astype(BF16)

    def tile_a(a):
        return jnp.tile(a.reshape(2, hs), (batch, 1))

    d8 = jnp.tile(d_skip.reshape(2, hw), (batch, 1))
    tc = 64
    rows = tc * SUBLANES
    kern = functools.partial(_ssm_kernel, tc=tc, hw=hw)
    const = lambda i: (0, 0)
    return pl.pallas_call(
        kern, grid=(l // tc,),
        in_specs=[pl.BlockSpec((tc, SUBLANES * hw), lambda i: (i, 0)),
                  pl.BlockSpec((2 * hw, hs), const), pl.BlockSpec((2 * hw, hs), const),
                  pl.BlockSpec((SUBLANES, hs), const), pl.BlockSpec((SUBLANES, hs), const),
                  pl.BlockSpec((hs, 2 * hw), const), pl.BlockSpec((hs, 2 * hw), const),
                  pl.BlockSpec((SUBLANES, hw), const)],
        out_specs=pl.BlockSpec((tc, SUBLANES * hw), lambda i: (i, 0)),
        out_shape=jax.ShapeDtypeStruct((l, SUBLANES * hw), BF16),
        scratch_shapes=[pltpu.VMEM((2, SUBLANES, hs), F32),
                        pltpu.VMEM((rows, hs), F32), pltpu.VMEM((rows, hs), F32),
                        pltpu.VMEM((rows, hs), F32), pltpu.VMEM((rows, hs), F32),
                        pltpu.VMEM((hw // LANES, rows, LANES), F32),
                        pltpu.VMEM((hw // LANES, rows, LANES), F32)],
        compiler_params=_cparams(("arbitrary",)), name="ssm_mixer",
    )(u_scan, stack_b(bbar_re_t), stack_b(bbar_im_t), tile_a(abar_re), tile_a(abar_im),
      cat_c(c_re), cat_c(c_im), d8)


def _attn_group(q_ref, k_ref, v_ref, kb, vb, acc, mrun, drun, og, mg, lg, *, d, ta, i):
    p = ATTN_BLOCK * d
    ncombo = ta // ATTN_BLOCK
    nslab = q_ref.shape[1]
    scale = HEAD_DIM ** -0.5
    kb[:, p:p + ta, :] = k_ref[0]
    vb[:, p:p + ta, :] = v_ref[0]

    @pl.when(i == 0)
    def _():
        kb[:, :p, :] = jnp.zeros((nslab, p, LANES), F32)
        vb[:, :p, :] = jnp.zeros((nslab, p, LANES), F32)

    nh = LANES // HEAD_DIM
    qi = lax.broadcasted_iota(jnp.int32, (nh * ATTN_BLOCK, ATTN_BLOCK), 0) % ATTN_BLOCK
    kk = lax.broadcasted_iota(jnp.int32, (nh * ATTN_BLOCK, ATTN_BLOCK), 1)
    mask_cur = kk <= qi
    mask_prev = kk >= qi
    low = lax.broadcasted_iota(jnp.int32, (ATTN_BLOCK, ATTN_BLOCK), 1) < HEAD_DIM

    def rows_at(start):
        return pl.ds(start, ATTN_BLOCK) if d == 1 else pl.ds(start, ATTN_BLOCK, stride=d)

    def combo(c, carry):
        r = c % d
        n = c // d
        qs = n * p + r
        has_prev = jnp.logical_or(i > 0, n > 0)
        mask_p = jnp.logical_and(mask_prev, has_prev)
        for s in range(nslab):
            q = q_ref[0, s, rows_at(qs), :] * scale
            kc = kb[s, rows_at(p + qs), :].astype(BF16)
            kp = kb[s, rows_at(qs), :].astype(BF16)
            vc = vb[s, rows_at(p + qs), :].astype(BF16)
            vp = vb[s, rows_at(qs), :].astype(BF16)
            qh = jnp.concatenate([jnp.where(low, q, 0.0), jnp.where(low, 0.0, q)], axis=0).astype(BF16)
            sc = jnp.where(mask_cur, _dot_nt(qh, kc), -jnp.inf)
            sp = jnp.where(mask_p, _dot_nt(qh, kp), -jnp.inf)
            m = jnp.max(jnp.maximum(sc, sp), axis=1, keepdims=True)
            pc = jnp.exp(sc - m)
            pp = jnp.exp(sp - m)
            den = jnp.sum(pc + pp, axis=1, keepdims=True)
            o = _dot(pc.astype(BF16), vc) + _dot(pp.astype(BF16), vp)
            rows = rows_at(qs)
            og[s, rows, :] = jnp.where(low, o[:ATTN_BLOCK], o[ATTN_BLOCK:])
            mg[s, rows, :] = jnp.where(low, m[:ATTN_BLOCK], m[ATTN_BLOCK:])
            lg[s, rows, :] = jnp.where(low, den[:ATTN_BLOCK], den[ATTN_BLOCK:])
        return carry

    lax.fori_loop(0, ncombo, combo, 0, unroll=2)
    kb[:, :p, :] = kb[:, ta:ta + p, :]
    vb[:, :p, :] = vb[:, ta:ta + p, :]

    rc = 256
    for s in range(nslab):
        for r0 in range(0, ta, rc):
            rs = slice(r0, r0 + rc)
            m_old = mrun[s, rs, :]
            m_blk = mg[s, rs, :]
            m_new = jnp.maximum(m_old, m_blk)
            a_old = jnp.exp(m_old - m_new)
            a_blk = jnp.exp(m_blk - m_new)
            acc[s, rs, :] = acc[s, rs, :] * a_old + og[s, rs, :] * a_blk
            drun[s, rs, :] = drun[s, rs, :] * a_old + lg[s, rs, :] * a_blk
            mrun[s, rs, :] = m_new


def _attn_kernel(q_ref, k_ref, v_ref, o_ref, *scratch, ta):
    ng = len(ATTN_DILATIONS)
    kbs = scratch[0:2 * ng:2]
    vbs = scratch[1:2 * ng:2]
    acc, mrun, drun, og, mg, lg = scratch[2 * ng:]
    i = pl.program_id(1)
    g = pl.program_id(2)

    @pl.when(g == 0)
    def _():
        acc[...] = jnp.zeros_like(acc)
        drun[...] = jnp.zeros_like(drun)
        mrun[...] = jnp.full(mrun.shape, NEG_BIG, F32)

    for gi, d in enumerate(ATTN_DILATIONS):
        @pl.when(g == gi)
        def _(gi=gi, d=d):
            _attn_group(q_ref, k_ref, v_ref, kbs[gi], vbs[gi], acc, mrun, drun, og, mg, lg, d=d, ta=ta, i=i)

    @pl.when(g == ng - 1)
    def _():
        for s in range(acc.shape[0]):
            o_ref[0, :, s * LANES:(s + 1) * LANES] = acc[s] / drun[s]


def _attention(q, k, v):
    b, nslab, l, _ = q.shape
    ng = len(ATTN_DILATIONS)
    gs = nslab // ng
    ta = ATTN_BLOCK * max(ATTN_DILATIONS)
    assert l % ta == 0
    blk = pl.BlockSpec((1, gs, ta, LANES), lambda bi, i, g: (bi, g, i, 0))
    scratch = []
    for d in ATTN_DILATIONS:
        scratch += [pltpu.VMEM((gs, ATTN_BLOCK * d + ta, LANES), F32)] * 2
    scratch += [pltpu.VMEM((gs, ta, LANES), F32)] * 6
    return pl.pallas_call(
        functools.partial(_attn_kernel, ta=ta), grid=(b, l // ta, ng),
        in_specs=[blk, blk, blk],
        out_specs=pl.BlockSpec((1, ta, gs * LANES), lambda bi, i, g: (bi, i, 0)),
        out_shape=jax.ShapeDtypeStruct((b, l, gs * LANES), F32),
        scratch_shapes=scratch,
        compiler_params=_cparams(("arbitrary", "arbitrary", "arbitrary")), name="dilated_attention",
    )(q, k, v)


def _merge_kernel(pm_ref, ys_ref, ao_ref, gate_ref, x_ref, wp_ref, wg_ref, ws_ref, wa_ref, wo_ref, g2_ref,
                  xo_ref, h2_ref, *, d_model, ssm_w):
    def weight(ref):
        return pltpu.bitcast(ref[...], BF16)

    y_pool = _dot(pm_ref[0], weight(wp_ref))
    glu = _dot(ys_ref[...], weight(wg_ref))
    sg = (glu[:, :ssm_w] * _sigmoid(glu[:, ssm_w:])).astype(BF16)
    y_ssm = _dot(sg, weight(ws_ref))
    y_attn = _dot(ao_ref[0].astype(BF16), weight(wa_ref))
    gate = pltpu.bitcast(gate_ref[0], BF16).astype(F32)
    merged = (gate[:, :d_model] * y_pool + gate[:, d_model:2 * d_model] * y_ssm
              + gate[:, 2 * d_model:] * y_attn)
    xn = x_ref[0] + _dot(merged.astype(BF16), weight(wo_ref))
    xo_ref[0] = xn
    h2_ref[...] = _rmsnorm(xn, g2_ref[...]).T.astype(BF16)


def _merge(pm, ys, ao, gates, x, w_pool, w_glu, w_ssm, w_attn, w_out, g2):
    b, l, d = x.shape
    tm = 256
    ssm_w = 2 * w_glu.shape[0]
    row = lambda bi, i: (bi, i, 0)
    const = lambda bi, i: (0, 0)
    full = lambda a: pl.BlockSpec(a.shape, const)
    return pl.pallas_call(
        functools.partial(_merge_kernel, d_model=d, ssm_w=ssm_w), grid=(b, l // tm),
        in_specs=[pl.BlockSpec((1, tm, pm.shape[-1]), row),
                  pl.BlockSpec((tm, ssm_w), lambda bi, i: (i, bi)),
                  pl.BlockSpec((1, tm, ao.shape[-1]), row),
                  pl.BlockSpec((1, tm // 2, 3 * d), row),
                  pl.BlockSpec((1, tm, d), row),
                  full(w_pool), full(w_glu), full(w_ssm), full(w_attn), full(w_out),
                  pl.BlockSpec((1, d), const)],
        out_specs=[pl.BlockSpec((1, tm, d), row),
                   pl.BlockSpec((d, tm), lambda bi, i: (0, bi * (l // tm) + i))],
        out_shape=[jax.ShapeDtypeStruct((b, l, d), F32), jax.ShapeDtypeStruct((d, b * l), BF16)],
        compiler_params=_cparams(("parallel", "parallel")), name="branch_merge",
    )(pm, ys, ao, gates, x, w_pool, w_glu, w_ssm, w_attn, w_out, g2.reshape(1, d))


def _candidate_pairs(k):
    return [(a, b) for a in range(k) for b in range(k) if (a + 1) * (b + 1) <= k]


_CODE_UNIT = 2.0 ** 121
_CODE_BASE = 80


def _removed_code(r):
    return -float(_CODE_BASE + r) * _CODE_UNIT


def _removed_round(w):
    return w * (-1.0 / _CODE_UNIT) - float(_CODE_BASE)


def _vreg_rows(x):
    return x.reshape(x.shape[0] // SUBLANES, SUBLANES, x.shape[1])


def _max_all_rows(w):
    p = jnp.max(w, axis=0)
    for sh in (4, 2, 1):
        p = jnp.maximum(p, pltpu.roll(p, sh, axis=0))
    return p


def _topk_round(work, rows):
    m = jnp.max(work, axis=0, keepdims=True)
    idx = jnp.min(jnp.where(work == m, rows, float(work.shape[0])), axis=0, keepdims=True)
    hit = rows == idx
    return m, hit, jnp.where(hit, -jnp.inf, work)


def _route_kernel(h_ref, wq_ref, k1_ref, k2_ref, r2_ref, e2_ref, n1_ref, e1_ref,
                  qt_ref, s1_ref, s2_ref, v1_ref, v2_ref, cand_ref, sel_ref, tied_ref,
                  *, nheads, nkeys, topk, tt):
    qt_ref[...] = _dot(pltpu.bitcast(wq_ref[...], BF16), h_ref[...])
    hc = k1_ref.shape[1]
    pairs = _candidate_pairs(topk)
    ncand = cand_ref.shape[1]

    def head(hd, carry):
        base = pl.multiple_of(hd * 2 * hc, 2 * hc)
        q1 = qt_ref[pl.ds(base, hc), :].astype(BF16)
        q2 = qt_ref[pl.ds(base + hc, hc), :].astype(BF16)
        s1_ref[...] = _dot(k1_ref[...], q1)
        s2_ref[...] = _dot(k2_ref[...], q2)

        def column(j, exact):
            if isinstance(j, int):
                cols = slice(j * LANES, (j + 1) * LANES)
            else:
                cols = pl.ds(pl.multiple_of(j * LANES, LANES), LANES)
            s1 = s1_ref[:, cols]
            s2 = s2_ref[:, cols]
            v1, v2, cand, selr = v1_ref.at[j], v2_ref.at[j], cand_ref.at[j], sel_ref.at[j]

            def build_candidates():
                for ci, (a, bq) in enumerate(pairs):
                    cand[ci:ci + 1, :] = v1[a:a + 1, :] + v2[bq:bq + 1, :]
                if ncand > len(pairs):
                    cand[len(pairs):, :] = jnp.full((ncand - len(pairs), LANES), -jnp.inf, F32)
                return cand[...]

            def finish(rank1, rank2, sel, cv):
                selr[...] = sel
                top = v1[0:1, :] + v2[0:1, :]
                z = jnp.sum(sel * jnp.exp(jnp.where(sel > 0.0, cv, top) - top), axis=0, keepdims=True)
                n1 = jnp.zeros(s1.shape, F32)
                off = 0
                for a in range(topk):
                    cnt = topk // (a + 1)
                    n_a = jnp.sum(selr[off:off + cnt, :], axis=0, keepdims=True)
                    n1 = jnp.where(rank1 == float(a), n_a, n1)
                    off += cnt
                n1_ref[hd, :, cols] = n1
                e1_ref[hd, :, cols] = jnp.exp(s1 - v1[0:1, :]) / z
                r2_ref[hd, j] = pltpu.bitcast(rank2.astype(BF16), jnp.uint32)
                e2_ref[hd, j] = pltpu.bitcast(jnp.exp(s2 - v2[0:1, :]).astype(BF16), jnp.uint32)

            if exact:
                key_rows = lax.broadcasted_iota(jnp.int32, s1.shape, 0).astype(F32)
                x1, x2 = s1, s2
                rk1 = jnp.full(s1.shape, float(nkeys), F32)
                rk2 = rk1
                for r in range(topk):
                    m1, hit1, x1 = _topk_round(x1, key_rows)
                    m2, hit2, x2 = _topk_round(x2, key_rows)
                    rk1 = jnp.where(hit1, float(r), rk1)
                    rk2 = jnp.where(hit2, float(r), rk2)
                    v1[r:r + 1, :] = m1
                    v2[r:r + 1, :] = m2
                cvs = build_candidates()
                cand_rows = lax.broadcasted_iota(jnp.int32, cvs.shape, 0).astype(F32)
                xc = cvs
                sl = jnp.zeros(cvs.shape, F32)
                for r in range(topk):
                    _, hit, xc = _topk_round(xc, cand_rows)
                    sl = jnp.where(hit, 1.0, sl)
                finish(rk1, rk2, sl, cvs)
                return None

            w1, w2 = _vreg_rows(s1), _vreg_rows(s2)
            for r in range(topk):
                m1 = _max_all_rows(w1)
                m2 = _max_all_rows(w2)
                w1 = jnp.where(w1 == m1[None], _removed_code(r), w1)
                w2 = jnp.where(w2 == m2[None], _removed_code(r), w2)
                v1[r:r + 1, :] = m1[0:1, :]
                v2[r:r + 1, :] = m2[0:1, :]
            w1, w2 = w1.reshape(s1.shape), w2.reshape(s2.shape)
            gone1 = w1 <= _removed_code(0)
            gone2 = w2 <= _removed_code(0)
            rank1 = jnp.where(gone1, _removed_round(w1), float(nkeys))
            rank2 = jnp.where(gone2, _removed_round(w2), float(nkeys))
            cv = build_candidates()
            wc = _vreg_rows(cv)
            for r in range(topk):
                wc = jnp.where(wc == _max_all_rows(wc)[None], _removed_code(r), wc)
            sel = jnp.where(wc.reshape(cv.shape) <= _removed_code(0), 1.0, 0.0)
            removed = (jnp.sum(jnp.where(gone1, 1.0, 0.0), axis=0, keepdims=True)
                       + jnp.sum(jnp.where(gone2, 1.0, 0.0), axis=0, keepdims=True)
                       + jnp.sum(sel, axis=0, keepdims=True))
            finish(rank1, rank2, sel, cv)
            expected = float(3 * topk + ncand - len(pairs))
            return jnp.max(jnp.where(removed != expected, 1.0, 0.0))

        for j in range(tt // LANES):
            tied_ref[j] = column(j, exact=False)

        def redo(j, carry2):
            @pl.when(tied_ref[j] > 0.0)
            def _():
                column(j, exact=True)
            return carry2

        lax.fori_loop(0, tt // LANES, redo, 0)
        return carry

    lax.fori_loop(0, nheads, head, 0)


def _peer_route(h2t, wq_t, k1, k2):
    d, t = h2t.shape
    nkeys, hc = k1.shape
    qrows = 2 * wq_t.shape[0]
    nheads = qrows // (2 * hc)
    tt = 512
    npairs = len(_candidate_pairs(PEER_TOPK))
    ncand = -(-npairs // SUBLANES) * SUBLANES
    ncol = tt // LANES
    kern = functools.partial(_route_kernel, nheads=nheads, nkeys=nkeys, topk=PEER_TOPK, tt=tt)
    kblk = pl.BlockSpec((nheads, tt // LANES, nkeys // 2, LANES), lambda i: (0, i, 0, 0))
    ksh = jax.ShapeDtypeStruct((nheads, t // LANES, nkeys // 2, LANES), jnp.uint32)
    oblk = pl.BlockSpec((nheads, nkeys, tt), lambda i: (0, 0, i))
    osh = jax.ShapeDtypeStruct((nheads, nkeys, t), F32)
    return pl.pallas_call(
        kern, grid=(t // tt,),
        in_specs=[pl.BlockSpec((d, tt), lambda i: (0, i)),
                  pl.BlockSpec(wq_t.shape, lambda i: (0, 0)),
                  pl.BlockSpec(k1.shape, lambda i: (0, 0)),
                  pl.BlockSpec(k2.shape, lambda i: (0, 0))],
        out_specs=[kblk, kblk, oblk, oblk], out_shape=[ksh, ksh, osh, osh],
        scratch_shapes=[pltpu.VMEM((qrows, tt), F32),
                        pltpu.VMEM((nkeys, tt), F32), pltpu.VMEM((nkeys, tt), F32),
                        pltpu.VMEM((ncol, PEER_TOPK, LANES), F32), pltpu.VMEM((ncol, PEER_TOPK, LANES), F32),
                        pltpu.VMEM((ncol, ncand, LANES), F32), pltpu.VMEM((ncol, ncand, LANES), F32),
                        pltpu.SMEM((ncol,), F32)],
        compiler_params=_cparams(("parallel",)), name="peer_route",
    )(h2t, wq_t, k1, k2)


def _pack_kernel(x_ref, o_ref, *, transpose):
    x = x_ref[0]
    if transpose:
        x = x.T
    o_ref[...] = pltpu.bitcast(x.astype(BF16), jnp.uint32)


def _pack_table(stacked, layer, *, transpose):
    _, r, c = stacked.shape
    tb = min(r, 512 if c <= 1024 else 128)
    in_spec = pl.BlockSpec((1, tb, c), lambda i: (layer, i, 0))
    if transpose:
        out_spec = pl.BlockSpec((c // 2, tb), lambda i: (0, i))
        out_shape = jax.ShapeDtypeStruct((c // 2, r), jnp.uint32)
    else:
        out_spec = pl.BlockSpec((tb // 2, c), lambda i: (i, 0))
        out_shape = jax.ShapeDtypeStruct((r // 2, c), jnp.uint32)
    return pl.pallas_call(
        functools.partial(_pack_kernel, transpose=transpose), grid=(r // tb,),
        in_specs=[in_spec], out_specs=out_spec, out_shape=out_shape,
        compiler_params=_cparams(("parallel",)), name="pack_table",
    )(stacked)


def _ffn_kernel(h_ref, x_ref, g_ref, u_ref, un_ref, vt_ref, r2_ref, e2_ref, n1_ref, e1_ref, o_ref,
                acc_ref, at_ref, pt_ref, *, nheads, nkeys, cw, final_norm):
    j = pl.program_id(1)
    tt = h_ref.shape[1]
    et = 2 * u_ref.shape[0]
    nhalf = cw // LANES

    def first_pre_activations(table_ref):
        at_ref[0] = _dot(pltpu.bitcast(table_ref[0:at_ref.shape[1] // 2, :], BF16), h_ref[:, 0:cw])

    @pl.when(j == 0)
    def _():
        acc_ref[...] = jnp.zeros_like(acc_ref)
        first_pre_activations(u_ref)

    ktiles = nkeys // BF16_ROWS
    es = at_ref.shape[1]
    nsub = es // nkeys
    units = [(eh, c) for eh in range(et // es) for c in range(tt // cw)]
    n_at, n_pt = at_ref.shape[0], pt_ref.shape[0]

    def pre_activations(k):
        eh, c = units[k]
        u = pltpu.bitcast(u_ref[eh * es // 2:(eh + 1) * es // 2, :], BF16)
        at_ref[k % n_at] = _dot(u, h_ref[:, c * cw:(c + 1) * cw])

    assert n_at >= len(units)
    for k, (eh, c) in enumerate(units):
        if k + 1 < len(units):
            pre_activations(k + 1)
        else:
            first_pre_activations(un_ref)
        for jj in range(nsub):
            row = eh * nsub + jj
            for hf in range(nhalf):
                ck = c * nhalf + hf
                cols = slice(ck * LANES, (ck + 1) * LANES)
                lanes = slice(hf * LANES, (hf + 1) * LANES)
                gate = jnp.zeros((ktiles, BF16_ROWS, LANES), BF16)
                for hd in range(nheads):
                    n_row = jnp.broadcast_to(n1_ref[hd, row:row + 1, cols], (BF16_ROWS, LANES)).astype(BF16)
                    e_row = jnp.broadcast_to(e1_ref[hd, row:row + 1, cols], (BF16_ROWS, LANES)).astype(BF16)
                    r2 = pltpu.bitcast(r2_ref[hd, ck], BF16).reshape(ktiles, BF16_ROWS, LANES)
                    e2 = pltpu.bitcast(e2_ref[hd, ck], BF16).reshape(ktiles, BF16_ROWS, LANES)
                    gate = gate + jnp.where(r2 < n_row[None], e2, 0) * e_row[None]
                a = at_ref[k % n_at, jj * nkeys:(jj + 1) * nkeys, lanes].astype(BF16)
                p = (gate * _gelu(a.reshape(ktiles, BF16_ROWS, LANES))).reshape(nkeys, LANES)
                pt_ref[k % n_pt, jj * nkeys // 2:(jj + 1) * nkeys // 2, lanes] = pltpu.bitcast(p, jnp.uint32)
        p = pltpu.bitcast(pt_ref[k % n_pt], BF16)
        vt = pltpu.bitcast(vt_ref[:, eh * es:(eh + 1) * es], BF16)
        acc_ref[c] += _dot(vt, p)

    @pl.when(j == pl.num_programs(1) - 1)
    def _():
        for c in range(tt // cw):
            rows = slice(c * cw, (c + 1) * cw)
            y = x_ref[rows, :] + acc_ref[c].T
            o_ref[rows, :] = _rmsnorm(y, g_ref[...]) if final_norm else y


def _peer_experts(h2t, x, u_tab, v_tab_t, r2, e2, n1, e1, final_g):
    d, t = h2t.shape
    ne = 2 * u_tab.shape[0]
    nheads, nkeys, _ = n1.shape
    tt, et, cw = 1024, 1024, 256
    es = nkeys * SUBLANES
    at_ring, pt_ring = 4, 4
    final_norm = final_g is not None
    g = (final_g if final_norm else jnp.ones((d,), F32)).reshape(1, d)
    kern = functools.partial(_ffn_kernel, nheads=nheads, nkeys=nkeys, cw=cw, final_norm=final_norm)
    full_keys = pl.BlockSpec((nheads, tt // LANES, nkeys // 2, LANES), lambda i, j: (0, i, 0, 0))
    sub_keys = pl.BlockSpec((nheads, et // nkeys, tt), lambda i, j: (0, j, i))
    return pl.pallas_call(
        kern, grid=(t // tt, ne // et),
        in_specs=[pl.BlockSpec((d, tt), lambda i, j: (0, i)),
                  pl.BlockSpec((tt, d), lambda i, j: (i, 0)),
                  pl.BlockSpec((1, d), lambda i, j: (0, 0)),
                  pl.BlockSpec((et // 2, d), lambda i, j: (j, 0)),
                  pl.BlockSpec((es // 2, d), lambda i, j: (jnp.minimum(j + 1, ne // et - 1) * (et // es), 0)),
                  pl.BlockSpec((d // 2, et), lambda i, j: (0, j)),
                  full_keys, full_keys, sub_keys, sub_keys],
        out_specs=pl.BlockSpec((tt, d), lambda i, j: (i, 0)),
        out_shape=jax.ShapeDtypeStruct((t, d), F32),
        scratch_shapes=[pltpu.VMEM((tt // cw, d, cw), F32),
                        pltpu.VMEM((at_ring, es, cw), F32),
                        pltpu.VMEM((pt_ring, es // 2, cw), jnp.uint32)],
        compiler_params=_cparams(("parallel", "arbitrary")), name="peer_experts",
    )(h2t, x, g, u_tab, u_tab, v_tab_t, r2, e2, n1, e1)


def kernel(x, positions, norm1_g, w_in, pool_w, pool_scale, pool_proj, ssm_a_re, ssm_a_im, ssm_log_dt,
           ssm_b_re, ssm_b_im, ssm_c_re, ssm_c_im, ssm_d, ssm_glu, ssm_proj, attn_proj, w_out, norm2_g,
           peer_wq, peer_k1, peer_k2, peer_u, peer_v, final_g):
    b, l, d = x.shape
    depth = w_in.shape[0]
    pool_w_cols = pool_proj.shape[1]
    ssm_w_cols = ssm_proj.shape[1]
    attn_w = len(ATTN_DILATIONS) * HEADS_PER_GROUP * HEAD_DIM
    cos, sin = _rope_tables(positions)
    for layer in range(depth):
        u_pool, u_ssm, q, k, v, gates = _in_projection(
            x, norm1_g[layer], _pack_table(w_in, layer, transpose=False), cos, sin,
            pool_w=pool_w_cols, ssm_w=ssm_w_cols, attn_w=attn_w)
        pm = _pool_mixer(u_pool, pool_w[layer].astype(BF16), pool_scale[layer])
        abar_re, abar_im, bbar_re_t, bbar_im_t = _ssm_params(
            ssm_a_re[layer], ssm_a_im[layer], ssm_log_dt[layer], ssm_b_re[layer], ssm_b_im[layer])
        ys = _ssm_mixer(u_ssm, b, abar_re, abar_im, bbar_re_t, bbar_im_t,
                        ssm_c_re[layer], ssm_c_im[layer], ssm_d[layer])
        ao = _attention(q, k, v)
        packed = [_pack_table(wt, layer, transpose=False)
                  for wt in (pool_proj, ssm_glu, ssm_proj, attn_proj, w_out)]
        x, h2t = _merge(pm, ys, ao, gates, x, *packed, norm2_g[layer])
        r2, e2, n1, e1 = _peer_route(h2t, _pack_table(peer_wq, layer, transpose=True), peer_k1[layer].astype(BF16),
                                     peer_k2[layer].astype(BF16))
        x = _peer_experts(h2t, x.reshape(b * l, d), _pack_table(peer_u, layer, transpose=False),
                          _pack_table(peer_v, layer, transpose=True), r2, e2, n1, e1,
                          final_g if layer == depth - 1 else None).reshape(b, l, d)
    return x
```

```python
import functools
import math

import jax
import jax.numpy as jnp
from jax import lax
from jax.experimental import pallas as pl
from jax.experimental.pallas import tpu as pltpu

F32 = jnp.float32
BF16 = jnp.bfloat16

EPS = 1e-6
POOL_WINDOWS = (2, 4, 8, 16)
POOL_GROUP = 128
SSM_GROUP = 16
SSM_GROUPS = 32
SSM_STATE = 64
HEAD_DIM = 64
HEADS_PER_GROUP = 4
ATTN_DILATIONS = (1, 4, 16)
ATTN_BLOCK = 128
ROPE_THETA = 10000.0
PEER_HEADS = 8
PEER_TOPK = 16

LANES = 128
SUBLANES = 8
BF16_ROWS = 16
VMEM_LIMIT = 56 * 1024 * 1024
NEG_BIG = -1e30


def _cparams(sem):
    return pltpu.CompilerParams(dimension_semantics=sem, vmem_limit_bytes=VMEM_LIMIT)


def _gelu(x):
    c = math.sqrt(2.0 / math.pi)
    return x * (0.5 * (1.0 + jnp.tanh(c * (x + 0.044715 * (x * x * x)))))


def _sigmoid(x):
    return 1.0 / (1.0 + jnp.exp(-x))


def _rmsnorm(x, g):
    ms = jnp.mean(x * x, axis=-1, keepdims=True)
    return x * lax.rsqrt(ms + EPS) * g


def _dot(a, b):
    return jnp.dot(a, b, preferred_element_type=F32)


def _dot_nt(a, b):
    return lax.dot_general(a, b, (((1,), (1,)), ((), ())), preferred_element_type=F32)


def _rope_kernel(pos_ref, invf_ref, sign_ref, cos_ref, sin_ref):
    ang = pos_ref[...].astype(F32) * invf_ref[...]
    cos_ref[...] = jnp.cos(ang)
    sin_ref[...] = jnp.sin(ang) * sign_ref[...]


def _rope_tables(positions):
    b, l = positions.shape
    t = b * l
    half = HEAD_DIM // 2
    inv_freq = ROPE_THETA ** (-jnp.arange(half, dtype=F32) / half)
    invf = jnp.tile(inv_freq, LANES // half).reshape(1, LANES)
    sign = jnp.tile(jnp.concatenate([-jnp.ones((half,), F32), jnp.ones((half,), F32)]),
                    LANES // HEAD_DIM).reshape(1, LANES)
    pos_b = jnp.broadcast_to(positions.reshape(t, 1), (t, LANES))
    tr = 2048
    cos, sin = pl.pallas_call(
        _rope_kernel,
        grid=(t // tr,),
        in_specs=[pl.BlockSpec((tr, LANES), lambda i: (i, 0)),
                  pl.BlockSpec((1, LANES), lambda i: (0, 0)),
                  pl.BlockSpec((1, LANES), lambda i: (0, 0))],
        out_specs=[pl.BlockSpec((tr, LANES), lambda i: (i, 0))] * 2,
        out_shape=[jax.ShapeDtypeStruct((t, LANES), F32)] * 2,
        compiler_params=_cparams(("parallel",)),
        name="rope_tables",
    )(pos_b, invf, sign)
    return cos.reshape(b, l, LANES), sin.reshape(b, l, LANES)


def _inproj_kernel(x_ref, g_ref, w_ref, cos_ref, sin_ref, pool_ref, ssm_ref, q_ref, k_ref, v_ref, gate_ref,
                   *, pool_w, ssm_w, attn_w, d_model):
    h = _rmsnorm(x_ref[0], g_ref[...]).astype(BF16)
    c0 = 0

    def w_cols(lo, hi):
        return pltpu.bitcast(w_ref[:, lo:hi], BF16)

    z = _dot(h, w_cols(c0, c0 + pool_w + ssm_w))
    pool_ref[0] = z[:, :pool_w]
    ssm_ref[...] = z[:, pool_w:]
    c0 += pool_w + ssm_w
    cos = cos_ref[0]
    sin = sin_ref[0]
    nslab = attn_w // LANES
    half = HEAD_DIM // 2
    first_half = lax.broadcasted_iota(jnp.int32, cos.shape, 1) % HEAD_DIM < half
    for dst in (q_ref, k_ref):
        z = _dot(h, w_cols(c0, c0 + attn_w))
        for s in range(nslab):
            a = z[:, s * LANES:(s + 1) * LANES]
            a_sw = jnp.where(first_half, pltpu.roll(a, LANES - half, axis=1), pltpu.roll(a, half, axis=1))
            dst[0, s] = a * cos + a_sw * sin
        c0 += attn_w
    z = _dot(h, w_cols(c0, c0 + attn_w))
    for s in range(nslab):
        v_ref[0, s] = z[:, s * LANES:(s + 1) * LANES]
    c0 += attn_w
    z = _dot(h, w_cols(c0, c0 + 3 * d_model))
    gate_ref[0] = pltpu.bitcast(_sigmoid(z).astype(BF16), jnp.uint32)


def _in_projection(x, g, w_packed, cos, sin, *, pool_w, ssm_w, attn_w):
    b, l, d = x.shape
    tm = 256
    nslab = attn_w // LANES
    kern = functools.partial(_inproj_kernel, pool_w=pool_w, ssm_w=ssm_w, attn_w=attn_w, d_model=d)
    row = lambda bi, i: (bi, i, 0)
    slab = lambda bi, i: (bi, 0, i, 0)
    const2 = lambda bi, i: (0, 0)
    in_specs = [pl.BlockSpec((1, tm, d), row),
                pl.BlockSpec((1, d), const2),
                pl.BlockSpec(w_packed.shape, const2, pipeline_mode=pl.Buffered(1)),
                pl.BlockSpec((1, tm, LANES), row),
                pl.BlockSpec((1, tm, LANES), row)]
    args = [x, g.reshape(1, d), w_packed, cos, sin]
    out_specs = [pl.BlockSpec((1, tm, pool_w), row),
                 pl.BlockSpec((tm, ssm_w), lambda bi, i: (i, bi)),
                 pl.BlockSpec((1, nslab, tm, LANES), slab),
                 pl.BlockSpec((1, nslab, tm, LANES), slab),
                 pl.BlockSpec((1, nslab, tm, LANES), slab),
                 pl.BlockSpec((1, tm // 2, 3 * d), row)]
    out_shape = [jax.ShapeDtypeStruct((b, l, pool_w), F32),
                 jax.ShapeDtypeStruct((l, b * ssm_w), F32),
                 jax.ShapeDtypeStruct((b, nslab, l, LANES), F32),
                 jax.ShapeDtypeStruct((b, nslab, l, LANES), F32),
                 jax.ShapeDtypeStruct((b, nslab, l, LANES), F32),
                 jax.ShapeDtypeStruct((b, l // 2, 3 * d), jnp.uint32)]
    return pl.pallas_call(
        kern, grid=(b, l // tm), in_specs=in_specs, out_specs=out_specs, out_shape=out_shape,
        compiler_params=_cparams(("parallel", "parallel")), name="in_projection",
    )(*args)


def _pool_kernel(u_ref, halo_ref, w_ref, scale_ref, o_ref, *, tm, halo):
    i = pl.program_id(1)
    cur = u_ref[0]
    prev = jnp.where(i > 0, halo_ref[0], 0.0)
    ext = jnp.concatenate([prev, cur], axis=0)
    t_idx = i * tm + lax.broadcasted_iota(jnp.int32, (tm, POOL_GROUP), 0)
    for gi, win in enumerate(POOL_WINDOWS):
        cs = slice(gi * POOL_GROUP, (gi + 1) * POOL_GROUP)
        s = ext[:, cs]
        sh = 1
        while sh < win:
            s = s + pltpu.roll(s, sh, axis=0)
            sh *= 2
        count = jnp.minimum(t_idx + 1, win).astype(F32)
        pooled = s[halo:, :] / count - cur[:, cs]
        mixed = _dot(pooled.astype(BF16), w_ref[gi]) * scale_ref[:, cs]
        o_ref[0, :, cs] = mixed.astype(BF16)


def _pool_mixer(u, w, scale):
    b, l, c = u.shape
    tm, halo = 512, 16
    assert halo >= max(POOL_WINDOWS) and halo % SUBLANES == 0
    kern = functools.partial(_pool_kernel, tm=tm, halo=halo)
    return pl.pallas_call(
        kern, grid=(b, l // tm),
        in_specs=[pl.BlockSpec((1, tm, c), lambda bi, i: (bi, i, 0)),
                  pl.BlockSpec((1, halo, c), lambda bi, i: (bi, jnp.maximum(i * (tm // halo) - 1, 0), 0)),
                  pl.BlockSpec(w.shape, lambda bi, i: (0, 0, 0)),
                  pl.BlockSpec((1, c), lambda bi, i: (0, 0))],
        out_specs=pl.BlockSpec((1, tm, c), lambda bi, i: (bi, i, 0)),
        out_shape=jax.ShapeDtypeStruct((b, l, c), BF16),
        compiler_params=_cparams(("parallel", "parallel")), name="pool_mixer",
    )(u, u, w, scale.reshape(1, c))


def _ssm_param_kernel(are_ref, aim_ref, ldt_ref, bre_ref, bim_ref, oar_ref, oai_ref, obr_ref, obi_ref):
    ar = are_ref[...]
    ai = aim_ref[...]
    dt = jnp.exp(ldt_ref[...])
    decay = jnp.exp(ar * dt)
    abar_re = decay * jnp.cos(ai * dt)
    abar_im = decay * jnp.sin(ai * dt)
    den = ar * ar + ai * ai
    nr = abar_re - 1.0
    k_re = (nr * ar + abar_im * ai) / den
    k_im = (abar_im * ar - nr * ai) / den
    oar_ref[...] = abar_re
    oai_ref[...] = abar_im
    for gi in range(are_ref.shape[0]):
        kr = k_re[gi:gi + 1, :]
        ki = k_im[gi:gi + 1, :]
        br = bre_ref[gi]
        bi = bim_ref[gi]
        obr_ref[gi] = kr * br - ki * bi
        obi_ref[gi] = kr * bi + ki * br


def _ssm_params(a_re, a_im, log_dt, b_re, b_im):
    g, n, p = b_re.shape
    brt = jnp.transpose(b_re, (0, 2, 1))
    bit = jnp.transpose(b_im, (0, 2, 1))
    return pl.pallas_call(
        _ssm_param_kernel,
        out_shape=[jax.ShapeDtypeStruct((g, n), F32)] * 2 + [jax.ShapeDtypeStruct((g, p, n), F32)] * 2,
        name="ssm_params",
    )(a_re, a_im, log_dt.reshape(g, 1), brt, bit)


def _block_diag(m):
    k, r, c = m.shape
    eye = jnp.eye(k, dtype=m.dtype)
    return (m[:, :, None, :] * eye[:, None, :, None]).reshape(k * r, k * c)


def _ssm_kernel(u_ref, bre_ref, bim_ref, are_ref, aim_ref, cre_ref, cim_ref, d_ref, y_ref,
                st_ref, xre_ref, xim_ref, sre_ref, sim_ref, uh_ref, yo_ref, *, tc, hw):
    nb = SUBLANES // 2
    hrows = tc * nb
    nslab = hw // LANES
    nstate = xre_ref.shape[0]

    @pl.when(pl.program_id(0) == 0)
    def _():
        st_ref[...] = jnp.zeros_like(st_ref)

    def seq_lanes(b, h, hf):
        s = b * 2 + h
        return slice(s * hw + hf * LANES, s * hw + (hf + 1) * LANES)

    for b in range(nb):
        for h in range(2):
            for hf in range(nslab):
                uh_ref[h, hf, pl.ds(b, tc, stride=nb), :] = u_ref[:, seq_lanes(b, h, hf)]
    for h in range(2):
        ub = jnp.concatenate([uh_ref[h, hf] for hf in range(nslab)], axis=1).astype(BF16)
        x_re = _dot(ub, bre_ref[h])
        x_im = _dot(ub, bim_ref[h])
        for j in range(nstate):
            xre_ref[j, pl.ds(h, hrows, stride=2), :] = x_re[:, j * LANES:(j + 1) * LANES]
            xim_ref[j, pl.ds(h, hrows, stride=2), :] = x_im[:, j * LANES:(j + 1) * LANES]
    a_re = are_ref[...]
    a_im = aim_ref[...]

    def step(t, carry):
        s_re, s_im = carry
        rows = pl.ds(pl.multiple_of(t * SUBLANES, SUBLANES), SUBLANES)
        n_re = a_re * s_re - a_im * s_im + xre_ref[:, rows, :]
        n_im = a_re * s_im + a_im * s_re + xim_ref[:, rows, :]
        sre_ref[:, rows, :] = n_re
        sim_ref[:, rows, :] = n_im
        return n_re, n_im

    s_re, s_im = lax.fori_loop(0, tc, step, (st_ref[0], st_ref[1]), unroll=8)
    st_ref[0] = s_re
    st_ref[1] = s_im
    for h in range(2):
        s_r = jnp.concatenate([sre_ref[j, pl.ds(h, hrows, stride=2), :] for j in range(nstate)], axis=1)
        s_i = jnp.concatenate([sim_ref[j, pl.ds(h, hrows, stride=2), :] for j in range(nstate)], axis=1)
        u_h = jnp.concatenate([uh_ref[h, hf] for hf in range(nslab)], axis=1)
        y = _gelu(_dot(s_r.astype(BF16), cre_ref[h]) - _dot(s_i.astype(BF16), cim_ref[h]) + d_ref[h] * u_h)
        for hf in range(nslab):
            yo_ref[h, hf] = y[:, hf * LANES:(hf + 1) * LANES]
    for b in range(nb):
        for h in range(2):
            for hf in range(nslab):
                y_ref[:, seq_lanes(b, h, hf)] = yo_ref[h, hf, pl.ds(b, tc, stride=nb), :].astype(BF16)


def _ssm_mixer(u_scan, batch, abar_re, abar_im, bbar_re_t, bbar_im_t, c_re, c_im, d_skip):
    l = u_scan.shape[0]
    g, p, n = bbar_re_t.shape
    hg = g // 2
    hw = hg * p
    hs = hg * n
    assert batch * 2 == SUBLANES

    nstate = hs // LANES

    def half_b(bt):
        return jnp.stack([_block_diag(bt[:hg]), _block_diag(bt[hg:])]).astype(BF16)

    def half_c(c):
        ct = jnp.transpose(c, (0, 2, 1))
        return jnp.stack([_block_diag(ct[:hg]), _block_diag(ct[hg:])]).astype(BF16)

    def tile_a(a):
        a = jnp.transpose(a.reshape(2, nstate, LANES), (1, 0, 2))
        return jnp.broadcast_to(a[:, None], (nstate, batch, 2, LANES)).reshape(nstate, SUBLANES, LANES)

    tc = 64
    rows = tc * SUBLANES
    kern = functools.partial(_ssm_kernel, tc=tc, hw=hw)
    const = lambda i: (0, 0, 0)
    state = pltpu.VMEM((nstate, rows, LANES), F32)
    halves = pltpu.VMEM((2, hw // LANES, rows // 2, LANES), F32)
    return pl.pallas_call(
        kern, grid=(l // tc,),
        in_specs=[pl.BlockSpec((tc, SUBLANES * hw), lambda i: (i, 0)),
                  pl.BlockSpec((2, hw, hs), const), pl.BlockSpec((2, hw, hs), const),
                  pl.BlockSpec((nstate, SUBLANES, LANES), const), pl.BlockSpec((nstate, SUBLANES, LANES), const),
                  pl.BlockSpec((2, hs, hw), const), pl.BlockSpec((2, hs, hw), const),
                  pl.BlockSpec((2, 1, hw), const)],
        out_specs=pl.BlockSpec((tc, SUBLANES * hw), lambda i: (i, 0)),
        out_shape=jax.ShapeDtypeStruct((l, SUBLANES * hw), BF16),
        scratch_shapes=[pltpu.VMEM((2, nstate, SUBLANES, LANES), F32), state, state, state, state, halves, halves],
        compiler_params=_cparams(("arbitrary",)), name="ssm_mixer",
    )(u_scan, half_b(bbar_re_t), half_b(bbar_im_t), tile_a(abar_re), tile_a(abar_im),
      half_c(c_re), half_c(c_im), d_skip.reshape(2, 1, hw))


def _attn_group(q_ref, k_ref, v_ref, kb, vb, acc, mrun, drun, og, mg, lg, *, d, ta, i):
    p = ATTN_BLOCK * d
    ncombo = ta // ATTN_BLOCK
    nslab = q_ref.shape[1]
    scale = HEAD_DIM ** -0.5
    kb[:, p:p + ta, :] = k_ref[0]
    vb[:, p:p + ta, :] = v_ref[0]

    @pl.when(i == 0)
    def _():
        kb[:, :p, :] = jnp.zeros((nslab, p, LANES), F32)
        vb[:, :p, :] = jnp.zeros((nslab, p, LANES), F32)

    nh = LANES // HEAD_DIM
    qi = lax.broadcasted_iota(jnp.int32, (nh * ATTN_BLOCK, ATTN_BLOCK), 0) % ATTN_BLOCK
    kk = lax.broadcasted_iota(jnp.int32, (nh * ATTN_BLOCK, ATTN_BLOCK), 1)
    mask_cur = kk <= qi
    mask_prev = kk >= qi
    low = lax.broadcasted_iota(jnp.int32, (ATTN_BLOCK, ATTN_BLOCK), 1) < HEAD_DIM

    def rows_at(start):
        return pl.ds(start, ATTN_BLOCK) if d == 1 else pl.ds(start, ATTN_BLOCK, stride=d)

    def combo(c, carry):
        r = c % d
        n = c // d
        qs = n * p + r
        has_prev = jnp.logical_or(i > 0, n > 0)
        mask_p = jnp.logical_and(mask_prev, has_prev)
        for s in range(nslab):
            q = q_ref[0, s, rows_at(qs), :] * scale
            kc = kb[s, rows_at(p + qs), :].astype(BF16)
            kp = kb[s, rows_at(qs), :].astype(BF16)
            vc = vb[s, rows_at(p + qs), :].astype(BF16)
            vp = vb[s, rows_at(qs), :].astype(BF16)
            qh = jnp.concatenate([jnp.where(low, q, 0.0), jnp.where(low, 0.0, q)], axis=0).astype(BF16)
            sc = jnp.where(mask_cur, _dot_nt(qh, kc), -jnp.inf)
            sp = jnp.where(mask_p, _dot_nt(qh, kp), -jnp.inf)
            m = jnp.max(jnp.maximum(sc, sp), axis=1, keepdims=True)
            pc = jnp.exp(sc - m)
            pp = jnp.exp(sp - m)
            den = jnp.sum(pc + pp, axis=1, keepdims=True)
            o = _dot(pc.astype(BF16), vc) + _dot(pp.astype(BF16), vp)
            rows = rows_at(qs)
            og[s, rows, :] = jnp.where(low, o[:ATTN_BLOCK], o[ATTN_BLOCK:])
            mg[s, rows, :] = jnp.where(low, m[:ATTN_BLOCK], m[ATTN_BLOCK:])
            lg[s, rows, :] = jnp.where(low, den[:ATTN_BLOCK], den[ATTN_BLOCK:])
        return carry

    lax.fori_loop(0, ncombo, combo, 0, unroll=2)
    kb[:, :p, :] = kb[:, ta:ta + p, :]
    vb[:, :p, :] = vb[:, ta:ta + p, :]

    rc = 256
    for s in range(nslab):
        for r0 in range(0, ta, rc):
            rs = slice(r0, r0 + rc)
            m_old = mrun[s, rs, :]
            m_blk = mg[s, rs, :]
            m_new = jnp.maximum(m_old, m_blk)
            a_old = jnp.exp(m_old - m_new)
            a_blk = jnp.exp(m_blk - m_new)
            acc[s, rs, :] = acc[s, rs, :] * a_old + og[s, rs, :] * a_blk
            drun[s, rs, :] = drun[s, rs, :] * a_old + lg[s, rs, :] * a_blk
            mrun[s, rs, :] = m_new


def _attn_kernel(q_ref, k_ref, v_ref, o_ref, *scratch, ta):
    ng = len(ATTN_DILATIONS)
    kbs = scratch[0:2 * ng:2]
    vbs = scratch[1:2 * ng:2]
    acc, mrun, drun, og, mg, lg = scratch[2 * ng:]
    i = pl.program_id(1)
    g = pl.program_id(2)

    @pl.when(g == 0)
    def _():
        acc[...] = jnp.zeros_like(acc)
        drun[...] = jnp.zeros_like(drun)
        mrun[...] = jnp.full(mrun.shape, NEG_BIG, F32)

    for gi, d in enumerate(ATTN_DILATIONS):
        @pl.when(g == gi)
        def _(gi=gi, d=d):
            _attn_group(q_ref, k_ref, v_ref, kbs[gi], vbs[gi], acc, mrun, drun, og, mg, lg, d=d, ta=ta, i=i)

    @pl.when(g == ng - 1)
    def _():
        for s in range(acc.shape[0]):
            o_ref[0, :, s * LANES:(s + 1) * LANES] = acc[s] / drun[s]


def _attention(q, k, v):
    b, nslab, l, _ = q.shape
    ng = len(ATTN_DILATIONS)
    gs = nslab // ng
    ta = ATTN_BLOCK * max(ATTN_DILATIONS)
    assert l % ta == 0
    blk = pl.BlockSpec((1, gs, ta, LANES), lambda bi, i, g: (bi, g, i, 0))
    scratch = []
    for d in ATTN_DILATIONS:
        scratch += [pltpu.VMEM((gs, ATTN_BLOCK * d + ta, LANES), F32)] * 2
    scratch += [pltpu.VMEM((gs, ta, LANES), F32)] * 6
    return pl.pallas_call(
        functools.partial(_attn_kernel, ta=ta), grid=(b, l // ta, ng),
        in_specs=[blk, blk, blk],
        out_specs=pl.BlockSpec((1, ta, gs * LANES), lambda bi, i, g: (bi, i, 0)),
        out_shape=jax.ShapeDtypeStruct((b, l, gs * LANES), F32),
        scratch_shapes=scratch,
        compiler_params=_cparams(("arbitrary", "arbitrary", "arbitrary")), name="dilated_attention",
    )(q, k, v)


def _merge_kernel(pm_ref, ys_ref, ao_ref, gate_ref, x_ref, wp_ref, wg_ref, ws_ref, wa_ref, wo_ref, g2_ref,
                  xo_ref, h2_ref, *, d_model, ssm_w):
    def weight(ref):
        return pltpu.bitcast(ref[...], BF16)

    y_pool = _dot(pm_ref[0], weight(wp_ref))
    glu = _dot(ys_ref[...], weight(wg_ref))
    sg = (glu[:, :ssm_w] * _sigmoid(glu[:, ssm_w:])).astype(BF16)
    y_ssm = _dot(sg, weight(ws_ref))
    y_attn = _dot(ao_ref[0].astype(BF16), weight(wa_ref))
    gate = pltpu.bitcast(gate_ref[0], BF16).astype(F32)
    merged = (gate[:, :d_model] * y_pool + gate[:, d_model:2 * d_model] * y_ssm
              + gate[:, 2 * d_model:] * y_attn)
    xn = x_ref[0] + _dot(merged.astype(BF16), weight(wo_ref))
    xo_ref[0] = xn
    h2_ref[...] = _rmsnorm(xn, g2_ref[...]).T.astype(BF16)


def _merge(pm, ys, ao, gates, x, w_pool, w_glu, w_ssm, w_attn, w_out, g2):
    b, l, d = x.shape
    tm = 256
    ssm_w = 2 * w_glu.shape[0]
    row = lambda bi, i: (bi, i, 0)
    const = lambda bi, i: (0, 0)
    full = lambda a: pl.BlockSpec(a.shape, const)
    return pl.pallas_call(
        functools.partial(_merge_kernel, d_model=d, ssm_w=ssm_w), grid=(b, l // tm),
        in_specs=[pl.BlockSpec((1, tm, pm.shape[-1]), row),
                  pl.BlockSpec((tm, ssm_w), lambda bi, i: (i, bi)),
                  pl.BlockSpec((1, tm, ao.shape[-1]), row),
                  pl.BlockSpec((1, tm // 2, 3 * d), row),
                  pl.BlockSpec((1, tm, d), row),
                  full(w_pool), full(w_glu), full(w_ssm), full(w_attn), full(w_out),
                  pl.BlockSpec((1, d), const)],
        out_specs=[pl.BlockSpec((1, tm, d), row),
                   pl.BlockSpec((d, tm), lambda bi, i: (0, bi * (l // tm) + i))],
        out_shape=[jax.ShapeDtypeStruct((b, l, d), F32), jax.ShapeDtypeStruct((d, b * l), BF16)],
        compiler_params=_cparams(("parallel", "parallel")), name="branch_merge",
    )(pm, ys, ao, gates, x, w_pool, w_glu, w_ssm, w_attn, w_out, g2.reshape(1, d))


def _candidate_pairs(k):
    return [(a, b) for a in range(k) for b in range(k) if (a + 1) * (b + 1) <= k]


_CODE_UNIT = 2.0 ** 121
_CODE_BASE = 80


def _removed_code(r):
    return -float(_CODE_BASE + r) * _CODE_UNIT


def _removed_round(w):
    return w * (-1.0 / _CODE_UNIT) - float(_CODE_BASE)


def _vreg_rows(x):
    return x.reshape(x.shape[0] // SUBLANES, SUBLANES, x.shape[1])


def _max_all_rows(w):
    p = jnp.max(w, axis=0)
    for sh in (4, 2, 1):
        p = jnp.maximum(p, pltpu.roll(p, sh, axis=0))
    return p


def _topk_round(work, rows):
    m = jnp.max(work, axis=0, keepdims=True)
    idx = jnp.min(jnp.where(work == m, rows, float(work.shape[0])), axis=0, keepdims=True)
    hit = rows == idx
    return m, hit, jnp.where(hit, -jnp.inf, work)


def _route_kernel(h_ref, wq_ref, k1_ref, k2_ref, r2_ref, e2_ref, n1_ref, e1_ref,
                  qt_ref, s1_ref, s2_ref, v1_ref, v2_ref, cand_ref, sel_ref, tied_ref,
                  *, nheads, nkeys, topk, tt):
    qt_ref[...] = _dot(pltpu.bitcast(wq_ref[...], BF16), h_ref[...])
    hc = k1_ref.shape[1]
    pairs = _candidate_pairs(topk)
    ncand = cand_ref.shape[1]

    def head(hd, carry):
        base = pl.multiple_of(hd * 2 * hc, 2 * hc)
        q1 = qt_ref[pl.ds(base, hc), :].astype(BF16)
        q2 = qt_ref[pl.ds(base + hc, hc), :].astype(BF16)
        s1_ref[...] = _dot(k1_ref[...], q1)
        s2_ref[...] = _dot(k2_ref[...], q2)

        def column(j, exact):
            if isinstance(j, int):
                cols = slice(j * LANES, (j + 1) * LANES)
            else:
                cols = pl.ds(pl.multiple_of(j * LANES, LANES), LANES)
            s1 = s1_ref[:, cols]
            s2 = s2_ref[:, cols]
            v1, v2, cand, selr = v1_ref.at[j], v2_ref.at[j], cand_ref.at[j], sel_ref.at[j]

            def build_candidates():
                for ci, (a, bq) in enumerate(pairs):
                    cand[ci:ci + 1, :] = v1[a:a + 1, :] + v2[bq:bq + 1, :]
                if ncand > len(pairs):
                    cand[len(pairs):, :] = jnp.full((ncand - len(pairs), LANES), -jnp.inf, F32)
                return cand[...]

            def finish(rank1, rank2, sel, cv):
                selr[...] = sel
                top = v1[0:1, :] + v2[0:1, :]
                z = jnp.sum(sel * jnp.exp(jnp.where(sel > 0.0, cv, top) - top), axis=0, keepdims=True)
                n1 = jnp.zeros(s1.shape, F32)
                off = 0
                for a in range(topk):
                    cnt = topk // (a + 1)
                    n_a = jnp.sum(selr[off:off + cnt, :], axis=0, keepdims=True)
                    n1 = jnp.where(rank1 == float(a), n_a, n1)
                    off += cnt
                n1_ref[hd, :, cols] = n1
                e1_ref[hd, :, cols] = jnp.exp(s1 - v1[0:1, :]) / z
                r2_ref[hd, j] = pltpu.bitcast(rank2.astype(BF16), jnp.uint32)
                e2_ref[hd, j] = pltpu.bitcast(jnp.exp(s2 - v2[0:1, :]).astype(BF16), jnp.uint32)

            if exact:
                key_rows = lax.broadcasted_iota(jnp.int32, s1.shape, 0).astype(F32)
                x1, x2 = s1, s2
                rk1 = jnp.full(s1.shape, float(nkeys), F32)
                rk2 = rk1
                for r in range(topk):
                    m1, hit1, x1 = _topk_round(x1, key_rows)
                    m2, hit2, x2 = _topk_round(x2, key_rows)
                    rk1 = jnp.where(hit1, float(r), rk1)
                    rk2 = jnp.where(hit2, float(r), rk2)
                    v1[r:r + 1, :] = m1
                    v2[r:r + 1, :] = m2
                cvs = build_candidates()
                cand_rows = lax.broadcasted_iota(jnp.int32, cvs.shape, 0).astype(F32)
                xc = cvs
                sl = jnp.zeros(cvs.shape, F32)
                for r in range(topk):
                    _, hit, xc = _topk_round(xc, cand_rows)
                    sl = jnp.where(hit, 1.0, sl)
                finish(rk1, rk2, sl, cvs)
                return None

            w1, w2 = _vreg_rows(s1), _vreg_rows(s2)
            for r in range(topk):
                m1 = _max_all_rows(w1)
                m2 = _max_all_rows(w2)
                w1 = jnp.where(w1 == m1[None], _removed_code(r), w1)
                w2 = jnp.where(w2 == m2[None], _removed_code(r), w2)
                v1[r:r + 1, :] = m1[0:1, :]
                v2[r:r + 1, :] = m2[0:1, :]
            w1, w2 = w1.reshape(s1.shape), w2.reshape(s2.shape)
            gone1 = w1 <= _removed_code(0)
            gone2 = w2 <= _removed_code(0)
            rank1 = jnp.where(gone1, _removed_round(w1), float(nkeys))
            rank2 = jnp.where(gone2, _removed_round(w2), float(nkeys))
            cv = build_candidates()
            wc = _vreg_rows(cv)
            for r in range(topk):
                wc = jnp.where(wc == _max_all_rows(wc)[None], _removed_code(r), wc)
            sel = jnp.where(wc.reshape(cv.shape) <= _removed_code(0), 1.0, 0.0)
            removed = (jnp.sum(jnp.where(gone1, 1.0, 0.0), axis=0, keepdims=True)
                       + jnp.sum(jnp.where(gone2, 1.0, 0.0), axis=0, keepdims=True)
                       + jnp.sum(sel, axis=0, keepdims=True))
            finish(rank1, rank2, sel, cv)
            expected = float(3 * topk + ncand - len(pairs))
            return jnp.max(jnp.where(removed != expected, 1.0, 0.0))

        for j in range(tt // LANES):
            tied_ref[j] = column(j, exact=False)

        def redo(j, carry2):
            @pl.when(tied_ref[j] > 0.0)
            def _():
                column(j, exact=True)
            return carry2

        lax.fori_loop(0, tt // LANES, redo, 0)
        return carry

    lax.fori_loop(0, nheads, head, 0)


def _peer_route(h2t, wq_t, k1, k2):
    d, t = h2t.shape
    nkeys, hc = k1.shape
    qrows = 2 * wq_t.shape[0]
    nheads = qrows // (2 * hc)
    tt = 1024
    npairs = len(_candidate_pairs(PEER_TOPK))
    ncand = -(-npairs // SUBLANES) * SUBLANES
    ncol = tt // LANES
    kern = functools.partial(_route_kernel, nheads=nheads, nkeys=nkeys, topk=PEER_TOPK, tt=tt)
    kblk = pl.BlockSpec((nheads, tt // LANES, nkeys // 2, LANES), lambda i: (0, i, 0, 0))
    ksh = jax.ShapeDtypeStruct((nheads, t // LANES, nkeys // 2, LANES), jnp.uint32)
    oblk = pl.BlockSpec((nheads, nkeys, tt), lambda i: (0, 0, i))
    osh = jax.ShapeDtypeStruct((nheads, nkeys, t), F32)
    return pl.pallas_call(
        kern, grid=(t // tt,),
        in_specs=[pl.BlockSpec((d, tt), lambda i: (0, i)),
                  pl.BlockSpec(wq_t.shape, lambda i: (0, 0)),
                  pl.BlockSpec(k1.shape, lambda i: (0, 0)),
                  pl.BlockSpec(k2.shape, lambda i: (0, 0))],
        out_specs=[kblk, kblk, oblk, oblk], out_shape=[ksh, ksh, osh, osh],
        scratch_shapes=[pltpu.VMEM((qrows, tt), F32),
                        pltpu.VMEM((nkeys, tt), F32), pltpu.VMEM((nkeys, tt), F32),
                        pltpu.VMEM((ncol, PEER_TOPK, LANES), F32), pltpu.VMEM((ncol, PEER_TOPK, LANES), F32),
                        pltpu.VMEM((ncol, ncand, LANES), F32), pltpu.VMEM((ncol, ncand, LANES), F32),
                        pltpu.SMEM((ncol,), F32)],
        compiler_params=_cparams(("parallel",)), name="peer_route",
    )(h2t, wq_t, k1, k2)


def _pack_kernel(x_ref, o_ref, *, transpose):
    x = x_ref[0]
    if transpose:
        x = x.T
    o_ref[...] = pltpu.bitcast(x.astype(BF16), jnp.uint32)


def _pack_table(stacked, layer, *, transpose):
    _, r, c = stacked.shape
    tb = min(r, 512 if c <= 1024 else 128)
    in_spec = pl.BlockSpec((1, tb, c), lambda i: (layer, i, 0))
    if transpose:
        out_spec = pl.BlockSpec((c // 2, tb), lambda i: (0, i))
        out_shape = jax.ShapeDtypeStruct((c // 2, r), jnp.uint32)
    else:
        out_spec = pl.BlockSpec((tb // 2, c), lambda i: (i, 0))
        out_shape = jax.ShapeDtypeStruct((r // 2, c), jnp.uint32)
    return pl.pallas_call(
        functools.partial(_pack_kernel, transpose=transpose), grid=(r // tb,),
        in_specs=[in_spec], out_specs=out_spec, out_shape=out_shape,
        compiler_params=_cparams(("parallel",)), name="pack_table",
    )(stacked)


def _ffn_kernel(h_ref, x_ref, g_ref, u_ref, un_ref, vt_ref, r2_ref, e2_ref, n1_ref, e1_ref, o_ref,
                acc_ref, at_ref, pt_ref, *, nheads, nkeys, cw, final_norm):
    j = pl.program_id(1)
    tt = h_ref.shape[1]
    et = 2 * u_ref.shape[0]
    nhalf = cw // LANES

    def first_pre_activations(table_ref):
        at_ref[0] = _dot(pltpu.bitcast(table_ref[0:at_ref.shape[1] // 2, :], BF16), h_ref[:, 0:cw])

    @pl.when(j == 0)
    def _():
        acc_ref[...] = jnp.zeros_like(acc_ref)
        first_pre_activations(u_ref)

    ktiles = nkeys // BF16_ROWS
    es = at_ref.shape[1]
    nsub = es // nkeys
    units = [(eh, c) for eh in range(et // es) for c in range(tt // cw)]
    n_at, n_pt = at_ref.shape[0], pt_ref.shape[0]

    def pre_activations(k):
        eh, c = units[k]
        u = pltpu.bitcast(u_ref[eh * es // 2:(eh + 1) * es // 2, :], BF16)
        at_ref[k % n_at] = _dot(u, h_ref[:, c * cw:(c + 1) * cw])

    assert n_at >= len(units)
    for k, (eh, c) in enumerate(units):
        if k + 1 < len(units):
            pre_activations(k + 1)
        else:
            first_pre_activations(un_ref)
        for jj in range(nsub):
            row = eh * nsub + jj
            for hf in range(nhalf):
                ck = c * nhalf + hf
                cols = slice(ck * LANES, (ck + 1) * LANES)
                lanes = slice(hf * LANES, (hf + 1) * LANES)
                gate = jnp.zeros((ktiles, BF16_ROWS, LANES), BF16)
                for hd in range(nheads):
                    n_row = jnp.broadcast_to(n1_ref[hd, row:row + 1, cols], (BF16_ROWS, LANES)).astype(BF16)
                    e_row = jnp.broadcast_to(e1_ref[hd, row:row + 1, cols], (BF16_ROWS, LANES)).astype(BF16)
                    r2 = pltpu.bitcast(r2_ref[hd, ck], BF16).reshape(ktiles, BF16_ROWS, LANES)
                    e2 = pltpu.bitcast(e2_ref[hd, ck], BF16).reshape(ktiles, BF16_ROWS, LANES)
                    gate = gate + jnp.where(r2 < n_row[None], e2, 0) * e_row[None]
                a = at_ref[k % n_at, jj * nkeys:(jj + 1) * nkeys, lanes].astype(BF16)
                p = (gate * _gelu(a.reshape(ktiles, BF16_ROWS, LANES))).reshape(nkeys, LANES)
                pt_ref[k % n_pt, jj * nkeys // 2:(jj + 1) * nkeys // 2, lanes] = pltpu.bitcast(p, jnp.uint32)
        p = pltpu.bitcast(pt_ref[k % n_pt], BF16)
        vt = pltpu.bitcast(vt_ref[:, eh * es:(eh + 1) * es], BF16)
        acc_ref[c] += _dot(vt, p)

    @pl.when(j == pl.num_programs(1) - 1)
    def _():
        for c in range(tt // cw):
            rows = slice(c * cw, (c + 1) * cw)
            y = x_ref[rows, :] + acc_ref[c].T
            o_ref[rows, :] = _rmsnorm(y, g_ref[...]) if final_norm else y


def _peer_experts(h2t, x, u_tab, v_tab_t, r2, e2, n1, e1, final_g):
    d, t = h2t.shape
    ne = 2 * u_tab.shape[0]
    nheads, nkeys, _ = n1.shape
    tt, et, cw = 1024, 1024, 256
    es = nkeys * SUBLANES
    at_ring, pt_ring = 4, 4
    final_norm = final_g is not None
    g = (final_g if final_norm else jnp.ones((d,), F32)).reshape(1, d)
    kern = functools.partial(_ffn_kernel, nheads=nheads, nkeys=nkeys, cw=cw, final_norm=final_norm)
    full_keys = pl.BlockSpec((nheads, tt // LANES, nkeys // 2, LANES), lambda i, j: (0, i, 0, 0))
    sub_keys = pl.BlockSpec((nheads, et // nkeys, tt), lambda i, j: (0, j, i))
    return pl.pallas_call(
        kern, grid=(t // tt, ne // et),
        in_specs=[pl.BlockSpec((d, tt), lambda i, j: (0, i)),
                  pl.BlockSpec((tt, d), lambda i, j: (i, 0)),
                  pl.BlockSpec((1, d), lambda i, j: (0, 0)),
                  pl.BlockSpec((et // 2, d), lambda i, j: (j, 0)),
                  pl.BlockSpec((es // 2, d), lambda i, j: (jnp.minimum(j + 1, ne // et - 1) * (et // es), 0)),
                  pl.BlockSpec((d // 2, et), lambda i, j: (0, j)),
                  full_keys, full_keys, sub_keys, sub_keys],
        out_specs=pl.BlockSpec((tt, d), lambda i, j: (i, 0)),
        out_shape=jax.ShapeDtypeStruct((t, d), F32),
        scratch_shapes=[pltpu.VMEM((tt // cw, d, cw), F32),
                        pltpu.VMEM((at_ring, es, cw), F32),
                        pltpu.VMEM((pt_ring, es // 2, cw), jnp.uint32)],
        compiler_params=_cparams(("parallel", "arbitrary")), name="peer_experts",
    )(h2t, x, g, u_tab, u_tab, v_tab_t, r2, e2, n1, e1)


def kernel(x, positions, norm1_g, w_in, pool_w, pool_scale, pool_proj, ssm_a_re, ssm_a_im, ssm_log_dt,
           ssm_b_re, ssm_b_im, ssm_c_re, ssm_c_im, ssm_d, ssm_glu, ssm_proj, attn_proj, w_out, norm2_g,
           peer_wq, peer_k1, peer_k2, peer_u, peer_v, final_g):
    b, l, d = x.shape
    depth = w_in.shape[0]
    pool_w_cols = pool_proj.shape[1]
    ssm_w_cols = ssm_proj.shape[1]
    attn_w = len(ATTN_DILATIONS) * HEADS_PER_GROUP * HEAD_DIM
    cos, sin = _rope_tables(positions)
    for layer in range(depth):
        u_pool, u_ssm, q, k, v, gates = _in_projection(
            x, norm1_g[layer], _pack_table(w_in, layer, transpose=False), cos, sin,
            pool_w=pool_w_cols, ssm_w=ssm_w_cols, attn_w=attn_w)
        pm = _pool_mixer(u_pool, pool_w[layer].astype(BF16), pool_scale[layer])
        abar_re, abar_im, bbar_re_t, bbar_im_t = _ssm_params(
            ssm_a_re[layer], ssm_a_im[layer], ssm_log_dt[layer], ssm_b_re[layer], ssm_b_im[layer])
        ys = _ssm_mixer(u_ssm, b, abar_re, abar_im, bbar_re_t, bbar_im_t,
                        ssm_c_re[layer], ssm_c_im[layer], ssm_d[layer])
        ao = _attention(q, k, v)
        packed = [_pack_table(wt, layer, transpose=False)
                  for wt in (pool_proj, ssm_glu, ssm_proj, attn_proj, w_out)]
        x, h2t = _merge(pm, ys, ao, gates, x, *packed, norm2_g[layer])
        r2, e2, n1, e1 = _peer_route(h2t, _pack_table(peer_wq, layer, transpose=True), peer_k1[layer].astype(BF16),
                                     peer_k2[layer].astype(BF16))
        x = _peer_experts(h2t, x.reshape(b * l, d), _pack_table(peer_u, layer, transpose=False),
                          _pack_table(peer_v, layer, transpose=True), r2, e2, n1, e1,
                          final_g if layer == depth - 1 else None).reshape(b, l, d)
    return x
```

```python
import functools
import math

import jax
import jax.numpy as jnp
from jax import lax
from jax.experimental import pallas as pl
from jax.experimental.pallas import tpu as pltpu

F32 = jnp.float32
BF16 = jnp.bfloat16

EPS = 1e-6
POOL_WINDOWS = (2, 4, 8, 16)
POOL_GROUP = 128
SSM_GROUP = 16
SSM_GROUPS = 32
SSM_STATE = 64
HEAD_DIM = 64
HEADS_PER_GROUP = 4
ATTN_DILATIONS = (1, 4, 16)
ATTN_BLOCK = 128
ROPE_THETA = 10000.0
PEER_HEADS = 8
PEER_TOPK = 16

LANES = 128
SUBLANES = 8
BF16_ROWS = 16
VMEM_LIMIT = 56 * 1024 * 1024


def _cparams(sem):
    return pltpu.CompilerParams(dimension_semantics=sem, vmem_limit_bytes=VMEM_LIMIT)


def _gelu(x):
    c = math.sqrt(2.0 / math.pi)
    return x * (0.5 * (1.0 + jnp.tanh(c * (x + 0.044715 * (x * x * x)))))


def _sigmoid(x):
    return 1.0 / (1.0 + jnp.exp(-x))


def _rmsnorm(x, g):
    ms = jnp.mean(x * x, axis=-1, keepdims=True)
    return x * lax.rsqrt(ms + EPS) * g


def _dot(a, b):
    return jnp.dot(a, b, preferred_element_type=F32)


def _dot_nt(a, b):
    return lax.dot_general(a, b, (((1,), (1,)), ((), ())), preferred_element_type=F32)


def _rope_kernel(pos_ref, invf_ref, sign_ref, cos_ref, sin_ref):
    ang = pos_ref[...].astype(F32) * invf_ref[...]
    cos_ref[...] = jnp.cos(ang)
    sin_ref[...] = jnp.sin(ang) * sign_ref[...]


def _rope_tables(positions):
    b, l = positions.shape
    t = b * l
    half = HEAD_DIM // 2
    inv_freq = ROPE_THETA ** (-jnp.arange(half, dtype=F32) / half)
    invf = jnp.tile(inv_freq, LANES // half).reshape(1, LANES)
    sign = jnp.tile(jnp.concatenate([-jnp.ones((half,), F32), jnp.ones((half,), F32)]),
                    LANES // HEAD_DIM).reshape(1, LANES)
    pos_b = jnp.broadcast_to(positions.reshape(t, 1), (t, LANES))
    tr = 2048
    cos, sin = pl.pallas_call(
        _rope_kernel,
        grid=(t // tr,),
        in_specs=[pl.BlockSpec((tr, LANES), lambda i: (i, 0)),
                  pl.BlockSpec((1, LANES), lambda i: (0, 0)),
                  pl.BlockSpec((1, LANES), lambda i: (0, 0))],
        out_specs=[pl.BlockSpec((tr, LANES), lambda i: (i, 0))] * 2,
        out_shape=[jax.ShapeDtypeStruct((t, LANES), F32)] * 2,
        compiler_params=_cparams(("parallel",)),
        name="rope_tables",
    )(pos_b, invf, sign)
    return cos.reshape(b, l, LANES), sin.reshape(b, l, LANES)


def _inproj_kernel(x_ref, g_ref, w_ref, cos_ref, sin_ref, pool_ref, ssm_ref, q_ref, k_ref, v_ref, gate_ref,
                   *, pool_w, ssm_w, attn_w, d_model):
    h = _rmsnorm(x_ref[0], g_ref[...]).astype(BF16)
    c0 = 0

    def w_cols(lo, hi):
        return pltpu.bitcast(w_ref[:, lo:hi], BF16)

    z = _dot(h, w_cols(c0, c0 + pool_w + ssm_w))
    pool_ref[0] = z[:, :pool_w]
    ssm_ref[...] = z[:, pool_w:]
    c0 += pool_w + ssm_w
    cos = cos_ref[0]
    sin = sin_ref[0]
    nslab = attn_w // LANES
    half = HEAD_DIM // 2
    first_half = lax.broadcasted_iota(jnp.int32, cos.shape, 1) % HEAD_DIM < half
    for dst in (q_ref, k_ref):
        z = _dot(h, w_cols(c0, c0 + attn_w))
        for s in range(nslab):
            a = z[:, s * LANES:(s + 1) * LANES]
            a_sw = jnp.where(first_half, pltpu.roll(a, LANES - half, axis=1), pltpu.roll(a, half, axis=1))
            dst[0, s] = a * cos + a_sw * sin
        c0 += attn_w
    z = _dot(h, w_cols(c0, c0 + attn_w))
    for s in range(nslab):
        v_ref[0, s] = z[:, s * LANES:(s + 1) * LANES]
    c0 += attn_w
    z = _dot(h, w_cols(c0, c0 + 3 * d_model))
    gate_ref[0] = pltpu.bitcast(_sigmoid(z).astype(BF16), jnp.uint32)


def _in_projection(x, g, w_packed, cos, sin, *, pool_w, ssm_w, attn_w):
    b, l, d = x.shape
    tm = 256
    nslab = attn_w // LANES
    kern = functools.partial(_inproj_kernel, pool_w=pool_w, ssm_w=ssm_w, attn_w=attn_w, d_model=d)
    row = lambda bi, i: (bi, i, 0)
    slab = lambda bi, i: (bi, 0, i, 0)
    const2 = lambda bi, i: (0, 0)
    in_specs = [pl.BlockSpec((1, tm, d), row),
                pl.BlockSpec((1, d), const2),
                pl.BlockSpec(w_packed.shape, const2, pipeline_mode=pl.Buffered(1)),
                pl.BlockSpec((1, tm, LANES), row),
                pl.BlockSpec((1, tm, LANES), row)]
    args = [x, g.reshape(1, d), w_packed, cos, sin]
    out_specs = [pl.BlockSpec((1, tm, pool_w), row),
                 pl.BlockSpec((tm, ssm_w), lambda bi, i: (i, bi)),
                 pl.BlockSpec((1, nslab, tm, LANES), slab),
                 pl.BlockSpec((1, nslab, tm, LANES), slab),
                 pl.BlockSpec((1, nslab, tm, LANES), slab),
                 pl.BlockSpec((1, tm // 2, 3 * d), row)]
    out_shape = [jax.ShapeDtypeStruct((b, l, pool_w), F32),
                 jax.ShapeDtypeStruct((l, b * ssm_w), F32),
                 jax.ShapeDtypeStruct((b, nslab, l, LANES), F32),
                 jax.ShapeDtypeStruct((b, nslab, l, LANES), F32),
                 jax.ShapeDtypeStruct((b, nslab, l, LANES), F32),
                 jax.ShapeDtypeStruct((b, l // 2, 3 * d), jnp.uint32)]
    return pl.pallas_call(
        kern, grid=(b, l // tm), in_specs=in_specs, out_specs=out_specs, out_shape=out_shape,
        compiler_params=_cparams(("parallel", "parallel")), name="in_projection",
    )(*args)


def _pool_kernel(u_ref, halo_ref, w_ref, scale_ref, o_ref, *, tm, halo):
    i = pl.program_id(1)
    cur = u_ref[0]
    prev = jnp.where(i > 0, halo_ref[0], 0.0)
    ext = jnp.concatenate([prev, cur], axis=0)
    t_idx = i * tm + lax.broadcasted_iota(jnp.int32, (tm, POOL_GROUP), 0)
    for gi, win in enumerate(POOL_WINDOWS):
        cs = slice(gi * POOL_GROUP, (gi + 1) * POOL_GROUP)
        s = ext[:, cs]
        sh = 1
        while sh < win:
            s = s + pltpu.roll(s, sh, axis=0)
            sh *= 2
        count = jnp.minimum(t_idx + 1, win).astype(F32)
        pooled = s[halo:, :] / count - cur[:, cs]
        mixed = _dot(pooled.astype(BF16), w_ref[gi]) * scale_ref[:, cs]
        o_ref[0, :, cs] = mixed.astype(BF16)


def _pool_mixer(u, w, scale):
    b, l, c = u.shape
    tm, halo = 512, 16
    assert halo >= max(POOL_WINDOWS) and halo % SUBLANES == 0
    kern = functools.partial(_pool_kernel, tm=tm, halo=halo)
    return pl.pallas_call(
        kern, grid=(b, l // tm),
        in_specs=[pl.BlockSpec((1, tm, c), lambda bi, i: (bi, i, 0)),
                  pl.BlockSpec((1, halo, c), lambda bi, i: (bi, jnp.maximum(i * (tm // halo) - 1, 0), 0)),
                  pl.BlockSpec(w.shape, lambda bi, i: (0, 0, 0)),
                  pl.BlockSpec((1, c), lambda bi, i: (0, 0))],
        out_specs=pl.BlockSpec((1, tm, c), lambda bi, i: (bi, i, 0)),
        out_shape=jax.ShapeDtypeStruct((b, l, c), BF16),
        compiler_params=_cparams(("parallel", "parallel")), name="pool_mixer",
    )(u, u, w, scale.reshape(1, c))


def _ssm_param_kernel(are_ref, aim_ref, ldt_ref, bre_ref, bim_ref, oar_ref, oai_ref, obr_ref, obi_ref):
    ar = are_ref[...]
    ai = aim_ref[...]
    dt = jnp.exp(ldt_ref[...])
    decay = jnp.exp(ar * dt)
    abar_re = decay * jnp.cos(ai * dt)
    abar_im = decay * jnp.sin(ai * dt)
    den = ar * ar + ai * ai
    nr = abar_re - 1.0
    k_re = (nr * ar + abar_im * ai) / den
    k_im = (abar_im * ar - nr * ai) / den
    oar_ref[...] = abar_re
    oai_ref[...] = abar_im
    for gi in range(are_ref.shape[0]):
        kr = k_re[gi:gi + 1, :]
        ki = k_im[gi:gi + 1, :]
        br = bre_ref[gi]
        bi = bim_ref[gi]
        obr_ref[gi] = kr * br - ki * bi
        obi_ref[gi] = kr * bi + ki * br


def _ssm_params(a_re, a_im, log_dt, b_re, b_im):
    g, n, p = b_re.shape
    brt = jnp.transpose(b_re, (0, 2, 1))
    bit = jnp.transpose(b_im, (0, 2, 1))
    return pl.pallas_call(
        _ssm_param_kernel,
        out_shape=[jax.ShapeDtypeStruct((g, n), F32)] * 2 + [jax.ShapeDtypeStruct((g, p, n), F32)] * 2,
        name="ssm_params",
    )(a_re, a_im, log_dt.reshape(g, 1), brt, bit)


def _block_diag(m):
    k, r, c = m.shape
    eye = jnp.eye(k, dtype=m.dtype)
    return (m[:, :, None, :] * eye[:, None, :, None]).reshape(k * r, k * c)


def _ssm_kernel(u_ref, bre_ref, bim_ref, are_ref, aim_ref, cre_ref, cim_ref, d_ref, y_ref,
                st_ref, xre_ref, xim_ref, sre_ref, sim_ref, uh_ref, yo_ref, *, tc, hw):
    nb = SUBLANES // 2
    hrows = tc * nb
    nslab = hw // LANES
    nstate = xre_ref.shape[0]

    @pl.when(pl.program_id(0) == 0)
    def _():
        st_ref[...] = jnp.zeros_like(st_ref)

    def seq_lanes(b, h, hf):
        s = b * 2 + h
        return slice(s * hw + hf * LANES, s * hw + (hf + 1) * LANES)

    for b in range(nb):
        for h in range(2):
            for hf in range(nslab):
                uh_ref[h, hf, pl.ds(b, tc, stride=nb), :] = u_ref[:, seq_lanes(b, h, hf)]
    for h in range(2):
        ub = jnp.concatenate([uh_ref[h, hf] for hf in range(nslab)], axis=1).astype(BF16)
        x_re = _dot(ub, bre_ref[h])
        x_im = _dot(ub, bim_ref[h])
        for j in range(nstate):
            xre_ref[j, pl.ds(h, hrows, stride=2), :] = x_re[:, j * LANES:(j + 1) * LANES]
            xim_ref[j, pl.ds(h, hrows, stride=2), :] = x_im[:, j * LANES:(j + 1) * LANES]
    a_re = are_ref[...]
    a_im = aim_ref[...]

    def step(t, carry):
        s_re, s_im = carry
        rows = pl.ds(pl.multiple_of(t * SUBLANES, SUBLANES), SUBLANES)
        n_re = a_re * s_re - a_im * s_im + xre_ref[:, rows, :]
        n_im = a_re * s_im + a_im * s_re + xim_ref[:, rows, :]
        sre_ref[:, rows, :] = n_re
        sim_ref[:, rows, :] = n_im
        return n_re, n_im

    s_re, s_im = lax.fori_loop(0, tc, step, (st_ref[0], st_ref[1]), unroll=8)
    st_ref[0] = s_re
    st_ref[1] = s_im
    for h in range(2):
        s_r = jnp.concatenate([sre_ref[j, pl.ds(h, hrows, stride=2), :] for j in range(nstate)], axis=1)
        s_i = jnp.concatenate([sim_ref[j, pl.ds(h, hrows, stride=2), :] for j in range(nstate)], axis=1)
        u_h = jnp.concatenate([uh_ref[h, hf] for hf in range(nslab)], axis=1)
        y = _gelu(_dot(s_r.astype(BF16), cre_ref[h]) - _dot(s_i.astype(BF16), cim_ref[h]) + d_ref[h] * u_h)
        for hf in range(nslab):
            yo_ref[h, hf] = y[:, hf * LANES:(hf + 1) * LANES]
    for b in range(nb):
        for h in range(2):
            for hf in range(nslab):
                y_ref[:, seq_lanes(b, h, hf)] = yo_ref[h, hf, pl.ds(b, tc, stride=nb), :].astype(BF16)


def _ssm_mixer(u_scan, batch, abar_re, abar_im, bbar_re_t, bbar_im_t, c_re, c_im, d_skip):
    l = u_scan.shape[0]
    g, p, n = bbar_re_t.shape
    hg = g // 2
    hw = hg * p
    hs = hg * n
    assert batch * 2 == SUBLANES

    nstate = hs // LANES

    def half_b(bt):
        return jnp.stack([_block_diag(bt[:hg]), _block_diag(bt[hg:])]).astype(BF16)

    def half_c(c):
        ct = jnp.transpose(c, (0, 2, 1))
        return jnp.stack([_block_diag(ct[:hg]), _block_diag(ct[hg:])]).astype(BF16)

    def tile_a(a):
        a = jnp.transpose(a.reshape(2, nstate, LANES), (1, 0, 2))
        return jnp.broadcast_to(a[:, None], (nstate, batch, 2, LANES)).reshape(nstate, SUBLANES, LANES)

    tc = 64
    rows = tc * SUBLANES
    kern = functools.partial(_ssm_kernel, tc=tc, hw=hw)
    const = lambda i: (0, 0, 0)
    state = pltpu.VMEM((nstate, rows, LANES), F32)
    halves = pltpu.VMEM((2, hw // LANES, rows // 2, LANES), F32)
    return pl.pallas_call(
        kern, grid=(l // tc,),
        in_specs=[pl.BlockSpec((tc, SUBLANES * hw), lambda i: (i, 0)),
                  pl.BlockSpec((2, hw, hs), const), pl.BlockSpec((2, hw, hs), const),
                  pl.BlockSpec((nstate, SUBLANES, LANES), const), pl.BlockSpec((nstate, SUBLANES, LANES), const),
                  pl.BlockSpec((2, hs, hw), const), pl.BlockSpec((2, hs, hw), const),
                  pl.BlockSpec((2, 1, hw), const)],
        out_specs=pl.BlockSpec((tc, SUBLANES * hw), lambda i: (i, 0)),
        out_shape=jax.ShapeDtypeStruct((l, SUBLANES * hw), BF16),
        scratch_shapes=[pltpu.VMEM((2, nstate, SUBLANES, LANES), F32), state, state, state, state, halves, halves],
        compiler_params=_cparams(("arbitrary",)), name="ssm_mixer",
    )(u_scan, half_b(bbar_re_t), half_b(bbar_im_t), tile_a(abar_re), tile_a(abar_im),
      half_c(c_re), half_c(c_im), d_skip.reshape(2, 1, hw))


def _attn_group(q_ref, k_ref, v_ref, kb, vb, acc, mrun, drun, og, mg, lg, *, d, ta, i):
    p = ATTN_BLOCK * d
    ncombo = ta // ATTN_BLOCK
    nslab = q_ref.shape[1]
    scale = HEAD_DIM ** -0.5
    kb[:, p:p + ta, :] = k_ref[0]
    vb[:, p:p + ta, :] = v_ref[0]

    @pl.when(i == 0)
    def _():
        kb[:, :p, :] = jnp.zeros((nslab, p, LANES), F32)
        vb[:, :p, :] = jnp.zeros((nslab, p, LANES), F32)

    nh = LANES // HEAD_DIM
    qi = lax.broadcasted_iota(jnp.int32, (nh * ATTN_BLOCK, ATTN_BLOCK), 0) % ATTN_BLOCK
    kk = lax.broadcasted_iota(jnp.int32, (nh * ATTN_BLOCK, ATTN_BLOCK), 1)
    mask_cur = kk <= qi
    mask_prev = kk >= qi
    low = lax.broadcasted_iota(jnp.int32, (ATTN_BLOCK, ATTN_BLOCK), 1) < HEAD_DIM

    def rows_at(start):
        return pl.ds(start, ATTN_BLOCK) if d == 1 else pl.ds(start, ATTN_BLOCK, stride=d)

    def combo(c, carry):
        r = c % d
        n = c // d
        qs = n * p + r
        has_prev = jnp.logical_or(i > 0, n > 0)
        mask_p = jnp.logical_and(mask_prev, has_prev)
        for s in range(nslab):
            q = q_ref[0, s, rows_at(qs), :] * scale
            kc = kb[s, rows_at(p + qs), :].astype(BF16)
            kp = kb[s, rows_at(qs), :].astype(BF16)
            vc = vb[s, rows_at(p + qs), :].astype(BF16)
            vp = vb[s, rows_at(qs), :].astype(BF16)
            qh = jnp.concatenate([jnp.where(low, q, 0.0), jnp.where(low, 0.0, q)], axis=0).astype(BF16)
            sc = jnp.where(mask_cur, _dot_nt(qh, kc), -jnp.inf)
            sp = jnp.where(mask_p, _dot_nt(qh, kp), -jnp.inf)
            m = jnp.max(jnp.maximum(sc, sp), axis=1, keepdims=True)
            pc = jnp.exp(sc - m)
            pp = jnp.exp(sp - m)
            den = jnp.sum(pc + pp, axis=1, keepdims=True)
            o = _dot(pc.astype(BF16), vc) + _dot(pp.astype(BF16), vp)
            rows = rows_at(qs)
            og[s, rows, :] = jnp.where(low, o[:ATTN_BLOCK], o[ATTN_BLOCK:])
            mg[s, rows, :] = jnp.where(low, m[:ATTN_BLOCK], m[ATTN_BLOCK:])
            lg[s, rows, :] = jnp.where(low, den[:ATTN_BLOCK], den[ATTN_BLOCK:])
        return carry

    lax.fori_loop(0, ncombo, combo, 0, unroll=4 if d < max(ATTN_DILATIONS) else 2)
    kb[:, :p, :] = kb[:, ta:ta + p, :]
    vb[:, :p, :] = vb[:, ta:ta + p, :]

    rc = 256
    for s in range(nslab):
        for r0 in range(0, ta, rc):
            rs = slice(r0, r0 + rc)
            m_old = mrun[s, rs, :]
            m_blk = mg[s, rs, :]
            m_new = jnp.maximum(m_old, m_blk)
            a_old = jnp.exp(m_old - m_new)
            a_blk = jnp.exp(m_blk - m_new)
            acc[s, rs, :] = acc[s, rs, :] * a_old + og[s, rs, :] * a_blk
            drun[s, rs, :] = drun[s, rs, :] * a_old + lg[s, rs, :] * a_blk
            mrun[s, rs, :] = m_new


def _attn_kernel(q_ref, k_ref, v_ref, o_ref, *scratch, ta):
    ng = len(ATTN_DILATIONS)
    kbs = scratch[0:2 * ng:2]
    vbs = scratch[1:2 * ng:2]
    acc, mrun, drun, og, mg, lg = scratch[2 * ng:]
    i = pl.program_id(1)
    g = pl.program_id(2)

    @pl.when(g == 0)
    def _():
        acc[...] = jnp.zeros_like(acc)
        drun[...] = jnp.zeros_like(drun)
        mrun[...] = jnp.full(mrun.shape, -jnp.inf, F32)

    for gi, d in enumerate(ATTN_DILATIONS):
        @pl.when(g == gi)
        def _(gi=gi, d=d):
            _attn_group(q_ref, k_ref, v_ref, kbs[gi], vbs[gi], acc, mrun, drun, og, mg, lg, d=d, ta=ta, i=i)

    @pl.when(g == ng - 1)
    def _():
        for s in range(acc.shape[0]):
            o_ref[0, :, s * LANES:(s + 1) * LANES] = acc[s] / drun[s]


def _attention(q, k, v):
    b, nslab, l, _ = q.shape
    ng = len(ATTN_DILATIONS)
    gs = nslab // ng
    ta = ATTN_BLOCK * max(ATTN_DILATIONS)
    assert l % ta == 0
    blk = pl.BlockSpec((1, gs, ta, LANES), lambda bi, i, g: (bi, g, i, 0))
    scratch = []
    for d in ATTN_DILATIONS:
        scratch += [pltpu.VMEM((gs, ATTN_BLOCK * d + ta, LANES), F32)] * 2
    scratch += [pltpu.VMEM((gs, ta, LANES), F32)] * 6
    return pl.pallas_call(
        functools.partial(_attn_kernel, ta=ta), grid=(b, l // ta, ng),
        in_specs=[blk, blk, blk],
        out_specs=pl.BlockSpec((1, ta, gs * LANES), lambda bi, i, g: (bi, i, 0)),
        out_shape=jax.ShapeDtypeStruct((b, l, gs * LANES), F32),
        scratch_shapes=scratch,
        compiler_params=_cparams(("arbitrary", "arbitrary", "arbitrary")), name="dilated_attention",
    )(q, k, v)


def _merge_kernel(pm_ref, ys_ref, ao_ref, gate_ref, x_ref, wp_ref, wg_ref, ws_ref, wa_ref, wo_ref, g2_ref,
                  xo_ref, h2_ref, *, d_model, ssm_w):
    def weight(ref):
        return pltpu.bitcast(ref[...], BF16)

    y_pool = _dot(pm_ref[0], weight(wp_ref))
    glu = _dot(ys_ref[...], weight(wg_ref))
    sg = (glu[:, :ssm_w] * _sigmoid(glu[:, ssm_w:])).astype(BF16)
    y_ssm = _dot(sg, weight(ws_ref))
    y_attn = _dot(ao_ref[0].astype(BF16), weight(wa_ref))
    gate = pltpu.bitcast(gate_ref[0], BF16).astype(F32)
    merged = (gate[:, :d_model] * y_pool + gate[:, d_model:2 * d_model] * y_ssm
              + gate[:, 2 * d_model:] * y_attn)
    xn = x_ref[0] + _dot(merged.astype(BF16), weight(wo_ref))
    xo_ref[0] = xn
    h2_ref[...] = _rmsnorm(xn, g2_ref[...]).T.astype(BF16)


def _merge(pm, ys, ao, gates, x, w_pool, w_glu, w_ssm, w_attn, w_out, g2):
    b, l, d = x.shape
    tm = 512
    ssm_w = 2 * w_glu.shape[0]
    row = lambda bi, i: (bi, i, 0)
    const = lambda bi, i: (0, 0)
    full = lambda a: pl.BlockSpec(a.shape, const)
    return pl.pallas_call(
        functools.partial(_merge_kernel, d_model=d, ssm_w=ssm_w), grid=(b, l // tm),
        in_specs=[pl.BlockSpec((1, tm, pm.shape[-1]), row),
                  pl.BlockSpec((tm, ssm_w), lambda bi, i: (i, bi)),
                  pl.BlockSpec((1, tm, ao.shape[-1]), row),
                  pl.BlockSpec((1, tm // 2, 3 * d), row),
                  pl.BlockSpec((1, tm, d), row),
                  full(w_pool), full(w_glu), full(w_ssm), full(w_attn), full(w_out),
                  pl.BlockSpec((1, d), const)],
        out_specs=[pl.BlockSpec((1, tm, d), row),
                   pl.BlockSpec((d, tm), lambda bi, i: (0, bi * (l // tm) + i))],
        out_shape=[jax.ShapeDtypeStruct((b, l, d), F32), jax.ShapeDtypeStruct((d, b * l), BF16)],
        compiler_params=_cparams(("parallel", "parallel")), name="branch_merge",
    )(pm, ys, ao, gates, x, w_pool, w_glu, w_ssm, w_attn, w_out, g2.reshape(1, d))


def _candidate_pairs(k):
    return [(a, b) for a in range(k) for b in range(k) if (a + 1) * (b + 1) <= k]


_CODE_UNIT = 2.0 ** 121
_CODE_BASE = 80


def _removed_code(r):
    return -float(_CODE_BASE + r) * _CODE_UNIT


def _removed_round(w):
    return w * (-1.0 / _CODE_UNIT) - float(_CODE_BASE)


def _vreg_rows(x):
    return x.reshape(x.shape[0] // SUBLANES, SUBLANES, x.shape[1])


def _max_all_rows(w):
    p = jnp.max(w, axis=0)
    for sh in (4, 2, 1):
        p = jnp.maximum(p, pltpu.roll(p, sh, axis=0))
    return p


def _topk_round(work, rows):
    m = jnp.max(work, axis=0, keepdims=True)
    idx = jnp.min(jnp.where(work == m, rows, float(work.shape[0])), axis=0, keepdims=True)
    hit = rows == idx
    return m, hit, jnp.where(hit, -jnp.inf, work)


def _route_kernel(h_ref, wq_ref, k1_ref, k2_ref, r2_ref, e2_ref, n1_ref, e1_ref,
                  qt_ref, s1_ref, s2_ref, v1_ref, v2_ref, cand_ref, sel_ref, tied_ref,
                  *, nheads, nkeys, topk, tt):
    qt_ref[...] = _dot(pltpu.bitcast(wq_ref[...], BF16), h_ref[...])
    hc = k1_ref.shape[1]
    pairs = _candidate_pairs(topk)
    ncand = cand_ref.shape[1]

    def head(hd, carry):
        base = pl.multiple_of(hd * 2 * hc, 2 * hc)
        q1 = qt_ref[pl.ds(base, hc), :].astype(BF16)
        q2 = qt_ref[pl.ds(base + hc, hc), :].astype(BF16)
        s1_ref[...] = _dot(k1_ref[...], q1)
        s2_ref[...] = _dot(k2_ref[...], q2)

        def column(j, exact):
            if isinstance(j, int):
                cols = slice(j * LANES, (j + 1) * LANES)
            else:
                cols = pl.ds(pl.multiple_of(j * LANES, LANES), LANES)
            s1 = s1_ref[:, cols]
            s2 = s2_ref[:, cols]
            v1, v2, cand, selr = v1_ref.at[j], v2_ref.at[j], cand_ref.at[j], sel_ref.at[j]

            def build_candidates():
                for ci, (a, bq) in enumerate(pairs):
                    cand[ci:ci + 1, :] = v1[a:a + 1, :] + v2[bq:bq + 1, :]
                if ncand > len(pairs):
                    cand[len(pairs):, :] = jnp.full((ncand - len(pairs), LANES), -jnp.inf, F32)
                return cand[...]

            def finish(rank1, rank2, sel, cv):
                selr[...] = sel
                top = v1[0:1, :] + v2[0:1, :]
                z = jnp.sum(sel * jnp.exp(jnp.where(sel > 0.0, cv, top) - top), axis=0, keepdims=True)
                n1 = jnp.zeros(s1.shape, F32)
                off = 0
                for a in range(topk):
                    cnt = topk // (a + 1)
                    n_a = jnp.sum(selr[off:off + cnt, :], axis=0, keepdims=True)
                    n1 = jnp.where(rank1 == float(a), n_a, n1)
                    off += cnt
                n1_ref[hd, :, cols] = n1
                e1_ref[hd, :, cols] = jnp.exp(s1 - v1[0:1, :]) / z
                r2_ref[hd, j] = pltpu.bitcast(rank2.astype(BF16), jnp.uint32)
                e2_ref[hd, j] = pltpu.bitcast(jnp.exp(s2 - v2[0:1, :]).astype(BF16), jnp.uint32)

            if exact:
                key_rows = lax.broadcasted_iota(jnp.int32, s1.shape, 0).astype(F32)
                x1, x2 = s1, s2
                rk1 = jnp.full(s1.shape, float(nkeys), F32)
                rk2 = rk1
                for r in range(topk):
                    m1, hit1, x1 = _topk_round(x1, key_rows)
                    m2, hit2, x2 = _topk_round(x2, key_rows)
                    rk1 = jnp.where(hit1, float(r), rk1)
                    rk2 = jnp.where(hit2, float(r), rk2)
                    v1[r:r + 1, :] = m1
                    v2[r:r + 1, :] = m2
                cvs = build_candidates()
                cand_rows = lax.broadcasted_iota(jnp.int32, cvs.shape, 0).astype(F32)
                xc = cvs
                sl = jnp.zeros(cvs.shape, F32)
                for r in range(topk):
                    _, hit, xc = _topk_round(xc, cand_rows)
                    sl = jnp.where(hit, 1.0, sl)
                finish(rk1, rk2, sl, cvs)
                return None

            w1, w2 = _vreg_rows(s1), _vreg_rows(s2)
            for r in range(topk):
                m1 = _max_all_rows(w1)
                m2 = _max_all_rows(w2)
                w1 = jnp.where(w1 == m1[None], _removed_code(r), w1)
                w2 = jnp.where(w2 == m2[None], _removed_code(r), w2)
                v1[r:r + 1, :] = m1[0:1, :]
                v2[r:r + 1, :] = m2[0:1, :]
            w1, w2 = w1.reshape(s1.shape), w2.reshape(s2.shape)
            gone1 = w1 <= _removed_code(0)
            gone2 = w2 <= _removed_code(0)
            rank1 = jnp.where(gone1, _removed_round(w1), float(nkeys))
            rank2 = jnp.where(gone2, _removed_round(w2), float(nkeys))
            cv = build_candidates()
            wc = _vreg_rows(cv)
            for r in range(topk):
                wc = jnp.where(wc == _max_all_rows(wc)[None], _removed_code(r), wc)
            sel = jnp.where(wc.reshape(cv.shape) <= _removed_code(0), 1.0, 0.0)
            removed = (jnp.sum(jnp.where(gone1, 1.0, 0.0), axis=0, keepdims=True)
                       + jnp.sum(jnp.where(gone2, 1.0, 0.0), axis=0, keepdims=True)
                       + jnp.sum(sel, axis=0, keepdims=True))
            finish(rank1, rank2, sel, cv)
            expected = float(3 * topk + ncand - len(pairs))
            return jnp.max(jnp.where(removed != expected, 1.0, 0.0))

        for j in range(tt // LANES):
            tied_ref[j] = column(j, exact=False)

        def redo(j, carry2):
            @pl.when(tied_ref[j] > 0.0)
            def _():
                column(j, exact=True)
            return carry2

        lax.fori_loop(0, tt // LANES, redo, 0)
        return carry

    lax.fori_loop(0, nheads, head, 0)


def _peer_route(h2t, wq_t, k1, k2):
    d, t = h2t.shape
    nkeys, hc = k1.shape
    qrows = 2 * wq_t.shape[0]
    nheads = qrows // (2 * hc)
    tt = 1024
    npairs = len(_candidate_pairs(PEER_TOPK))
    ncand = -(-npairs // SUBLANES) * SUBLANES
    ncol = tt // LANES
    kern = functools.partial(_route_kernel, nheads=nheads, nkeys=nkeys, topk=PEER_TOPK, tt=tt)
    kblk = pl.BlockSpec((nheads, tt // LANES, nkeys // 2, LANES), lambda i: (0, i, 0, 0))
    ksh = jax.ShapeDtypeStruct((nheads, t // LANES, nkeys // 2, LANES), jnp.uint32)
    oblk = pl.BlockSpec((nheads, nkeys, tt), lambda i: (0, 0, i))
    osh = jax.ShapeDtypeStruct((nheads, nkeys, t), F32)
    return pl.pallas_call(
        kern, grid=(t // tt,),
        in_specs=[pl.BlockSpec((d, tt), lambda i: (0, i)),
                  pl.BlockSpec(wq_t.shape, lambda i: (0, 0)),
                  pl.BlockSpec(k1.shape, lambda i: (0, 0)),
                  pl.BlockSpec(k2.shape, lambda i: (0, 0))],
        out_specs=[kblk, kblk, oblk, oblk], out_shape=[ksh, ksh, osh, osh],
        scratch_shapes=[pltpu.VMEM((qrows, tt), F32),
                        pltpu.VMEM((nkeys, tt), F32), pltpu.VMEM((nkeys, tt), F32),
                        pltpu.VMEM((ncol, PEER_TOPK, LANES), F32), pltpu.VMEM((ncol, PEER_TOPK, LANES), F32),
                        pltpu.VMEM((ncol, ncand, LANES), F32), pltpu.VMEM((ncol, ncand, LANES), F32),
                        pltpu.SMEM((ncol,), F32)],
        compiler_params=_cparams(("parallel",)), name="peer_route",
    )(h2t, wq_t, k1, k2)


def _pack_kernel(x_ref, o_ref, *, transpose):
    x = x_ref[0]
    if transpose:
        x = x.T
    o_ref[...] = pltpu.bitcast(x.astype(BF16), jnp.uint32)


def _pack_table(stacked, layer, *, transpose):
    _, r, c = stacked.shape
    tb = min(r, 512 if c <= 1024 else 128)
    in_spec = pl.BlockSpec((1, tb, c), lambda i: (layer, i, 0))
    if transpose:
        out_spec = pl.BlockSpec((c // 2, tb), lambda i: (0, i))
        out_shape = jax.ShapeDtypeStruct((c // 2, r), jnp.uint32)
    else:
        out_spec = pl.BlockSpec((tb // 2, c), lambda i: (i, 0))
        out_shape = jax.ShapeDtypeStruct((r // 2, c), jnp.uint32)
    return pl.pallas_call(
        functools.partial(_pack_kernel, transpose=transpose), grid=(r // tb,),
        in_specs=[in_spec], out_specs=out_spec, out_shape=out_shape,
        compiler_params=_cparams(("parallel",)), name="pack_table",
    )(stacked)


def _ffn_kernel(h_ref, x_ref, g_ref, u_ref, un_ref, vt_ref, r2_ref, e2_ref, n1_ref, e1_ref, o_ref,
                acc_ref, at_ref, pt_ref, *, nheads, nkeys, cw, final_norm):
    j = pl.program_id(1)
    tt = h_ref.shape[1]
    et = 2 * u_ref.shape[0]
    nhalf = cw // LANES

    def first_pre_activations(table_ref):
        at_ref[0] = _dot(pltpu.bitcast(table_ref[0:at_ref.shape[1] // 2, :], BF16), h_ref[:, 0:cw])

    @pl.when(j == 0)
    def _():
        acc_ref[...] = jnp.zeros_like(acc_ref)
        first_pre_activations(u_ref)

    ktiles = nkeys // BF16_ROWS
    es = at_ref.shape[1]
    nsub = es // nkeys
    units = [(eh, c) for eh in range(et // es) for c in range(tt // cw)]
    n_at, n_pt = at_ref.shape[0], pt_ref.shape[0]

    def pre_activations(k):
        eh, c = units[k]
        u = pltpu.bitcast(u_ref[eh * es // 2:(eh + 1) * es // 2, :], BF16)
        at_ref[k % n_at] = _dot(u, h_ref[:, c * cw:(c + 1) * cw])

    assert n_at >= len(units)
    for k, (eh, c) in enumerate(units):
        if k + 1 < len(units):
            pre_activations(k + 1)
        else:
            first_pre_activations(un_ref)
        for jj in range(nsub):
            row = eh * nsub + jj
            for hf in range(nhalf):
                ck = c * nhalf + hf
                cols = slice(ck * LANES, (ck + 1) * LANES)
                lanes = slice(hf * LANES, (hf + 1) * LANES)
                gate = jnp.zeros((ktiles, BF16_ROWS, LANES), BF16)
                for hd in range(nheads):
                    n_row = jnp.broadcast_to(n1_ref[hd, row:row + 1, cols], (BF16_ROWS, LANES)).astype(BF16)
                    e_row = jnp.broadcast_to(e1_ref[hd, row:row + 1, cols], (BF16_ROWS, LANES)).astype(BF16)
                    r2 = pltpu.bitcast(r2_ref[hd, ck], BF16).reshape(ktiles, BF16_ROWS, LANES)
                    e2 = pltpu.bitcast(e2_ref[hd, ck], BF16).reshape(ktiles, BF16_ROWS, LANES)
                    gate = gate + jnp.where(r2 < n_row[None], e2, 0) * e_row[None]
                a = at_ref[k % n_at, jj * nkeys:(jj + 1) * nkeys, lanes].astype(BF16)
                p = (gate * _gelu(a.reshape(ktiles, BF16_ROWS, LANES))).reshape(nkeys, LANES)
                pt_ref[k % n_pt, jj * nkeys // 2:(jj + 1) * nkeys // 2, lanes] = pltpu.bitcast(p, jnp.uint32)
        p = pltpu.bitcast(pt_ref[k % n_pt], BF16)
        vt = pltpu.bitcast(vt_ref[:, eh * es:(eh + 1) * es], BF16)
        acc_ref[c] += _dot(vt, p)

    @pl.when(j == pl.num_programs(1) - 1)
    def _():
        for c in range(tt // cw):
            rows = slice(c * cw, (c + 1) * cw)
            y = x_ref[rows, :] + acc_ref[c].T
            o_ref[rows, :] = _rmsnorm(y, g_ref[...]) if final_norm else y


def _peer_experts(h2t, x, u_tab, v_tab_t, r2, e2, n1, e1, final_g):
    d, t = h2t.shape
    ne = 2 * u_tab.shape[0]
    nheads, nkeys, _ = n1.shape
    tt, et, cw = 1024, 1024, 256
    es = nkeys * SUBLANES
    at_ring, pt_ring = 4, 4
    final_norm = final_g is not None
    g = (final_g if final_norm else jnp.ones((d,), F32)).reshape(1, d)
    kern = functools.partial(_ffn_kernel, nheads=nheads, nkeys=nkeys, cw=cw, final_norm=final_norm)
    full_keys = pl.BlockSpec((nheads, tt // LANES, nkeys // 2, LANES), lambda i, j: (0, i, 0, 0))
    sub_keys = pl.BlockSpec((nheads, et // nkeys, tt), lambda i, j: (0, j, i))
    return pl.pallas_call(
        kern, grid=(t // tt, ne // et),
        in_specs=[pl.BlockSpec((d, tt), lambda i, j: (0, i)),
                  pl.BlockSpec((tt, d), lambda i, j: (i, 0)),
                  pl.BlockSpec((1, d), lambda i, j: (0, 0)),
                  pl.BlockSpec((et // 2, d), lambda i, j: (j, 0)),
                  pl.BlockSpec((es // 2, d), lambda i, j: (jnp.minimum(j + 1, ne // et - 1) * (et // es), 0)),
                  pl.BlockSpec((d // 2, et), lambda i, j: (0, j)),
                  full_keys, full_keys, sub_keys, sub_keys],
        out_specs=pl.BlockSpec((tt, d), lambda i, j: (i, 0)),
        out_shape=jax.ShapeDtypeStruct((t, d), F32),
        scratch_shapes=[pltpu.VMEM((tt // cw, d, cw), F32),
                        pltpu.VMEM((at_ring, es, cw), F32),
                        pltpu.VMEM((pt_ring, es // 2, cw), jnp.uint32)],
        compiler_params=_cparams(("parallel", "arbitrary")), name="peer_experts",
    )(h2t, x, g, u_tab, u_tab, v_tab_t, r2, e2, n1, e1)


def kernel(x, positions, norm1_g, w_in, pool_w, pool_scale, pool_proj, ssm_a_re, ssm_a_im, ssm_log_dt,
           ssm_b_re, ssm_b_im, ssm_c_re, ssm_c_im, ssm_d, ssm_glu, ssm_proj, attn_proj, w_out, norm2_g,
           peer_wq, peer_k1, peer_k2, peer_u, peer_v, final_g):
    b, l, d = x.shape
    depth = w_in.shape[0]
    pool_w_cols = pool_proj.shape[1]
    ssm_w_cols = ssm_proj.shape[1]
    attn_w = len(ATTN_DILATIONS) * HEADS_PER_GROUP * HEAD_DIM
    cos, sin = _rope_tables(positions)
    for layer in range(depth):
        u_pool, u_ssm, q, k, v, gates = _in_projection(
            x, norm1_g[layer], _pack_table(w_in, layer, transpose=False), cos, sin,
            pool_w=pool_w_cols, ssm_w=ssm_w_cols, attn_w=attn_w)
        pm = _pool_mixer(u_pool, pool_w[layer].astype(BF16), pool_scale[layer])
        abar_re, abar_im, bbar_re_t, bbar_im_t = _ssm_params(
            ssm_a_re[layer], ssm_a_im[layer], ssm_log_dt[layer], ssm_b_re[layer], ssm_b_im[layer])
        ys = _ssm_mixer(u_ssm, b, abar_re, abar_im, bbar_re_t, bbar_im_t,
                        ssm_c_re[layer], ssm_c_im[layer], ssm_d[layer])
        ao = _attention(q, k, v)
        packed = [_pack_table(wt, layer, transpose=False)
                  for wt in (pool_proj, ssm_glu, ssm_proj, attn_proj, w_out)]
        x, h2t = _merge(pm, ys, ao, gates, x, *packed, norm2_g[layer])
        r2, e2, n1, e1 = _peer_route(h2t, _pack_table(peer_wq, layer, transpose=True), peer_k1[layer].astype(BF16),
                                     peer_k2[layer].astype(BF16))
        x = _peer_experts(h2t, x.reshape(b * l, d), _pack_table(peer_u, layer, transpose=False),
                          _pack_table(peer_v, layer, transpose=True), r2, e2, n1, e1,
                          final_g if layer == depth - 1 else None).reshape(b, l, d)
    return x
```

```python
import functools
import math

import jax
import jax.numpy as jnp
from jax import lax
from jax.experimental import pallas as pl
from jax.experimental.pallas import tpu as pltpu

F32 = jnp.float32
BF16 = jnp.bfloat16

EPS = 1e-6
POOL_WINDOWS = (2, 4, 8, 16)
POOL_GROUP = 128
SSM_GROUP = 16
SSM_GROUPS = 32
SSM_STATE = 64
HEAD_DIM = 64
HEADS_PER_GROUP = 4
ATTN_DILATIONS = (1, 4, 16)
ATTN_BLOCK = 128
ROPE_THETA = 10000.0
PEER_HEADS = 8
PEER_TOPK = 16

LANES = 128
SUBLANES = 8
BF16_ROWS = 16
VMEM_LIMIT = 56 * 1024 * 1024


def _cparams(sem):
    return pltpu.CompilerParams(dimension_semantics=sem, vmem_limit_bytes=VMEM_LIMIT)


def _gelu(x):
    c = math.sqrt(2.0 / math.pi)
    return x * (0.5 * (1.0 + jnp.tanh(c * (x + 0.044715 * (x * x * x)))))


def _sigmoid(x):
    return 1.0 / (1.0 + jnp.exp(-x))


def _rmsnorm(x, g):
    ms = jnp.mean(x * x, axis=-1, keepdims=True)
    return x * lax.rsqrt(ms + EPS) * g


def _dot(a, b):
    return jnp.dot(a, b, preferred_element_type=F32)


def _dot_nt(a, b):
    return lax.dot_general(a, b, (((1,), (1,)), ((), ())), preferred_element_type=F32)


def _rope_kernel(pos_ref, invf_ref, sign_ref, cos_ref, sin_ref):
    ang = pos_ref[...].astype(F32) * invf_ref[...]
    cos_ref[...] = jnp.cos(ang)
    sin_ref[...] = jnp.sin(ang) * sign_ref[...]


def _rope_tables(positions):
    b, l = positions.shape
    t = b * l
    half = HEAD_DIM // 2
    inv_freq = ROPE_THETA ** (-jnp.arange(half, dtype=F32) / half)
    invf = jnp.tile(inv_freq, LANES // half).reshape(1, LANES)
    sign = jnp.tile(jnp.concatenate([-jnp.ones((half,), F32), jnp.ones((half,), F32)]),
                    LANES // HEAD_DIM).reshape(1, LANES)
    pos_b = jnp.broadcast_to(positions.reshape(t, 1), (t, LANES))
    tr = 2048
    cos, sin = pl.pallas_call(
        _rope_kernel,
        grid=(t // tr,),
        in_specs=[pl.BlockSpec((tr, LANES), lambda i: (i, 0)),
                  pl.BlockSpec((1, LANES), lambda i: (0, 0)),
                  pl.BlockSpec((1, LANES), lambda i: (0, 0))],
        out_specs=[pl.BlockSpec((tr, LANES), lambda i: (i, 0))] * 2,
        out_shape=[jax.ShapeDtypeStruct((t, LANES), F32)] * 2,
        compiler_params=_cparams(("parallel",)),
        name="rope_tables",
    )(pos_b, invf, sign)
    return cos.reshape(b, l, LANES), sin.reshape(b, l, LANES)


def _pool_mix(cur, prev, t0, w_ref, scale_ref, o_ref):
    tm, halo = cur.shape[0], prev.shape[0]
    ext = jnp.concatenate([prev, cur], axis=0)
    t_idx = t0 + lax.broadcasted_iota(jnp.int32, (tm, POOL_GROUP), 0)
    for gi, win in enumerate(POOL_WINDOWS):
        cs = slice(gi * POOL_GROUP, (gi + 1) * POOL_GROUP)
        s = ext[:, cs]
        sh = 1
        while sh < win:
            s = s + pltpu.roll(s, sh, axis=0)
            sh *= 2
        count = jnp.minimum(t_idx + 1, win).astype(F32)
        pooled = s[halo:, :] / count - cur[:, cs]
        mixed = _dot(pooled.astype(BF16), w_ref[gi]) * scale_ref[:, cs]
        o_ref[0, :, cs] = mixed.astype(BF16)


def _inproj_kernel(x_ref, g_ref, w_ref, cos_ref, sin_ref, pw_ref, ps_ref,
                   pool_ref, ssm_ref, q_ref, k_ref, v_ref, gate_ref, halo_ref, *, pool_w, ssm_w, attn_w, d_model):
    i = pl.program_id(1)
    tm = x_ref.shape[1]
    h = _rmsnorm(x_ref[0], g_ref[...]).astype(BF16)
    c0 = 0

    def w_cols(lo, hi):
        return pltpu.bitcast(w_ref[:, lo:hi], BF16)

    @pl.when(i == 0)
    def _():
        halo_ref[...] = jnp.zeros_like(halo_ref)

    z = _dot(h, w_cols(c0, c0 + pool_w + ssm_w))
    u_pool = z[:, :pool_w]
    _pool_mix(u_pool, halo_ref[...], i * tm, pw_ref, ps_ref, pool_ref)
    halo_ref[...] = u_pool[tm - halo_ref.shape[0]:, :]
    ssm_ref[...] = z[:, pool_w:]
    c0 += pool_w + ssm_w
    cos = cos_ref[0]
    sin = sin_ref[0]
    nslab = attn_w // LANES
    half = HEAD_DIM // 2
    first_half = lax.broadcasted_iota(jnp.int32, cos.shape, 1) % HEAD_DIM < half
    for dst in (q_ref, k_ref):
        z = _dot(h, w_cols(c0, c0 + attn_w))
        for s in range(nslab):
            a = z[:, s * LANES:(s + 1) * LANES]
            a_sw = jnp.where(first_half, pltpu.roll(a, LANES - half, axis=1), pltpu.roll(a, half, axis=1))
            dst[0, s] = a * cos + a_sw * sin
        c0 += attn_w
    z = _dot(h, w_cols(c0, c0 + attn_w))
    for s in range(nslab):
        v_ref[0, s] = z[:, s * LANES:(s + 1) * LANES]
    c0 += attn_w
    z = _dot(h, w_cols(c0, c0 + 3 * d_model))
    gate_ref[0] = pltpu.bitcast(_sigmoid(z).astype(BF16), jnp.uint32)


def _in_projection(x, g, w_packed, cos, sin, pool_mix_w, pool_scale, *, pool_w, ssm_w, attn_w):
    b, l, d = x.shape
    tm, halo = 512, 16
    assert halo >= max(POOL_WINDOWS) and halo % SUBLANES == 0
    nslab = attn_w // LANES
    kern = functools.partial(_inproj_kernel, pool_w=pool_w, ssm_w=ssm_w, attn_w=attn_w, d_model=d)
    row = lambda bi, i: (bi, i, 0)
    slab = lambda bi, i: (bi, 0, i, 0)
    const2 = lambda bi, i: (0, 0)
    in_specs = [pl.BlockSpec((1, tm, d), row),
                pl.BlockSpec((1, d), const2),
                pl.BlockSpec(w_packed.shape, const2, pipeline_mode=pl.Buffered(1)),
                pl.BlockSpec((1, tm, LANES), row),
                pl.BlockSpec((1, tm, LANES), row),
                pl.BlockSpec(pool_mix_w.shape, lambda bi, i: (0, 0, 0)),
                pl.BlockSpec((1, pool_w), const2)]
    args = [x, g.reshape(1, d), w_packed, cos, sin, pool_mix_w, pool_scale.reshape(1, pool_w)]
    out_specs = [pl.BlockSpec((1, tm, pool_w), row),
                 pl.BlockSpec((tm, ssm_w), lambda bi, i: (i, bi)),
                 pl.BlockSpec((1, nslab, tm, LANES), slab),
                 pl.BlockSpec((1, nslab, tm, LANES), slab),
                 pl.BlockSpec((1, nslab, tm, LANES), slab),
                 pl.BlockSpec((1, tm // 2, 3 * d), row)]
    out_shape = [jax.ShapeDtypeStruct((b, l, pool_w), BF16),
                 jax.ShapeDtypeStruct((l, b * ssm_w), F32),
                 jax.ShapeDtypeStruct((b, nslab, l, LANES), F32),
                 jax.ShapeDtypeStruct((b, nslab, l, LANES), F32),
                 jax.ShapeDtypeStruct((b, nslab, l, LANES), F32),
                 jax.ShapeDtypeStruct((b, l // 2, 3 * d), jnp.uint32)]
    return pl.pallas_call(
        kern, grid=(b, l // tm), in_specs=in_specs, out_specs=out_specs, out_shape=out_shape,
        scratch_shapes=[pltpu.VMEM((halo, pool_w), F32)],
        compiler_params=_cparams(("arbitrary", "arbitrary")), name="in_projection",
    )(*args)


def _ssm_param_kernel(are_ref, aim_ref, ldt_ref, bre_ref, bim_ref, oar_ref, oai_ref, obr_ref, obi_ref):
    ar = are_ref[...]
    ai = aim_ref[...]
    dt = jnp.exp(ldt_ref[...])
    decay = jnp.exp(ar * dt)
    abar_re = decay * jnp.cos(ai * dt)
    abar_im = decay * jnp.sin(ai * dt)
    den = ar * ar + ai * ai
    nr = abar_re - 1.0
    k_re = (nr * ar + abar_im * ai) / den
    k_im = (abar_im * ar - nr * ai) / den
    oar_ref[...] = abar_re
    oai_ref[...] = abar_im
    for gi in range(are_ref.shape[0]):
        kr = k_re[gi:gi + 1, :]
        ki = k_im[gi:gi + 1, :]
        br = bre_ref[gi]
        bi = bim_ref[gi]
        obr_ref[gi] = kr * br - ki * bi
        obi_ref[gi] = kr * bi + ki * br


def _ssm_params(a_re, a_im, log_dt, b_re, b_im):
    g, n, p = b_re.shape
    brt = jnp.transpose(b_re, (0, 2, 1))
    bit = jnp.transpose(b_im, (0, 2, 1))
    return pl.pallas_call(
        _ssm_param_kernel,
        out_shape=[jax.ShapeDtypeStruct((g, n), F32)] * 2 + [jax.ShapeDtypeStruct((g, p, n), F32)] * 2,
        name="ssm_params",
    )(a_re, a_im, log_dt.reshape(g, 1), brt, bit)


def _block_diag(m):
    k, r, c = m.shape
    eye = jnp.eye(k, dtype=m.dtype)
    return (m[:, :, None, :] * eye[:, None, :, None]).reshape(k * r, k * c)


def _ssm_kernel(u_ref, bre_ref, bim_ref, are_ref, aim_ref, cre_ref, cim_ref, d_ref, y_ref,
                st_ref, xre_ref, xim_ref, sre_ref, sim_ref, uh_ref, yo_ref, *, tc, hw):
    nb = SUBLANES // 2
    hrows = tc * nb
    nslab = hw // LANES
    nstate = xre_ref.shape[0]

    @pl.when(pl.program_id(0) == 0)
    def _():
        st_ref[...] = jnp.zeros_like(st_ref)

    def seq_lanes(b, h, hf):
        s = b * 2 + h
        return slice(s * hw + hf * LANES, s * hw + (hf + 1) * LANES)

    for b in range(nb):
        for h in range(2):
            for hf in range(nslab):
                uh_ref[h, hf, pl.ds(b, tc, stride=nb), :] = u_ref[:, seq_lanes(b, h, hf)]
    for h in range(2):
        ub = jnp.concatenate([uh_ref[h, hf] for hf in range(nslab)], axis=1).astype(BF16)
        x_re = _dot(ub, bre_ref[h])
        x_im = _dot(ub, bim_ref[h])
        for j in range(nstate):
            xre_ref[j, pl.ds(h, hrows, stride=2), :] = x_re[:, j * LANES:(j + 1) * LANES]
            xim_ref[j, pl.ds(h, hrows, stride=2), :] = x_im[:, j * LANES:(j + 1) * LANES]
    a_re = are_ref[...]
    a_im = aim_ref[...]

    def step(t, carry):
        s_re, s_im = carry
        rows = pl.ds(pl.multiple_of(t * SUBLANES, SUBLANES), SUBLANES)
        n_re = a_re * s_re - a_im * s_im + xre_ref[:, rows, :]
        n_im = a_re * s_im + a_im * s_re + xim_ref[:, rows, :]
        sre_ref[:, rows, :] = n_re
        sim_ref[:, rows, :] = n_im
        return n_re, n_im

    s_re, s_im = lax.fori_loop(0, tc, step, (st_ref[0], st_ref[1]), unroll=8)
    st_ref[0] = s_re
    st_ref[1] = s_im
    for h in range(2):
        s_r = jnp.concatenate([sre_ref[j, pl.ds(h, hrows, stride=2), :] for j in range(nstate)], axis=1)
        s_i = jnp.concatenate([sim_ref[j, pl.ds(h, hrows, stride=2), :] for j in range(nstate)], axis=1)
        u_h = jnp.concatenate([uh_ref[h, hf] for hf in range(nslab)], axis=1)
        y = _gelu(_dot(s_r.astype(BF16), cre_ref[h]) - _dot(s_i.astype(BF16), cim_ref[h]) + d_ref[h] * u_h)
        for hf in range(nslab):
            yo_ref[h, hf] = y[:, hf * LANES:(hf + 1) * LANES]
    for b in range(nb):
        for h in range(2):
            for hf in range(nslab):
                y_ref[:, seq_lanes(b, h, hf)] = yo_ref[h, hf, pl.ds(b, tc, stride=nb), :].astype(BF16)


def _ssm_mixer(u_scan, batch, abar_re, abar_im, bbar_re_t, bbar_im_t, c_re, c_im, d_skip):
    l = u_scan.shape[0]
    g, p, n = bbar_re_t.shape
    hg = g // 2
    hw = hg * p
    hs = hg * n
    assert batch * 2 == SUBLANES

    nstate = hs // LANES

    def half_b(bt):
        return jnp.stack([_block_diag(bt[:hg]), _block_diag(bt[hg:])]).astype(BF16)

    def half_c(c):
        ct = jnp.transpose(c, (0, 2, 1))
        return jnp.stack([_block_diag(ct[:hg]), _block_diag(ct[hg:])]).astype(BF16)

    def tile_a(a):
        a = jnp.transpose(a.reshape(2, nstate, LANES), (1, 0, 2))
        return jnp.broadcast_to(a[:, None], (nstate, batch, 2, LANES)).reshape(nstate, SUBLANES, LANES)

    tc = 128
    rows = tc * SUBLANES
    kern = functools.partial(_ssm_kernel, tc=tc, hw=hw)
    const = lambda i: (0, 0, 0)
    state = pltpu.VMEM((nstate, rows, LANES), F32)
    halves = pltpu.VMEM((2, hw // LANES, rows // 2, LANES), F32)
    return pl.pallas_call(
        kern, grid=(l // tc,),
        in_specs=[pl.BlockSpec((tc, SUBLANES * hw), lambda i: (i, 0)),
                  pl.BlockSpec((2, hw, hs), const), pl.BlockSpec((2, hw, hs), const),
                  pl.BlockSpec((nstate, SUBLANES, LANES), const), pl.BlockSpec((nstate, SUBLANES, LANES), const),
                  pl.BlockSpec((2, hs, hw), const), pl.BlockSpec((2, hs, hw), const),
                  pl.BlockSpec((2, 1, hw), const)],
        out_specs=pl.BlockSpec((tc, SUBLANES * hw), lambda i: (i, 0)),
        out_shape=jax.ShapeDtypeStruct((l, SUBLANES * hw), BF16),
        scratch_shapes=[pltpu.VMEM((2, nstate, SUBLANES, LANES), F32), state, state, state, state, halves, halves],
        compiler_params=_cparams(("arbitrary",)), name="ssm_mixer",
    )(u_scan, half_b(bbar_re_t), half_b(bbar_im_t), tile_a(abar_re), tile_a(abar_im),
      half_c(c_re), half_c(c_im), d_skip.reshape(2, 1, hw))


def _attn_group(q_ref, k_ref, v_ref, kb, vb, acc, mrun, drun, og, mg, lg, *, d, ta, i):
    p = ATTN_BLOCK * d
    ncombo = ta // ATTN_BLOCK
    nslab = q_ref.shape[1]
    scale = HEAD_DIM ** -0.5
    kb[:, p:p + ta, :] = k_ref[0]
    vb[:, p:p + ta, :] = v_ref[0]

    @pl.when(i == 0)
    def _():
        kb[:, :p, :] = jnp.zeros((nslab, p, LANES), F32)
        vb[:, :p, :] = jnp.zeros((nslab, p, LANES), F32)

    nh = LANES // HEAD_DIM
    qi = lax.broadcasted_iota(jnp.int32, (nh * ATTN_BLOCK, ATTN_BLOCK), 0) % ATTN_BLOCK
    kk = lax.broadcasted_iota(jnp.int32, (nh * ATTN_BLOCK, ATTN_BLOCK), 1)
    mask_cur = kk <= qi
    mask_prev = kk >= qi
    low = lax.broadcasted_iota(jnp.int32, (ATTN_BLOCK, ATTN_BLOCK), 1) < HEAD_DIM

    def rows_at(start):
        return pl.ds(start, ATTN_BLOCK) if d == 1 else pl.ds(start, ATTN_BLOCK, stride=d)

    def combo(c, carry):
        r = c % d
        n = c // d
        qs = n * p + r
        has_prev = jnp.logical_or(i > 0, n > 0)
        mask_p = jnp.logical_and(mask_prev, has_prev)
        for s in range(nslab):
            q = q_ref[0, s, rows_at(qs), :] * scale
            kc = kb[s, rows_at(p + qs), :].astype(BF16)
            kp = kb[s, rows_at(qs), :].astype(BF16)
            vc = vb[s, rows_at(p + qs), :].astype(BF16)
            vp = vb[s, rows_at(qs), :].astype(BF16)
            qh = jnp.concatenate([jnp.where(low, q, 0.0), jnp.where(low, 0.0, q)], axis=0).astype(BF16)
            sc = jnp.where(mask_cur, _dot_nt(qh, kc), -jnp.inf)
            sp = jnp.where(mask_p, _dot_nt(qh, kp), -jnp.inf)
            m = jnp.max(jnp.maximum(sc, sp), axis=1, keepdims=True)
            pc = jnp.exp(sc - m)
            pp = jnp.exp(sp - m)
            den = jnp.sum(pc + pp, axis=1, keepdims=True)
            o = _dot(pc.astype(BF16), vc) + _dot(pp.astype(BF16), vp)
            rows = rows_at(qs)
            og[s, rows, :] = jnp.where(low, o[:ATTN_BLOCK], o[ATTN_BLOCK:])
            mg[s, rows, :] = jnp.where(low, m[:ATTN_BLOCK], m[ATTN_BLOCK:])
            lg[s, rows, :] = jnp.where(low, den[:ATTN_BLOCK], den[ATTN_BLOCK:])
        return carry

    lax.fori_loop(0, ncombo, combo, 0, unroll=4 if d < max(ATTN_DILATIONS) else 2)
    kb[:, :p, :] = kb[:, ta:ta + p, :]
    vb[:, :p, :] = vb[:, ta:ta + p, :]

    rc = 256
    for s in range(nslab):
        for r0 in range(0, ta, rc):
            rs = slice(r0, r0 + rc)
            m_old = mrun[s, rs, :]
            m_blk = mg[s, rs, :]
            m_new = jnp.maximum(m_old, m_blk)
            a_old = jnp.exp(m_old - m_new)
            a_blk = jnp.exp(m_blk - m_new)
            acc[s, rs, :] = acc[s, rs, :] * a_old + og[s, rs, :] * a_blk
            drun[s, rs, :] = drun[s, rs, :] * a_old + lg[s, rs, :] * a_blk
            mrun[s, rs, :] = m_new


def _attn_kernel(q_ref, k_ref, v_ref, o_ref, *scratch, ta):
    ng = len(ATTN_DILATIONS)
    kbs = scratch[0:2 * ng:2]
    vbs = scratch[1:2 * ng:2]
    acc, mrun, drun, og, mg, lg = scratch[2 * ng:]
    i = pl.program_id(1)
    g = pl.program_id(2)

    @pl.when(g == 0)
    def _():
        acc[...] = jnp.zeros_like(acc)
        drun[...] = jnp.zeros_like(drun)
        mrun[...] = jnp.full(mrun.shape, -jnp.inf, F32)

    for gi, d in enumerate(ATTN_DILATIONS):
        @pl.when(g == gi)
        def _(gi=gi, d=d):
            _attn_group(q_ref, k_ref, v_ref, kbs[gi], vbs[gi], acc, mrun, drun, og, mg, lg, d=d, ta=ta, i=i)

    @pl.when(g == ng - 1)
    def _():
        for s in range(acc.shape[0]):
            o_ref[0, :, s * LANES:(s + 1) * LANES] = acc[s] / drun[s]


def _attention(q, k, v):
    b, nslab, l, _ = q.shape
    ng = len(ATTN_DILATIONS)
    gs = nslab // ng
    ta = ATTN_BLOCK * max(ATTN_DILATIONS)
    assert l % ta == 0
    blk = pl.BlockSpec((1, gs, ta, LANES), lambda bi, i, g: (bi, g, i, 0))
    scratch = []
    for d in ATTN_DILATIONS:
        scratch += [pltpu.VMEM((gs, ATTN_BLOCK * d + ta, LANES), F32)] * 2
    scratch += [pltpu.VMEM((gs, ta, LANES), F32)] * 6
    return pl.pallas_call(
        functools.partial(_attn_kernel, ta=ta), grid=(b, l // ta, ng),
        in_specs=[blk, blk, blk],
        out_specs=pl.BlockSpec((1, ta, gs * LANES), lambda bi, i, g: (bi, i, 0)),
        out_shape=jax.ShapeDtypeStruct((b, l, gs * LANES), F32),
        scratch_shapes=scratch,
        compiler_params=_cparams(("arbitrary", "arbitrary", "arbitrary")), name="dilated_attention",
    )(q, k, v)


def _merge_kernel(pm_ref, ys_ref, ao_ref, gate_ref, x_ref, wp_ref, wg_ref, ws_ref, wa_ref, wo_ref, g2_ref,
                  xo_ref, h2_ref, *, d_model, ssm_w):
    def weight(ref):
        return pltpu.bitcast(ref[...], BF16)

    y_pool = _dot(pm_ref[0], weight(wp_ref))
    glu = _dot(ys_ref[...], weight(wg_ref))
    sg = (glu[:, :ssm_w] * _sigmoid(glu[:, ssm_w:])).astype(BF16)
    y_ssm = _dot(sg, weight(ws_ref))
    y_attn = _dot(ao_ref[0].astype(BF16), weight(wa_ref))
    gate = pltpu.bitcast(gate_ref[0], BF16).astype(F32)
    merged = (gate[:, :d_model] * y_pool + gate[:, d_model:2 * d_model] * y_ssm
              + gate[:, 2 * d_model:] * y_attn)
    xn = x_ref[0] + _dot(merged.astype(BF16), weight(wo_ref))
    xo_ref[0] = xn
    h2_ref[...] = _rmsnorm(xn, g2_ref[...]).T.astype(BF16)


def _merge(pm, ys, ao, gates, x, w_pool, w_glu, w_ssm, w_attn, w_out, g2):
    b, l, d = x.shape
    tm = 512
    ssm_w = 2 * w_glu.shape[0]
    row = lambda bi, i: (bi, i, 0)
    const = lambda bi, i: (0, 0)
    full = lambda a: pl.BlockSpec(a.shape, const)
    return pl.pallas_call(
        functools.partial(_merge_kernel, d_model=d, ssm_w=ssm_w), grid=(b, l // tm),
        in_specs=[pl.BlockSpec((1, tm, pm.shape[-1]), row),
                  pl.BlockSpec((tm, ssm_w), lambda bi, i: (i, bi)),
                  pl.BlockSpec((1, tm, ao.shape[-1]), row),
                  pl.BlockSpec((1, tm // 2, 3 * d), row),
                  pl.BlockSpec((1, tm, d), row),
                  full(w_pool), full(w_glu), full(w_ssm), full(w_attn), full(w_out),
                  pl.BlockSpec((1, d), const)],
        out_specs=[pl.BlockSpec((1, tm, d), row),
                   pl.BlockSpec((d, tm), lambda bi, i: (0, bi * (l // tm) + i))],
        out_shape=[jax.ShapeDtypeStruct((b, l, d), F32), jax.ShapeDtypeStruct((d, b * l), BF16)],
        compiler_params=_cparams(("parallel", "parallel")), name="branch_merge",
    )(pm, ys, ao, gates, x, w_pool, w_glu, w_ssm, w_attn, w_out, g2.reshape(1, d))


def _candidate_pairs(k):
    return [(a, b) for a in range(k) for b in range(k) if (a + 1) * (b + 1) <= k]


_CODE_UNIT = 2.0 ** 121
_CODE_BASE = 80


def _removed_code(r):
    return -float(_CODE_BASE + r) * _CODE_UNIT


def _removed_round(w):
    return w * (-1.0 / _CODE_UNIT) - float(_CODE_BASE)


def _vreg_rows(x):
    return x.reshape(x.shape[0] // SUBLANES, SUBLANES, x.shape[1])


def _max_all_rows(w):
    p = jnp.max(w, axis=0)
    for sh in (4, 2, 1):
        p = jnp.maximum(p, pltpu.roll(p, sh, axis=0))
    return p


def _topk_round(work, rows):
    m = jnp.max(work, axis=0, keepdims=True)
    idx = jnp.min(jnp.where(work == m, rows, float(work.shape[0])), axis=0, keepdims=True)
    hit = rows == idx
    return m, hit, jnp.where(hit, -jnp.inf, work)


def _route_kernel(h_ref, wq_ref, k1_ref, k2_ref, r2_ref, e2_ref, n1_ref, e1_ref,
                  qt_ref, s1_ref, s2_ref, v1_ref, v2_ref, cand_ref, sel_ref, tied_ref,
                  *, nheads, nkeys, topk, tt):
    qt_ref[...] = _dot(pltpu.bitcast(wq_ref[...], BF16), h_ref[...])
    hc = k1_ref.shape[1]
    pairs = _candidate_pairs(topk)
    ncand = cand_ref.shape[1]

    def head(hd, carry):
        base = pl.multiple_of(hd * 2 * hc, 2 * hc)
        q1 = qt_ref[pl.ds(base, hc), :].astype(BF16)
        q2 = qt_ref[pl.ds(base + hc, hc), :].astype(BF16)
        s1_ref[...] = _dot(k1_ref[...], q1)
        s2_ref[...] = _dot(k2_ref[...], q2)

        def column(j, exact):
            if isinstance(j, int):
                cols = slice(j * LANES, (j + 1) * LANES)
            else:
                cols = pl.ds(pl.multiple_of(j * LANES, LANES), LANES)
            s1 = s1_ref[:, cols]
            s2 = s2_ref[:, cols]
            v1, v2, cand, selr = v1_ref.at[j], v2_ref.at[j], cand_ref.at[j], sel_ref.at[j]

            def build_candidates():
                for ci, (a, bq) in enumerate(pairs):
                    cand[ci:ci + 1, :] = v1[a:a + 1, :] + v2[bq:bq + 1, :]
                if ncand > len(pairs):
                    cand[len(pairs):, :] = jnp.full((ncand - len(pairs), LANES), -jnp.inf, F32)
                return cand[...]

            def finish(rank1, rank2, sel, cv):
                selr[...] = sel
                top = v1[0:1, :] + v2[0:1, :]
                z = jnp.sum(sel * jnp.exp(jnp.where(sel > 0.0, cv, top) - top), axis=0, keepdims=True)
                n1 = jnp.zeros(s1.shape, F32)
                off = 0
                for a in range(topk):
                    cnt = topk // (a + 1)
                    n_a = jnp.sum(selr[off:off + cnt, :], axis=0, keepdims=True)
                    n1 = jnp.where(rank1 == float(a), n_a, n1)
                    off += cnt
                n1_ref[hd, :, cols] = n1
                e1_ref[hd, :, cols] = jnp.exp(s1 - v1[0:1, :]) / z
                r2_ref[hd, j] = pltpu.bitcast(rank2.astype(BF16), jnp.uint32)
                e2_ref[hd, j] = pltpu.bitcast(jnp.exp(s2 - v2[0:1, :]).astype(BF16), jnp.uint32)

            if exact:
                key_rows = lax.broadcasted_iota(jnp.int32, s1.shape, 0).astype(F32)
                x1, x2 = s1, s2
                rk1 = jnp.full(s1.shape, float(nkeys), F32)
                rk2 = rk1
                for r in range(topk):
                    m1, hit1, x1 = _topk_round(x1, key_rows)
                    m2, hit2, x2 = _topk_round(x2, key_rows)
                    rk1 = jnp.where(hit1, float(r), rk1)
                    rk2 = jnp.where(hit2, float(r), rk2)
                    v1[r:r + 1, :] = m1
                    v2[r:r + 1, :] = m2
                cvs = build_candidates()
                cand_rows = lax.broadcasted_iota(jnp.int32, cvs.shape, 0).astype(F32)
                xc = cvs
                sl = jnp.zeros(cvs.shape, F32)
                for r in range(topk):
                    _, hit, xc = _topk_round(xc, cand_rows)
                    sl = jnp.where(hit, 1.0, sl)
                finish(rk1, rk2, sl, cvs)
                return None

            w1, w2 = _vreg_rows(s1), _vreg_rows(s2)
            for r in range(topk):
                m1 = _max_all_rows(w1)
                m2 = _max_all_rows(w2)
                w1 = jnp.where(w1 == m1[None], _removed_code(r), w1)
                w2 = jnp.where(w2 == m2[None], _removed_code(r), w2)
                v1[r:r + 1, :] = m1[0:1, :]
                v2[r:r + 1, :] = m2[0:1, :]
            w1, w2 = w1.reshape(s1.shape), w2.reshape(s2.shape)
            gone1 = w1 <= _removed_code(0)
            gone2 = w2 <= _removed_code(0)
            rank1 = jnp.where(gone1, _removed_round(w1), float(nkeys))
            rank2 = jnp.where(gone2, _removed_round(w2), float(nkeys))
            cv = build_candidates()
            wc = _vreg_rows(cv)
            for r in range(topk):
                wc = jnp.where(wc == _max_all_rows(wc)[None], _removed_code(r), wc)
            sel = jnp.where(wc.reshape(cv.shape) <= _removed_code(0), 1.0, 0.0)
            removed = (jnp.sum(jnp.where(gone1, 1.0, 0.0), axis=0, keepdims=True)
                       + jnp.sum(jnp.where(gone2, 1.0, 0.0), axis=0, keepdims=True)
                       + jnp.sum(sel, axis=0, keepdims=True))
            finish(rank1, rank2, sel, cv)
            expected = float(3 * topk + ncand - len(pairs))
            return jnp.max(jnp.where(removed != expected, 1.0, 0.0))

        for j in range(tt // LANES):
            tied_ref[j] = column(j, exact=False)

        def redo(j, carry2):
            @pl.when(tied_ref[j] > 0.0)
            def _():
                column(j, exact=True)
            return carry2

        lax.fori_loop(0, tt // LANES, redo, 0)
        return carry

    lax.fori_loop(0, nheads, head, 0)


def _peer_route(h2t, wq_t, k1, k2):
    d, t = h2t.shape
    nkeys, hc = k1.shape
    qrows = 2 * wq_t.shape[0]
    nheads = qrows // (2 * hc)
    tt = 1024
    npairs = len(_candidate_pairs(PEER_TOPK))
    ncand = -(-npairs // SUBLANES) * SUBLANES
    ncol = tt // LANES
    kern = functools.partial(_route_kernel, nheads=nheads, nkeys=nkeys, topk=PEER_TOPK, tt=tt)
    kblk = pl.BlockSpec((nheads, tt // LANES, nkeys // 2, LANES), lambda i: (0, i, 0, 0))
    ksh = jax.ShapeDtypeStruct((nheads, t // LANES, nkeys // 2, LANES), jnp.uint32)
    oblk = pl.BlockSpec((nheads, nkeys, tt), lambda i: (0, 0, i))
    osh = jax.ShapeDtypeStruct((nheads, nkeys, t), F32)
    return pl.pallas_call(
        kern, grid=(t // tt,),
        in_specs=[pl.BlockSpec((d, tt), lambda i: (0, i)),
                  pl.BlockSpec(wq_t.shape, lambda i: (0, 0)),
                  pl.BlockSpec(k1.shape, lambda i: (0, 0)),
                  pl.BlockSpec(k2.shape, lambda i: (0, 0))],
        out_specs=[kblk, kblk, oblk, oblk], out_shape=[ksh, ksh, osh, osh],
        scratch_shapes=[pltpu.VMEM((qrows, tt), F32),
                        pltpu.VMEM((nkeys, tt), F32), pltpu.VMEM((nkeys, tt), F32),
                        pltpu.VMEM((ncol, PEER_TOPK, LANES), F32), pltpu.VMEM((ncol, PEER_TOPK, LANES), F32),
                        pltpu.VMEM((ncol, ncand, LANES), F32), pltpu.VMEM((ncol, ncand, LANES), F32),
                        pltpu.SMEM((ncol,), F32)],
        compiler_params=_cparams(("parallel",)), name="peer_route",
    )(h2t, wq_t, k1, k2)


def _pack_kernel(x_ref, o_ref, *, transpose):
    x = x_ref[0]
    if transpose:
        x = x.T
    o_ref[...] = pltpu.bitcast(x.astype(BF16), jnp.uint32)


def _pack_table(stacked, layer, *, transpose):
    _, r, c = stacked.shape
    tb = min(r, 512 if c <= 1024 else 128)
    in_spec = pl.BlockSpec((1, tb, c), lambda i: (layer, i, 0))
    if transpose:
        out_spec = pl.BlockSpec((c // 2, tb), lambda i: (0, i))
        out_shape = jax.ShapeDtypeStruct((c // 2, r), jnp.uint32)
    else:
        out_spec = pl.BlockSpec((tb // 2, c), lambda i: (i, 0))
        out_shape = jax.ShapeDtypeStruct((r // 2, c), jnp.uint32)
    return pl.pallas_call(
        functools.partial(_pack_kernel, transpose=transpose), grid=(r // tb,),
        in_specs=[in_spec], out_specs=out_spec, out_shape=out_shape,
        compiler_params=_cparams(("parallel",)), name="pack_table",
    )(stacked)


def _ffn_kernel(h_ref, x_ref, g_ref, u_ref, un_ref, vt_ref, r2_ref, e2_ref, n1_ref, e1_ref, o_ref,
                acc_ref, at_ref, pt_ref, *, nheads, nkeys, cw, final_norm):
    j = pl.program_id(1)
    tt = h_ref.shape[1]
    et = 2 * u_ref.shape[0]
    nhalf = cw // LANES

    def first_pre_activations(table_ref):
        at_ref[0] = _dot(pltpu.bitcast(table_ref[0:at_ref.shape[1] // 2, :], BF16), h_ref[:, 0:cw])

    @pl.when(j == 0)
    def _():
        acc_ref[...] = jnp.zeros_like(acc_ref)
        first_pre_activations(u_ref)

    ktiles = nkeys // BF16_ROWS
    es = at_ref.shape[1]
    nsub = es // nkeys
    units = [(eh, c) for eh in range(et // es) for c in range(tt // cw)]
    n_at, n_pt = at_ref.shape[0], pt_ref.shape[0]

    def pre_activations(k):
        eh, c = units[k]
        u = pltpu.bitcast(u_ref[eh * es // 2:(eh + 1) * es // 2, :], BF16)
        at_ref[k % n_at] = _dot(u, h_ref[:, c * cw:(c + 1) * cw])

    assert n_at >= len(units)
    for k, (eh, c) in enumerate(units):
        if k + 1 < len(units):
            pre_activations(k + 1)
        else:
            first_pre_activations(un_ref)
        for jj in range(nsub):
            row = eh * nsub + jj
            for hf in range(nhalf):
                ck = c * nhalf + hf
                cols = slice(ck * LANES, (ck + 1) * LANES)
                lanes = slice(hf * LANES, (hf + 1) * LANES)
                gate = jnp.zeros((ktiles, BF16_ROWS, LANES), BF16)
                for hd in range(nheads):
                    n_row = jnp.broadcast_to(n1_ref[hd, row:row + 1, cols], (BF16_ROWS, LANES)).astype(BF16)
                    e_row = jnp.broadcast_to(e1_ref[hd, row:row + 1, cols], (BF16_ROWS, LANES)).astype(BF16)
                    r2 = pltpu.bitcast(r2_ref[hd, ck], BF16).reshape(ktiles, BF16_ROWS, LANES)
                    e2 = pltpu.bitcast(e2_ref[hd, ck], BF16).reshape(ktiles, BF16_ROWS, LANES)
                    gate = gate + jnp.where(r2 < n_row[None], e2, 0) * e_row[None]
                a = at_ref[k % n_at, jj * nkeys:(jj + 1) * nkeys, lanes].astype(BF16)
                p = (gate * _gelu(a.reshape(ktiles, BF16_ROWS, LANES))).reshape(nkeys, LANES)
                pt_ref[k % n_pt, jj * nkeys // 2:(jj + 1) * nkeys // 2, lanes] = pltpu.bitcast(p, jnp.uint32)
        p = pltpu.bitcast(pt_ref[k % n_pt], BF16)
        vt = pltpu.bitcast(vt_ref[:, eh * es:(eh + 1) * es], BF16)
        acc_ref[c] += _dot(vt, p)

    @pl.when(j == pl.num_programs(1) - 1)
    def _():
        for c in range(tt // cw):
            rows = slice(c * cw, (c + 1) * cw)
            y = x_ref[rows, :] + acc_ref[c].T
            o_ref[rows, :] = _rmsnorm(y, g_ref[...]) if final_norm else y


def _peer_experts(h2t, x, u_tab, v_tab_t, r2, e2, n1, e1, final_g):
    d, t = h2t.shape
    ne = 2 * u_tab.shape[0]
    nheads, nkeys, _ = n1.shape
    tt, et, cw = 1024, 1024, 256
    es = nkeys * SUBLANES
    at_ring, pt_ring = 4, 4
    final_norm = final_g is not None
    g = (final_g if final_norm else jnp.ones((d,), F32)).reshape(1, d)
    kern = functools.partial(_ffn_kernel, nheads=nheads, nkeys=nkeys, cw=cw, final_norm=final_norm)
    full_keys = pl.BlockSpec((nheads, tt // LANES, nkeys // 2, LANES), lambda i, j: (0, i, 0, 0))
    sub_keys = pl.BlockSpec((nheads, et // nkeys, tt), lambda i, j: (0, j, i))
    return pl.pallas_call(
        kern, grid=(t // tt, ne // et),
        in_specs=[pl.BlockSpec((d, tt), lambda i, j: (0, i)),
                  pl.BlockSpec((tt, d), lambda i, j: (i, 0)),
                  pl.BlockSpec((1, d), lambda i, j: (0, 0)),
                  pl.BlockSpec((et // 2, d), lambda i, j: (j, 0)),
                  pl.BlockSpec((es // 2, d), lambda i, j: (jnp.minimum(j + 1, ne // et - 1) * (et // es), 0)),
                  pl.BlockSpec((d // 2, et), lambda i, j: (0, j)),
                  full_keys, full_keys, sub_keys, sub_keys],
        out_specs=pl.BlockSpec((tt, d), lambda i, j: (i, 0)),
        out_shape=jax.ShapeDtypeStruct((t, d), F32),
        scratch_shapes=[pltpu.VMEM((tt // cw, d, cw), F32),
                        pltpu.VMEM((at_ring, es, cw), F32),
                        pltpu.VMEM((pt_ring, es // 2, cw), jnp.uint32)],
        compiler_params=_cparams(("parallel", "arbitrary")), name="peer_experts",
    )(h2t, x, g, u_tab, u_tab, v_tab_t, r2, e2, n1, e1)


def kernel(x, positions, norm1_g, w_in, pool_w, pool_scale, pool_proj, ssm_a_re, ssm_a_im, ssm_log_dt,
           ssm_b_re, ssm_b_im, ssm_c_re, ssm_c_im, ssm_d, ssm_glu, ssm_proj, attn_proj, w_out, norm2_g,
           peer_wq, peer_k1, peer_k2, peer_u, peer_v, final_g):
    b, l, d = x.shape
    depth = w_in.shape[0]
    pool_w_cols = pool_proj.shape[1]
    ssm_w_cols = ssm_proj.shape[1]
    attn_w = len(ATTN_DILATIONS) * HEADS_PER_GROUP * HEAD_DIM
    cos, sin = _rope_tables(positions)
    for layer in range(depth):
        pm, u_ssm, q, k, v, gates = _in_projection(
            x, norm1_g[layer], _pack_table(w_in, layer, transpose=False), cos, sin,
            pool_w[layer].astype(BF16), pool_scale[layer],
            pool_w=pool_w_cols, ssm_w=ssm_w_cols, attn_w=attn_w)
        abar_re, abar_im, bbar_re_t, bbar_im_t = _ssm_params(
            ssm_a_re[layer], ssm_a_im[layer], ssm_log_dt[layer], ssm_b_re[layer], ssm_b_im[layer])
        ys = _ssm_mixer(u_ssm, b, abar_re, abar_im, bbar_re_t, bbar_im_t,
                        ssm_c_re[layer], ssm_c_im[layer], ssm_d[layer])
        ao = _attention(q, k, v)
        packed = [_pack_table(wt, layer, transpose=False)
                  for wt in (pool_proj, ssm_glu, ssm_proj, attn_proj, w_out)]
        x, h2t = _merge(pm, ys, ao, gates, x, *packed, norm2_g[layer])
        r2, e2, n1, e1 = _peer_route(h2t, _pack_table(peer_wq, layer, transpose=True), peer_k1[layer].astype(BF16),
                                     peer_k2[layer].astype(BF16))
        x = _peer_experts(h2t, x.reshape(b * l, d), _pack_table(peer_u, layer, transpose=False),
                          _pack_table(peer_v, layer, transpose=True), r2, e2, n1, e1,
                          final_g if layer == depth - 1 else None).reshape(b, l, d)
    return x
```

```python
import functools
import math

import jax
import jax.numpy as jnp
from jax import lax
from jax.experimental import pallas as pl
from jax.experimental.pallas import tpu as pltpu

F32 = jnp.float32
BF16 = jnp.bfloat16

EPS = 1e-6
POOL_WINDOWS = (2, 4, 8, 16)
POOL_GROUP = 128
SSM_GROUP = 16
SSM_GROUPS = 32
SSM_STATE = 64
HEAD_DIM = 64
HEADS_PER_GROUP = 4
ATTN_DILATIONS = (1, 4, 16)
ATTN_BLOCK = 128
ROPE_THETA = 10000.0
PEER_HEADS = 8
PEER_TOPK = 16

LANES = 128
SUBLANES = 8
BF16_ROWS = 16
VMEM_LIMIT = 56 * 1024 * 1024


def _cparams(sem):
    return pltpu.CompilerParams(dimension_semantics=sem, vmem_limit_bytes=VMEM_LIMIT)


def _gelu(x):
    c = math.sqrt(2.0 / math.pi)
    return x * (0.5 * (1.0 + jnp.tanh(c * (x + 0.044715 * (x * x * x)))))


def _sigmoid(x):
    return 1.0 / (1.0 + jnp.exp(-x))


def _rmsnorm(x, g):
    ms = jnp.mean(x * x, axis=-1, keepdims=True)
    return x * lax.rsqrt(ms + EPS) * g


def _dot(a, b):
    return jnp.dot(a, b, preferred_element_type=F32)


def _dot_nt(a, b):
    return lax.dot_general(a, b, (((1,), (1,)), ((), ())), preferred_element_type=F32)


def _rope_kernel(pos_ref, invf_ref, sign_ref, cos_ref, sin_ref):
    ang = pos_ref[...].astype(F32) * invf_ref[...]
    cos_ref[...] = jnp.cos(ang)
    sin_ref[...] = jnp.sin(ang) * sign_ref[...]


def _rope_tables(positions):
    b, l = positions.shape
    t = b * l
    half = HEAD_DIM // 2
    inv_freq = ROPE_THETA ** (-jnp.arange(half, dtype=F32) / half)
    invf = jnp.tile(inv_freq, LANES // half).reshape(1, LANES)
    sign = jnp.tile(jnp.concatenate([-jnp.ones((half,), F32), jnp.ones((half,), F32)]),
                    LANES // HEAD_DIM).reshape(1, LANES)
    pos_b = jnp.broadcast_to(positions.reshape(t, 1), (t, LANES))
    tr = 2048
    cos, sin = pl.pallas_call(
        _rope_kernel,
        grid=(t // tr,),
        in_specs=[pl.BlockSpec((tr, LANES), lambda i: (i, 0)),
                  pl.BlockSpec((1, LANES), lambda i: (0, 0)),
                  pl.BlockSpec((1, LANES), lambda i: (0, 0))],
        out_specs=[pl.BlockSpec((tr, LANES), lambda i: (i, 0))] * 2,
        out_shape=[jax.ShapeDtypeStruct((t, LANES), F32)] * 2,
        compiler_params=_cparams(("parallel",)),
        name="rope_tables",
    )(pos_b, invf, sign)
    return cos.reshape(b, l, LANES), sin.reshape(b, l, LANES)


def _pool_mix(cur, prev, t0, w_ref, scale_ref, o_ref):
    tm, halo = cur.shape[0], prev.shape[0]
    ext = jnp.concatenate([prev, cur], axis=0)
    t_idx = t0 + lax.broadcasted_iota(jnp.int32, (tm, POOL_GROUP), 0)
    for gi, win in enumerate(POOL_WINDOWS):
        cs = slice(gi * POOL_GROUP, (gi + 1) * POOL_GROUP)
        s = ext[:, cs]
        sh = 1
        while sh < win:
            s = s + pltpu.roll(s, sh, axis=0)
            sh *= 2
        count = jnp.minimum(t_idx + 1, win).astype(F32)
        pooled = s[halo:, :] / count - cur[:, cs]
        mixed = _dot(pooled.astype(BF16), w_ref[gi]) * scale_ref[:, cs]
        o_ref[0, :, cs] = mixed.astype(BF16)


def _inproj_kernel(x_ref, g_ref, w_ref, cos_ref, sin_ref, pw_ref, ps_ref,
                   pool_ref, ssm_ref, q_ref, k_ref, v_ref, gate_ref, halo_ref, *, pool_w, ssm_w, attn_w, d_model):
    i = pl.program_id(1)
    tm = x_ref.shape[1]
    h = _rmsnorm(x_ref[0], g_ref[...]).astype(BF16)
    c0 = 0

    def w_cols(lo, hi):
        return pltpu.bitcast(w_ref[:, lo:hi], BF16)

    @pl.when(i == 0)
    def _():
        halo_ref[...] = jnp.zeros_like(halo_ref)

    z = _dot(h, w_cols(c0, c0 + pool_w + ssm_w))
    u_pool = z[:, :pool_w]
    _pool_mix(u_pool, halo_ref[...], i * tm, pw_ref, ps_ref, pool_ref)
    halo_ref[...] = u_pool[tm - halo_ref.shape[0]:, :]
    ssm_ref[...] = z[:, pool_w:]
    c0 += pool_w + ssm_w
    cos = cos_ref[0]
    sin = sin_ref[0]
    nslab = attn_w // LANES
    half = HEAD_DIM // 2
    first_half = lax.broadcasted_iota(jnp.int32, cos.shape, 1) % HEAD_DIM < half
    for dst in (q_ref, k_ref):
        z = _dot(h, w_cols(c0, c0 + attn_w))
        for s in range(nslab):
            a = z[:, s * LANES:(s + 1) * LANES]
            a_sw = jnp.where(first_half, pltpu.roll(a, LANES - half, axis=1), pltpu.roll(a, half, axis=1))
            dst[0, s] = a * cos + a_sw * sin
        c0 += attn_w
    z = _dot(h, w_cols(c0, c0 + attn_w))
    for s in range(nslab):
        v_ref[0, s] = z[:, s * LANES:(s + 1) * LANES]
    c0 += attn_w
    z = _dot(h, w_cols(c0, c0 + 3 * d_model))
    gate_ref[0] = pltpu.bitcast(_sigmoid(z).astype(BF16), jnp.uint32)


def _in_projection(x, g, w_packed, cos, sin, pool_mix_w, pool_scale, *, pool_w, ssm_w, attn_w):
    b, l, d = x.shape
    tm, halo = 512, 16
    assert halo >= max(POOL_WINDOWS) and halo % SUBLANES == 0
    nslab = attn_w // LANES
    kern = functools.partial(_inproj_kernel, pool_w=pool_w, ssm_w=ssm_w, attn_w=attn_w, d_model=d)
    row = lambda bi, i: (bi, i, 0)
    slab = lambda bi, i: (bi, 0, i, 0)
    const2 = lambda bi, i: (0, 0)
    in_specs = [pl.BlockSpec((1, tm, d), row),
                pl.BlockSpec((1, d), const2),
                pl.BlockSpec(w_packed.shape, const2, pipeline_mode=pl.Buffered(1)),
                pl.BlockSpec((1, tm, LANES), row),
                pl.BlockSpec((1, tm, LANES), row),
                pl.BlockSpec(pool_mix_w.shape, lambda bi, i: (0, 0, 0)),
                pl.BlockSpec((1, pool_w), const2)]
    args = [x, g.reshape(1, d), w_packed, cos, sin, pool_mix_w, pool_scale.reshape(1, pool_w)]
    out_specs = [pl.BlockSpec((1, tm, pool_w), row),
                 pl.BlockSpec((tm, ssm_w), lambda bi, i: (i, bi)),
                 pl.BlockSpec((1, nslab, tm, LANES), slab),
                 pl.BlockSpec((1, nslab, tm, LANES), slab),
                 pl.BlockSpec((1, nslab, tm, LANES), slab),
                 pl.BlockSpec((1, tm // 2, 3 * d), row)]
    out_shape = [jax.ShapeDtypeStruct((b, l, pool_w), BF16),
                 jax.ShapeDtypeStruct((l, b * ssm_w), F32),
                 jax.ShapeDtypeStruct((b, nslab, l, LANES), F32),
                 jax.ShapeDtypeStruct((b, nslab, l, LANES), F32),
                 jax.ShapeDtypeStruct((b, nslab, l, LANES), F32),
                 jax.ShapeDtypeStruct((b, l // 2, 3 * d), jnp.uint32)]
    return pl.pallas_call(
        kern, grid=(b, l // tm), in_specs=in_specs, out_specs=out_specs, out_shape=out_shape,
        scratch_shapes=[pltpu.VMEM((halo, pool_w), F32)],
        compiler_params=_cparams(("arbitrary", "arbitrary")), name="in_projection",
    )(*args)


def _ssm_param_kernel(are_ref, aim_ref, ldt_ref, bre_ref, bim_ref, oar_ref, oai_ref, obr_ref, obi_ref):
    ar = are_ref[...]
    ai = aim_ref[...]
    dt = jnp.exp(ldt_ref[...])
    decay = jnp.exp(ar * dt)
    abar_re = decay * jnp.cos(ai * dt)
    abar_im = decay * jnp.sin(ai * dt)
    den = ar * ar + ai * ai
    nr = abar_re - 1.0
    k_re = (nr * ar + abar_im * ai) / den
    k_im = (abar_im * ar - nr * ai) / den
    oar_ref[...] = abar_re
    oai_ref[...] = abar_im
    for gi in range(are_ref.shape[0]):
        kr = k_re[gi:gi + 1, :]
        ki = k_im[gi:gi + 1, :]
        br = bre_ref[gi]
        bi = bim_ref[gi]
        obr_ref[gi] = kr * br - ki * bi
        obi_ref[gi] = kr * bi + ki * br


def _ssm_params(a_re, a_im, log_dt, b_re, b_im):
    g, n, p = b_re.shape
    brt = jnp.transpose(b_re, (0, 2, 1))
    bit = jnp.transpose(b_im, (0, 2, 1))
    return pl.pallas_call(
        _ssm_param_kernel,
        out_shape=[jax.ShapeDtypeStruct((g, n), F32)] * 2 + [jax.ShapeDtypeStruct((g, p, n), F32)] * 2,
        name="ssm_params",
    )(a_re, a_im, log_dt.reshape(g, 1), brt, bit)


def _block_diag(m):
    k, r, c = m.shape
    eye = jnp.eye(k, dtype=m.dtype)
    return (m[:, :, None, :] * eye[:, None, :, None]).reshape(k * r, k * c)


def _ssm_kernel(u_ref, bre_ref, bim_ref, are_ref, aim_ref, cre_ref, cim_ref, d_ref, y_ref,
                st_ref, xre_ref, xim_ref, sre_ref, sim_ref, uh_ref, yo_ref, *, tc, hw):
    nb = SUBLANES // 2
    hrows = tc * nb
    nslab = hw // LANES
    nstate = xre_ref.shape[0]

    @pl.when(pl.program_id(0) == 0)
    def _():
        st_ref[...] = jnp.zeros_like(st_ref)

    def seq_lanes(b, h, hf):
        s = b * 2 + h
        return slice(s * hw + hf * LANES, s * hw + (hf + 1) * LANES)

    for b in range(nb):
        for h in range(2):
            for hf in range(nslab):
                uh_ref[h, hf, pl.ds(b, tc, stride=nb), :] = u_ref[:, seq_lanes(b, h, hf)]
    for h in range(2):
        ub = jnp.concatenate([uh_ref[h, hf] for hf in range(nslab)], axis=1).astype(BF16)
        x_re = _dot(ub, bre_ref[h])
        x_im = _dot(ub, bim_ref[h])
        for j in range(nstate):
            xre_ref[j, pl.ds(h, hrows, stride=2), :] = x_re[:, j * LANES:(j + 1) * LANES]
            xim_ref[j, pl.ds(h, hrows, stride=2), :] = x_im[:, j * LANES:(j + 1) * LANES]
    a_re = are_ref[...]
    a_im = aim_ref[...]

    def step(t, carry):
        s_re, s_im = carry
        rows = pl.ds(pl.multiple_of(t * SUBLANES, SUBLANES), SUBLANES)
        n_re = a_re * s_re - a_im * s_im + xre_ref[:, rows, :]
        n_im = a_re * s_im + a_im * s_re + xim_ref[:, rows, :]
        sre_ref[:, rows, :] = n_re
        sim_ref[:, rows, :] = n_im
        return n_re, n_im

    s_re, s_im = lax.fori_loop(0, tc, step, (st_ref[0], st_ref[1]), unroll=8)
    st_ref[0] = s_re
    st_ref[1] = s_im
    for h in range(2):
        s_r = jnp.concatenate([sre_ref[j, pl.ds(h, hrows, stride=2), :] for j in range(nstate)], axis=1)
        s_i = jnp.concatenate([sim_ref[j, pl.ds(h, hrows, stride=2), :] for j in range(nstate)], axis=1)
        u_h = jnp.concatenate([uh_ref[h, hf] for hf in range(nslab)], axis=1)
        y = _gelu(_dot(s_r.astype(BF16), cre_ref[h]) - _dot(s_i.astype(BF16), cim_ref[h]) + d_ref[h] * u_h)
        for hf in range(nslab):
            yo_ref[h, hf] = y[:, hf * LANES:(hf + 1) * LANES]
    for b in range(nb):
        for h in range(2):
            for hf in range(nslab):
                y_ref[:, seq_lanes(b, h, hf)] = yo_ref[h, hf, pl.ds(b, tc, stride=nb), :].astype(BF16)


def _ssm_mixer(u_scan, batch, abar_re, abar_im, bbar_re_t, bbar_im_t, c_re, c_im, d_skip):
    l = u_scan.shape[0]
    g, p, n = bbar_re_t.shape
    hg = g // 2
    hw = hg * p
    hs = hg * n
    assert batch * 2 == SUBLANES

    nstate = hs // LANES

    def half_b(bt):
        return jnp.stack([_block_diag(bt[:hg]), _block_diag(bt[hg:])]).astype(BF16)

    def half_c(c):
        ct = jnp.transpose(c, (0, 2, 1))
        return jnp.stack([_block_diag(ct[:hg]), _block_diag(ct[hg:])]).astype(BF16)

    def tile_a(a):
        a = jnp.transpose(a.reshape(2, nstate, LANES), (1, 0, 2))
        return jnp.broadcast_to(a[:, None], (nstate, batch, 2, LANES)).reshape(nstate, SUBLANES, LANES)

    tc = 128
    rows = tc * SUBLANES
    kern = functools.partial(_ssm_kernel, tc=tc, hw=hw)
    const = lambda i: (0, 0, 0)
    state = pltpu.VMEM((nstate, rows, LANES), F32)
    halves = pltpu.VMEM((2, hw // LANES, rows // 2, LANES), F32)
    return pl.pallas_call(
        kern, grid=(l // tc,),
        in_specs=[pl.BlockSpec((tc, SUBLANES * hw), lambda i: (i, 0)),
                  pl.BlockSpec((2, hw, hs), const), pl.BlockSpec((2, hw, hs), const),
                  pl.BlockSpec((nstate, SUBLANES, LANES), const), pl.BlockSpec((nstate, SUBLANES, LANES), const),
                  pl.BlockSpec((2, hs, hw), const), pl.BlockSpec((2, hs, hw), const),
                  pl.BlockSpec((2, 1, hw), const)],
        out_specs=pl.BlockSpec((tc, SUBLANES * hw), lambda i: (i, 0)),
        out_shape=jax.ShapeDtypeStruct((l, SUBLANES * hw), BF16),
        scratch_shapes=[pltpu.VMEM((2, nstate, SUBLANES, LANES), F32), state, state, state, state, halves, halves],
        compiler_params=_cparams(("arbitrary",)), name="ssm_mixer",
    )(u_scan, half_b(bbar_re_t), half_b(bbar_im_t), tile_a(abar_re), tile_a(abar_im),
      half_c(c_re), half_c(c_im), d_skip.reshape(2, 1, hw))


def _attn_group(q_ref, k_ref, v_ref, kb, vb, acc, mrun, drun, og, mg, lg, *, d, ta, i):
    p = ATTN_BLOCK * d
    ncombo = ta // ATTN_BLOCK
    nslab = q_ref.shape[1]
    scale = HEAD_DIM ** -0.5
    kb[:, p:p + ta, :] = k_ref[0]
    vb[:, p:p + ta, :] = v_ref[0]

    @pl.when(i == 0)
    def _():
        kb[:, :p, :] = jnp.zeros((nslab, p, LANES), F32)
        vb[:, :p, :] = jnp.zeros((nslab, p, LANES), F32)

    nh = LANES // HEAD_DIM
    qi = lax.broadcasted_iota(jnp.int32, (nh * ATTN_BLOCK, ATTN_BLOCK), 0) % ATTN_BLOCK
    kk = lax.broadcasted_iota(jnp.int32, (nh * ATTN_BLOCK, ATTN_BLOCK), 1)
    mask_cur = kk <= qi
    mask_prev = kk >= qi
    low = lax.broadcasted_iota(jnp.int32, (ATTN_BLOCK, ATTN_BLOCK), 1) < HEAD_DIM

    def rows_at(start):
        return pl.ds(start, ATTN_BLOCK) if d == 1 else pl.ds(start, ATTN_BLOCK, stride=d)

    def combo(c, carry):
        r = c % d
        n = c // d
        qs = n * p + r
        has_prev = jnp.logical_or(i > 0, n > 0)
        mask_p = jnp.logical_and(mask_prev, has_prev)
        for s in range(nslab):
            q = q_ref[0, s, rows_at(qs), :] * scale
            kc = kb[s, rows_at(p + qs), :].astype(BF16)
            kp = kb[s, rows_at(qs), :].astype(BF16)
            vc = vb[s, rows_at(p + qs), :].astype(BF16)
            vp = vb[s, rows_at(qs), :].astype(BF16)
            qh = jnp.concatenate([jnp.where(low, q, 0.0), jnp.where(low, 0.0, q)], axis=0).astype(BF16)
            sc = jnp.where(mask_cur, _dot_nt(qh, kc), -jnp.inf)
            sp = jnp.where(mask_p, _dot_nt(qh, kp), -jnp.inf)
            m = jnp.max(jnp.maximum(sc, sp), axis=1, keepdims=True)
            pc = jnp.exp(sc - m)
            pp = jnp.exp(sp - m)
            den = jnp.sum(pc + pp, axis=1, keepdims=True)
            o = _dot(pc.astype(BF16), vc) + _dot(pp.astype(BF16), vp)
            rows = rows_at(qs)
            og[s, rows, :] = jnp.where(low, o[:ATTN_BLOCK], o[ATTN_BLOCK:])
            mg[s, rows, :] = jnp.where(low, m[:ATTN_BLOCK], m[ATTN_BLOCK:])
            lg[s, rows, :] = jnp.where(low, den[:ATTN_BLOCK], den[ATTN_BLOCK:])
        return carry

    lax.fori_loop(0, ncombo, combo, 0, unroll=4 if d < max(ATTN_DILATIONS) else 2)
    kb[:, :p, :] = kb[:, ta:ta + p, :]
    vb[:, :p, :] = vb[:, ta:ta + p, :]

    rc = 256
    for s in range(nslab):
        for r0 in range(0, ta, rc):
            rs = slice(r0, r0 + rc)
            m_old = mrun[s, rs, :]
            m_blk = mg[s, rs, :]
            m_new = jnp.maximum(m_old, m_blk)
            a_old = jnp.exp(m_old - m_new)
            a_blk = jnp.exp(m_blk - m_new)
            acc[s, rs, :] = acc[s, rs, :] * a_old + og[s, rs, :] * a_blk
            drun[s, rs, :] = drun[s, rs, :] * a_old + lg[s, rs, :] * a_blk
            mrun[s, rs, :] = m_new


def _attn_kernel(q_ref, k_ref, v_ref, o_ref, *scratch, ta):
    ng = len(ATTN_DILATIONS)
    kbs = scratch[0:2 * ng:2]
    vbs = scratch[1:2 * ng:2]
    acc, mrun, drun, og, mg, lg = scratch[2 * ng:]
    i = pl.program_id(1)
    g = pl.program_id(2)

    @pl.when(g == 0)
    def _():
        acc[...] = jnp.zeros_like(acc)
        drun[...] = jnp.zeros_like(drun)
        mrun[...] = jnp.full(mrun.shape, -jnp.inf, F32)

    for gi, d in enumerate(ATTN_DILATIONS):
        @pl.when(g == gi)
        def _(gi=gi, d=d):
            _attn_group(q_ref, k_ref, v_ref, kbs[gi], vbs[gi], acc, mrun, drun, og, mg, lg, d=d, ta=ta, i=i)

    @pl.when(g == ng - 1)
    def _():
        for s in range(acc.shape[0]):
            o_ref[0, :, s * LANES:(s + 1) * LANES] = acc[s] / drun[s]


def _attention(q, k, v):
    b, nslab, l, _ = q.shape
    ng = len(ATTN_DILATIONS)
    gs = nslab // ng
    ta = ATTN_BLOCK * max(ATTN_DILATIONS)
    assert l % ta == 0
    blk = pl.BlockSpec((1, gs, ta, LANES), lambda bi, i, g: (bi, g, i, 0))
    scratch = []
    for d in ATTN_DILATIONS:
        scratch += [pltpu.VMEM((gs, ATTN_BLOCK * d + ta, LANES), F32)] * 2
    scratch += [pltpu.VMEM((gs, ta, LANES), F32)] * 6
    return pl.pallas_call(
        functools.partial(_attn_kernel, ta=ta), grid=(b, l // ta, ng),
        in_specs=[blk, blk, blk],
        out_specs=pl.BlockSpec((1, ta, gs * LANES), lambda bi, i, g: (bi, i, 0)),
        out_shape=jax.ShapeDtypeStruct((b, l, gs * LANES), F32),
        scratch_shapes=scratch,
        compiler_params=_cparams(("arbitrary", "arbitrary", "arbitrary")), name="dilated_attention",
    )(q, k, v)


def _merge_kernel(pm_ref, ys_ref, ao_ref, gate_ref, x_ref, wp_ref, wg_ref, ws_ref, wa_ref, wo_ref, g2_ref,
                  xo_ref, h2_ref, *, d_model, ssm_w):
    def weight(ref):
        return pltpu.bitcast(ref[...], BF16)

    y_pool = _dot(pm_ref[0], weight(wp_ref))
    glu = _dot(ys_ref[...], weight(wg_ref))
    sg = (glu[:, :ssm_w] * _sigmoid(glu[:, ssm_w:])).astype(BF16)
    y_ssm = _dot(sg, weight(ws_ref))
    y_attn = _dot(ao_ref[0].astype(BF16), weight(wa_ref))
    gate = pltpu.bitcast(gate_ref[0], BF16).astype(F32)
    merged = (gate[:, :d_model] * y_pool + gate[:, d_model:2 * d_model] * y_ssm
              + gate[:, 2 * d_model:] * y_attn)
    xn = x_ref[0] + _dot(merged.astype(BF16), weight(wo_ref))
    xo_ref[0] = xn
    h2_ref[...] = _rmsnorm(xn, g2_ref[...]).T.astype(BF16)


def _merge(pm, ys, ao, gates, x, w_pool, w_glu, w_ssm, w_attn, w_out, g2):
    b, l, d = x.shape
    tm = 512
    ssm_w = 2 * w_glu.shape[0]
    row = lambda bi, i: (bi, i, 0)
    const = lambda bi, i: (0, 0)
    full = lambda a: pl.BlockSpec(a.shape, const)
    return pl.pallas_call(
        functools.partial(_merge_kernel, d_model=d, ssm_w=ssm_w), grid=(b, l // tm),
        in_specs=[pl.BlockSpec((1, tm, pm.shape[-1]), row),
                  pl.BlockSpec((tm, ssm_w), lambda bi, i: (i, bi)),
                  pl.BlockSpec((1, tm, ao.shape[-1]), row),
                  pl.BlockSpec((1, tm // 2, 3 * d), row),
                  pl.BlockSpec((1, tm, d), row),
                  full(w_pool), full(w_glu), full(w_ssm), full(w_attn), full(w_out),
                  pl.BlockSpec((1, d), const)],
        out_specs=[pl.BlockSpec((1, tm, d), row),
                   pl.BlockSpec((d, tm), lambda bi, i: (0, bi * (l // tm) + i))],
        out_shape=[jax.ShapeDtypeStruct((b, l, d), F32), jax.ShapeDtypeStruct((d, b * l), BF16)],
        compiler_params=_cparams(("parallel", "parallel")), name="branch_merge",
    )(pm, ys, ao, gates, x, w_pool, w_glu, w_ssm, w_attn, w_out, g2.reshape(1, d))


def _candidate_pairs(k):
    return [(a, b) for a in range(k) for b in range(k) if (a + 1) * (b + 1) <= k]


_CODE_UNIT = 2.0 ** 121
_CODE_BASE = 80


def _removed_code(r):
    return -float(_CODE_BASE + r) * _CODE_UNIT


def _removed_round(w):
    return w * (-1.0 / _CODE_UNIT) - float(_CODE_BASE)


def _vreg_rows(x):
    return x.reshape(x.shape[0] // SUBLANES, SUBLANES, x.shape[1])


def _max_all_rows(w):
    p = jnp.max(w, axis=0)
    for sh in (4, 2, 1):
        p = jnp.maximum(p, pltpu.roll(p, sh, axis=0))
    return p


def _topk_round(work, rows):
    m = jnp.max(work, axis=0, keepdims=True)
    idx = jnp.min(jnp.where(work == m, rows, float(work.shape[0])), axis=0, keepdims=True)
    hit = rows == idx
    return m, hit, jnp.where(hit, -jnp.inf, work)


def _route_kernel(h_ref, wq_ref, k1_ref, k2_ref, r2_ref, e2_ref, n1_ref, e1_ref,
                  qt_ref, s1_ref, s2_ref, v1_ref, v2_ref, cand_ref, sel_ref, tied_ref,
                  *, nheads, nkeys, topk, tt):
    qt_ref[...] = _dot(pltpu.bitcast(wq_ref[...], BF16), h_ref[...])
    hc = k1_ref.shape[1]
    pairs = _candidate_pairs(topk)
    ncand = cand_ref.shape[1]

    def head(hd, carry):
        base = pl.multiple_of(hd * 2 * hc, 2 * hc)
        q1 = qt_ref[pl.ds(base, hc), :].astype(BF16)
        q2 = qt_ref[pl.ds(base + hc, hc), :].astype(BF16)
        s1_ref[...] = _dot(k1_ref[...], q1)
        s2_ref[...] = _dot(k2_ref[...], q2)

        def column(j, exact):
            if isinstance(j, int):
                cols = slice(j * LANES, (j + 1) * LANES)
            else:
                cols = pl.ds(pl.multiple_of(j * LANES, LANES), LANES)
            s1 = s1_ref[:, cols]
            s2 = s2_ref[:, cols]
            v1, v2, cand, selr = v1_ref.at[j], v2_ref.at[j], cand_ref.at[j], sel_ref.at[j]

            def build_candidates():
                for ci, (a, bq) in enumerate(pairs):
                    cand[ci:ci + 1, :] = v1[a:a + 1, :] + v2[bq:bq + 1, :]
                if ncand > len(pairs):
                    cand[len(pairs):, :] = jnp.full((ncand - len(pairs), LANES), -jnp.inf, F32)
                return cand[...]

            def finish(rank1, rank2, sel, cv):
                selr[...] = sel
                top = v1[0:1, :] + v2[0:1, :]
                z = jnp.sum(sel * jnp.exp(jnp.where(sel > 0.0, cv, top) - top), axis=0, keepdims=True)
                n1 = jnp.zeros(s1.shape, F32)
                off = 0
                for a in range(topk):
                    cnt = topk // (a + 1)
                    n_a = jnp.sum(selr[off:off + cnt, :], axis=0, keepdims=True)
                    n1 = jnp.where(rank1 == float(a), n_a, n1)
                    off += cnt
                n1_ref[hd, :, cols] = n1
                e1_ref[hd, :, cols] = jnp.exp(s1 - v1[0:1, :]) / z
                r2_ref[hd, j] = pltpu.bitcast(rank2.astype(BF16), jnp.uint32)
                e2_ref[hd, j] = pltpu.bitcast(jnp.exp(s2 - v2[0:1, :]).astype(BF16), jnp.uint32)

            if exact:
                key_rows = lax.broadcasted_iota(jnp.int32, s1.shape, 0).astype(F32)
                x1, x2 = s1, s2
                rk1 = jnp.full(s1.shape, float(nkeys), F32)
                rk2 = rk1
                for r in range(topk):
                    m1, hit1, x1 = _topk_round(x1, key_rows)
                    m2, hit2, x2 = _topk_round(x2, key_rows)
                    rk1 = jnp.where(hit1, float(r), rk1)
                    rk2 = jnp.where(hit2, float(r), rk2)
                    v1[r:r + 1, :] = m1
                    v2[r:r + 1, :] = m2
                cvs = build_candidates()
                cand_rows = lax.broadcasted_iota(jnp.int32, cvs.shape, 0).astype(F32)
                xc = cvs
                sl = jnp.zeros(cvs.shape, F32)
                for r in range(topk):
                    _, hit, xc = _topk_round(xc, cand_rows)
                    sl = jnp.where(hit, 1.0, sl)
                finish(rk1, rk2, sl, cvs)
                return None

            w1, w2 = _vreg_rows(s1), _vreg_rows(s2)
            for r in range(topk):
                m1 = _max_all_rows(w1)
                m2 = _max_all_rows(w2)
                w1 = jnp.where(w1 == m1[None], _removed_code(r), w1)
                w2 = jnp.where(w2 == m2[None], _removed_code(r), w2)
                v1[r:r + 1, :] = m1[0:1, :]
                v2[r:r + 1, :] = m2[0:1, :]
            w1, w2 = w1.reshape(s1.shape), w2.reshape(s2.shape)
            gone1 = w1 <= _removed_code(0)
            gone2 = w2 <= _removed_code(0)
            rank1 = jnp.where(gone1, _removed_round(w1), float(nkeys))
            rank2 = jnp.where(gone2, _removed_round(w2), float(nkeys))
            cv = build_candidates()
            wc = _vreg_rows(cv)
            for r in range(topk):
                wc = jnp.where(wc == _max_all_rows(wc)[None], _removed_code(r), wc)
            sel = jnp.where(wc.reshape(cv.shape) <= _removed_code(0), 1.0, 0.0)
            removed = (jnp.sum(jnp.where(gone1, 1.0, 0.0), axis=0, keepdims=True)
                       + jnp.sum(jnp.where(gone2, 1.0, 0.0), axis=0, keepdims=True)
                       + jnp.sum(sel, axis=0, keepdims=True))
            finish(rank1, rank2, sel, cv)
            expected = float(3 * topk + ncand - len(pairs))
            return jnp.max(jnp.where(removed != expected, 1.0, 0.0))

        for j in range(tt // LANES):
            tied_ref[j] = column(j, exact=False)

        def redo(j, carry2):
            @pl.when(tied_ref[j] > 0.0)
            def _():
                column(j, exact=True)
            return carry2

        lax.fori_loop(0, tt // LANES, redo, 0)
        return carry

    lax.fori_loop(0, nheads, head, 0)


def _peer_route(h2t, wq_t, k1, k2):
    d, t = h2t.shape
    nkeys, hc = k1.shape
    qrows = 2 * wq_t.shape[0]
    nheads = qrows // (2 * hc)
    tt = 1024
    npairs = len(_candidate_pairs(PEER_TOPK))
    ncand = -(-npairs // SUBLANES) * SUBLANES
    ncol = tt // LANES
    kern = functools.partial(_route_kernel, nheads=nheads, nkeys=nkeys, topk=PEER_TOPK, tt=tt)
    kblk = pl.BlockSpec((nheads, tt // LANES, nkeys // 2, LANES), lambda i: (0, i, 0, 0))
    ksh = jax.ShapeDtypeStruct((nheads, t // LANES, nkeys // 2, LANES), jnp.uint32)
    oblk = pl.BlockSpec((nheads, nkeys, tt), lambda i: (0, 0, i))
    osh = jax.ShapeDtypeStruct((nheads, nkeys, t), F32)
    return pl.pallas_call(
        kern, grid=(t // tt,),
        in_specs=[pl.BlockSpec((d, tt), lambda i: (0, i)),
                  pl.BlockSpec(wq_t.shape, lambda i: (0, 0)),
                  pl.BlockSpec(k1.shape, lambda i: (0, 0)),
                  pl.BlockSpec(k2.shape, lambda i: (0, 0))],
        out_specs=[kblk, kblk, oblk, oblk], out_shape=[ksh, ksh, osh, osh],
        scratch_shapes=[pltpu.VMEM((qrows, tt), F32),
                        pltpu.VMEM((nkeys, tt), F32), pltpu.VMEM((nkeys, tt), F32),
                        pltpu.VMEM((ncol, PEER_TOPK, LANES), F32), pltpu.VMEM((ncol, PEER_TOPK, LANES), F32),
                        pltpu.VMEM((ncol, ncand, LANES), F32), pltpu.VMEM((ncol, ncand, LANES), F32),
                        pltpu.SMEM((ncol,), F32)],
        compiler_params=_cparams(("parallel",)), name="peer_route",
    )(h2t, wq_t, k1, k2)


def _pack_kernel(x_ref, o_ref, *, transpose):
    x = x_ref[0]
    if transpose:
        x = x.T
    o_ref[...] = pltpu.bitcast(x.astype(BF16), jnp.uint32)


def _pack_table(stacked, layer, *, transpose):
    _, r, c = stacked.shape
    tb = min(r, 512 if c <= 1024 else 128)
    in_spec = pl.BlockSpec((1, tb, c), lambda i: (layer, i, 0))
    if transpose:
        out_spec = pl.BlockSpec((c // 2, tb), lambda i: (0, i))
        out_shape = jax.ShapeDtypeStruct((c // 2, r), jnp.uint32)
    else:
        out_spec = pl.BlockSpec((tb // 2, c), lambda i: (i, 0))
        out_shape = jax.ShapeDtypeStruct((r // 2, c), jnp.uint32)
    return pl.pallas_call(
        functools.partial(_pack_kernel, transpose=transpose), grid=(r // tb,),
        in_specs=[in_spec], out_specs=out_spec, out_shape=out_shape,
        compiler_params=_cparams(("parallel",)), name="pack_table",
    )(stacked)


def _ffn_kernel(h_ref, x_ref, g_ref, u_ref, un_ref, vt_ref, r2_ref, e2_ref, n1_ref, e1_ref, o_ref,
                acc_ref, at_ref, pt_ref, *, nheads, nkeys, cw, final_norm):
    j = pl.program_id(1)
    tt = h_ref.shape[1]
    et = 2 * u_ref.shape[0]
    nhalf = cw // LANES

    def first_pre_activations(table_ref):
        at_ref[0] = _dot(pltpu.bitcast(table_ref[0:at_ref.shape[1] // 2, :], BF16), h_ref[:, 0:cw])

    @pl.when(j == 0)
    def _():
        acc_ref[...] = jnp.zeros_like(acc_ref)
        first_pre_activations(u_ref)

    ktiles = nkeys // BF16_ROWS
    es = at_ref.shape[1]
    nsub = es // nkeys
    units = [(eh, c) for eh in range(et // es) for c in range(tt // cw)]
    n_at, n_pt = at_ref.shape[0], pt_ref.shape[0]

    def at_slot(k):
        return 0 if k == 0 else 1 + (k - 1) % (n_at - 1)

    def pre_activations(k):
        eh, c = units[k]
        u = pltpu.bitcast(u_ref[eh * es // 2:(eh + 1) * es // 2, :], BF16)
        at_ref[at_slot(k)] = _dot(u, h_ref[:, c * cw:(c + 1) * cw])

    for k, (eh, c) in enumerate(units):
        if k + 1 < len(units):
            pre_activations(k + 1)
        else:
            first_pre_activations(un_ref)
        for jj in range(nsub):
            row = eh * nsub + jj
            for hf in range(nhalf):
                ck = c * nhalf + hf
                cols = slice(ck * LANES, (ck + 1) * LANES)
                lanes = slice(hf * LANES, (hf + 1) * LANES)
                gate = jnp.zeros((ktiles, BF16_ROWS, LANES), BF16)
                for hd in range(nheads):
                    n_row = jnp.broadcast_to(n1_ref[hd, row:row + 1, cols], (BF16_ROWS, LANES)).astype(BF16)
                    e_row = jnp.broadcast_to(e1_ref[hd, row:row + 1, cols], (BF16_ROWS, LANES)).astype(BF16)
                    r2 = pltpu.bitcast(r2_ref[hd, ck], BF16).reshape(ktiles, BF16_ROWS, LANES)
                    e2 = pltpu.bitcast(e2_ref[hd, ck], BF16).reshape(ktiles, BF16_ROWS, LANES)
                    gate = gate + jnp.where(r2 < n_row[None], e2, 0) * e_row[None]
                a = at_ref[at_slot(k), jj * nkeys:(jj + 1) * nkeys, lanes].astype(BF16)
                p = (gate * _gelu(a.reshape(ktiles, BF16_ROWS, LANES))).reshape(nkeys, LANES)
                pt_ref[k % n_pt, jj * nkeys // 2:(jj + 1) * nkeys // 2, lanes] = pltpu.bitcast(p, jnp.uint32)
        p = pltpu.bitcast(pt_ref[k % n_pt], BF16)
        vt = pltpu.bitcast(vt_ref[:, eh * es:(eh + 1) * es], BF16)
        acc_ref[c] += _dot(vt, p)

    @pl.when(j == pl.num_programs(1) - 1)
    def _():
        for c in range(tt // cw):
            rows = slice(c * cw, (c + 1) * cw)
            y = x_ref[rows, :] + acc_ref[c].T
            o_ref[rows, :] = _rmsnorm(y, g_ref[...]) if final_norm else y


def _peer_experts(h2t, x, u_tab, v_tab_t, r2, e2, n1, e1, final_g):
    d, t = h2t.shape
    ne = 2 * u_tab.shape[0]
    nheads, nkeys, _ = n1.shape
    tt, et, cw = 1024, 2048, 256
    es = nkeys * SUBLANES
    at_ring, pt_ring = 4, 2
    final_norm = final_g is not None
    g = (final_g if final_norm else jnp.ones((d,), F32)).reshape(1, d)
    kern = functools.partial(_ffn_kernel, nheads=nheads, nkeys=nkeys, cw=cw, final_norm=final_norm)
    full_keys = pl.BlockSpec((nheads, tt // LANES, nkeys // 2, LANES), lambda i, j: (0, i, 0, 0),
                             pipeline_mode=pl.Buffered(1))
    sub_keys = pl.BlockSpec((nheads, et // nkeys, tt), lambda i, j: (0, j, i))
    return pl.pallas_call(
        kern, grid=(t // tt, ne // et),
        in_specs=[pl.BlockSpec((d, tt), lambda i, j: (0, i)),
                  pl.BlockSpec((tt, d), lambda i, j: (i, 0), pipeline_mode=pl.Buffered(1)),
                  pl.BlockSpec((1, d), lambda i, j: (0, 0)),
                  pl.BlockSpec((et // 2, d), lambda i, j: (j, 0)),
                  pl.BlockSpec((es // 2, d), lambda i, j: (jnp.minimum(j + 1, ne // et - 1) * (et // es), 0)),
                  pl.BlockSpec((d // 2, et), lambda i, j: (0, j)),
                  full_keys, full_keys, sub_keys, sub_keys],
        out_specs=pl.BlockSpec((tt, d), lambda i, j: (i, 0)),
        out_shape=jax.ShapeDtypeStruct((t, d), F32),
        scratch_shapes=[pltpu.VMEM((tt // cw, d, cw), F32),
                        pltpu.VMEM((at_ring, es, cw), F32),
                        pltpu.VMEM((pt_ring, es // 2, cw), jnp.uint32)],
        compiler_params=_cparams(("parallel", "arbitrary")), name="peer_experts",
    )(h2t, x, g, u_tab, u_tab, v_tab_t, r2, e2, n1, e1)


def kernel(x, positions, norm1_g, w_in, pool_w, pool_scale, pool_proj, ssm_a_re, ssm_a_im, ssm_log_dt,
           ssm_b_re, ssm_b_im, ssm_c_re, ssm_c_im, ssm_d, ssm_glu, ssm_proj, attn_proj, w_out, norm2_g,
           peer_wq, peer_k1, peer_k2, peer_u, peer_v, final_g):
    b, l, d = x.shape
    depth = w_in.shape[0]
    pool_w_cols = pool_proj.shape[1]
    ssm_w_cols = ssm_proj.shape[1]
    attn_w = len(ATTN_DILATIONS) * HEADS_PER_GROUP * HEAD_DIM
    cos, sin = _rope_tables(positions)
    for layer in range(depth):
        pm, u_ssm, q, k, v, gates = _in_projection(
            x, norm1_g[layer], _pack_table(w_in, layer, transpose=False), cos, sin,
            pool_w[layer].astype(BF16), pool_scale[layer],
            pool_w=pool_w_cols, ssm_w=ssm_w_cols, attn_w=attn_w)
        abar_re, abar_im, bbar_re_t, bbar_im_t = _ssm_params(
            ssm_a_re[layer], ssm_a_im[layer], ssm_log_dt[layer], ssm_b_re[layer], ssm_b_im[layer])
        ys = _ssm_mixer(u_ssm, b, abar_re, abar_im, bbar_re_t, bbar_im_t,
                        ssm_c_re[layer], ssm_c_im[layer], ssm_d[layer])
        ao = _attention(q, k, v)
        packed = [_pack_table(wt, layer, transpose=False)
                  for wt in (pool_proj, ssm_glu, ssm_proj, attn_proj, w_out)]
        x, h2t = _merge(pm, ys, ao, gates, x, *packed, norm2_g[layer])
        r2, e2, n1, e1 = _peer_route(h2t, _pack_table(peer_wq, layer, transpose=True), peer_k1[layer].astype(BF16),
                                     peer_k2[layer].astype(BF16))
        x = _peer_experts(h2t, x.reshape(b * l, d), _pack_table(peer_u, layer, transpose=False),
                          _pack_table(peer_v, layer, transpose=True), r2, e2, n1, e1,
                          final_g if layer == depth - 1 else None).reshape(b, l, d)
    return x
```

```python
import functools
import math

import jax
import jax.numpy as jnp
from jax import lax
from jax.experimental import pallas as pl
from jax.experimental.pallas import tpu as pltpu

F32 = jnp.float32
BF16 = jnp.bfloat16

EPS = 1e-6
POOL_WINDOWS = (2, 4, 8, 16)
POOL_GROUP = 128
HEAD_DIM = 64
HEADS_PER_GROUP = 4
ATTN_DILATIONS = (1, 4, 16)
ATTN_BLOCK = 128
ROPE_THETA = 10000.0
PEER_TOPK = 16

LANES = 128
SUBLANES = 8
BF16_ROWS = 16
VMEM_LIMIT = 56 * 1024 * 1024

ROPE_ROWS = 2048
IN_PROJ_TOKENS = 512
POOL_HALO = 16
SSM_CHUNK_STEPS = 128
ATTN_MERGE_ROWS = 256
MERGE_TOKENS = 512
ROUTE_TOKENS = 1024
EXPERT_TOKENS = 1024
EXPERT_BLOCK = 1024
EXPERT_CHUNK = 256
PACK_ROWS = 512
PACK_ROWS_WIDE = 128


def _cparams(sem):
    return pltpu.CompilerParams(dimension_semantics=sem, vmem_limit_bytes=VMEM_LIMIT)


def _gelu(x):
    c = math.sqrt(2.0 / math.pi)
    return x * (0.5 * (1.0 + jnp.tanh(c * (x + 0.044715 * (x * x * x)))))


def _sigmoid(x):
    return 1.0 / (1.0 + jnp.exp(-x))


def _rmsnorm(x, g):
    ms = jnp.mean(x * x, axis=-1, keepdims=True)
    return x * lax.rsqrt(ms + EPS) * g


def _dot(a, b):
    return jnp.dot(a, b, preferred_element_type=F32)


def _dot_nt(a, b):
    return lax.dot_general(a, b, (((1,), (1,)), ((), ())), preferred_element_type=F32)


def _rope_kernel(pos_ref, invf_ref, sign_ref, cos_ref, sin_ref):
    ang = pos_ref[...].astype(F32) * invf_ref[...]
    cos_ref[...] = jnp.cos(ang)
    sin_ref[...] = jnp.sin(ang) * sign_ref[...]


def _rope_tables(positions):
    b, l = positions.shape
    t = b * l
    half = HEAD_DIM // 2
    inv_freq = ROPE_THETA ** (-jnp.arange(half, dtype=F32) / half)
    invf = jnp.tile(inv_freq, LANES // half).reshape(1, LANES)
    sign = jnp.tile(jnp.concatenate([-jnp.ones((half,), F32), jnp.ones((half,), F32)]),
                    LANES // HEAD_DIM).reshape(1, LANES)
    pos_b = jnp.broadcast_to(positions.reshape(t, 1), (t, LANES))
    tr = ROPE_ROWS
    cos, sin = pl.pallas_call(
        _rope_kernel,
        grid=(t // tr,),
        in_specs=[pl.BlockSpec((tr, LANES), lambda i: (i, 0)),
                  pl.BlockSpec((1, LANES), lambda i: (0, 0)),
                  pl.BlockSpec((1, LANES), lambda i: (0, 0))],
        out_specs=[pl.BlockSpec((tr, LANES), lambda i: (i, 0))] * 2,
        out_shape=[jax.ShapeDtypeStruct((t, LANES), F32)] * 2,
        compiler_params=_cparams(("parallel",)),
        name="rope_tables",
    )(pos_b, invf, sign)
    return cos.reshape(b, l, LANES), sin.reshape(b, l, LANES)


def _pool_mix(cur, prev, t0, w_ref, scale_ref, o_ref):
    tm, halo = cur.shape[0], prev.shape[0]
    ext = jnp.concatenate([prev, cur], axis=0)
    t_idx = t0 + lax.broadcasted_iota(jnp.int32, (tm, POOL_GROUP), 0)
    for gi, win in enumerate(POOL_WINDOWS):
        cs = slice(gi * POOL_GROUP, (gi + 1) * POOL_GROUP)
        s = ext[:, cs]
        sh = 1
        while sh < win:
            s = s + pltpu.roll(s, sh, axis=0)
            sh *= 2
        count = jnp.minimum(t_idx + 1, win).astype(F32)
        pooled = s[halo:, :] / count - cur[:, cs]
        mixed = _dot(pooled.astype(BF16), w_ref[gi]) * scale_ref[:, cs]
        o_ref[0, :, cs] = mixed.astype(BF16)


def _inproj_kernel(x_ref, g_ref, w_ref, cos_ref, sin_ref, pw_ref, ps_ref,
                   pool_ref, ssm_ref, q_ref, k_ref, v_ref, gate_ref, halo_ref, *, pool_w, ssm_w, attn_w, d_model):
    i = pl.program_id(1)
    tm = x_ref.shape[1]
    h = _rmsnorm(x_ref[0], g_ref[...]).astype(BF16)
    c0 = 0

    def w_cols(lo, hi):
        return pltpu.bitcast(w_ref[:, lo:hi], BF16)

    @pl.when(i == 0)
    def _():
        halo_ref[...] = jnp.zeros_like(halo_ref)

    z = _dot(h, w_cols(c0, c0 + pool_w + ssm_w))
    u_pool = z[:, :pool_w]
    _pool_mix(u_pool, halo_ref[...], i * tm, pw_ref, ps_ref, pool_ref)
    halo_ref[...] = u_pool[tm - halo_ref.shape[0]:, :]
    ssm_ref[...] = z[:, pool_w:]
    c0 += pool_w + ssm_w
    cos = cos_ref[0]
    sin = sin_ref[0]
    nslab = attn_w // LANES
    half = HEAD_DIM // 2
    first_half = lax.broadcasted_iota(jnp.int32, cos.shape, 1) % HEAD_DIM < half
    for dst in (q_ref, k_ref):
        z = _dot(h, w_cols(c0, c0 + attn_w))
        for s in range(nslab):
            a = z[:, s * LANES:(s + 1) * LANES]
            a_sw = jnp.where(first_half, pltpu.roll(a, LANES - half, axis=1), pltpu.roll(a, half, axis=1))
            dst[0, s] = a * cos + a_sw * sin
        c0 += attn_w
    z = _dot(h, w_cols(c0, c0 + attn_w))
    for s in range(nslab):
        v_ref[0, s] = z[:, s * LANES:(s + 1) * LANES]
    c0 += attn_w
    z = _dot(h, w_cols(c0, c0 + 3 * d_model))
    gate_ref[0] = pltpu.bitcast(_sigmoid(z).astype(BF16), jnp.uint32)


def _in_projection(x, g, w_packed, cos, sin, pool_mix_w, pool_scale, *, pool_w, ssm_w, attn_w):
    b, l, d = x.shape
    tm, halo = IN_PROJ_TOKENS, POOL_HALO
    assert halo >= max(POOL_WINDOWS) and halo % SUBLANES == 0
    nslab = attn_w // LANES
    kern = functools.partial(_inproj_kernel, pool_w=pool_w, ssm_w=ssm_w, attn_w=attn_w, d_model=d)
    row = lambda bi, i: (bi, i, 0)
    slab = lambda bi, i: (bi, 0, i, 0)
    const2 = lambda bi, i: (0, 0)
    in_specs = [pl.BlockSpec((1, tm, d), row),
                pl.BlockSpec((1, d), const2),
                pl.BlockSpec(w_packed.shape, const2, pipeline_mode=pl.Buffered(1)),
                pl.BlockSpec((1, tm, LANES), row),
                pl.BlockSpec((1, tm, LANES), row),
                pl.BlockSpec(pool_mix_w.shape, lambda bi, i: (0, 0, 0)),
                pl.BlockSpec((1, pool_w), const2)]
    args = [x, g.reshape(1, d), w_packed, cos, sin, pool_mix_w, pool_scale.reshape(1, pool_w)]
    out_specs = [pl.BlockSpec((1, tm, pool_w), row),
                 pl.BlockSpec((tm, ssm_w), lambda bi, i: (i, bi)),
                 pl.BlockSpec((1, nslab, tm, LANES), slab),
                 pl.BlockSpec((1, nslab, tm, LANES), slab),
                 pl.BlockSpec((1, nslab, tm, LANES), slab),
                 pl.BlockSpec((1, tm // 2, 3 * d), row)]
    out_shape = [jax.ShapeDtypeStruct((b, l, pool_w), BF16),
                 jax.ShapeDtypeStruct((l, b * ssm_w), F32),
                 jax.ShapeDtypeStruct((b, nslab, l, LANES), F32),
                 jax.ShapeDtypeStruct((b, nslab, l, LANES), F32),
                 jax.ShapeDtypeStruct((b, nslab, l, LANES), F32),
                 jax.ShapeDtypeStruct((b, l // 2, 3 * d), jnp.uint32)]
    return pl.pallas_call(
        kern, grid=(b, l // tm), in_specs=in_specs, out_specs=out_specs, out_shape=out_shape,
        scratch_shapes=[pltpu.VMEM((halo, pool_w), F32)],
        compiler_params=_cparams(("arbitrary", "arbitrary")), name="in_projection",
    )(*args)


def _ssm_param_kernel(are_ref, aim_ref, ldt_ref, bre_ref, bim_ref, oar_ref, oai_ref, obr_ref, obi_ref):
    ar = are_ref[...]
    ai = aim_ref[...]
    dt = jnp.exp(ldt_ref[...])
    decay = jnp.exp(ar * dt)
    abar_re = decay * jnp.cos(ai * dt)
    abar_im = decay * jnp.sin(ai * dt)
    den = ar * ar + ai * ai
    nr = abar_re - 1.0
    k_re = (nr * ar + abar_im * ai) / den
    k_im = (abar_im * ar - nr * ai) / den
    oar_ref[...] = abar_re
    oai_ref[...] = abar_im
    for gi in range(are_ref.shape[0]):
        kr = k_re[gi:gi + 1, :]
        ki = k_im[gi:gi + 1, :]
        br = bre_ref[gi]
        bi = bim_ref[gi]
        obr_ref[gi] = kr * br - ki * bi
        obi_ref[gi] = kr * bi + ki * br


def _ssm_params(a_re, a_im, log_dt, b_re, b_im):
    g, n, p = b_re.shape
    brt = jnp.transpose(b_re, (0, 2, 1))
    bit = jnp.transpose(b_im, (0, 2, 1))
    return pl.pallas_call(
        _ssm_param_kernel,
        out_shape=[jax.ShapeDtypeStruct((g, n), F32)] * 2 + [jax.ShapeDtypeStruct((g, p, n), F32)] * 2,
        name="ssm_params",
    )(a_re, a_im, log_dt.reshape(g, 1), brt, bit)


def _block_diag(m):
    k, r, c = m.shape
    eye = jnp.eye(k, dtype=m.dtype)
    return (m[:, :, None, :] * eye[:, None, :, None]).reshape(k * r, k * c)


def _ssm_kernel(u_ref, bre_ref, bim_ref, are_ref, aim_ref, cre_ref, cim_ref, d_ref, y_ref,
                st_ref, xre_ref, xim_ref, sre_ref, sim_ref, uh_ref, yo_ref, *, tc, hw):
    nb = SUBLANES // 2
    hrows = tc * nb
    nslab = hw // LANES
    nstate = xre_ref.shape[0]

    @pl.when(pl.program_id(0) == 0)
    def _():
        st_ref[...] = jnp.zeros_like(st_ref)

    def seq_lanes(b, h, hf):
        s = b * 2 + h
        return slice(s * hw + hf * LANES, s * hw + (hf + 1) * LANES)

    for b in range(nb):
        for h in range(2):
            for hf in range(nslab):
                uh_ref[h, hf, pl.ds(b, tc, stride=nb), :] = u_ref[:, seq_lanes(b, h, hf)]
    for h in range(2):
        ub = jnp.concatenate([uh_ref[h, hf] for hf in range(nslab)], axis=1).astype(BF16)
        x_re = _dot(ub, bre_ref[h])
        x_im = _dot(ub, bim_ref[h])
        for j in range(nstate):
            xre_ref[j, pl.ds(h, hrows, stride=2), :] = x_re[:, j * LANES:(j + 1) * LANES]
            xim_ref[j, pl.ds(h, hrows, stride=2), :] = x_im[:, j * LANES:(j + 1) * LANES]
    a_re = are_ref[...]
    a_im = aim_ref[...]

    def step(t, carry):
        s_re, s_im = carry
        rows = pl.ds(pl.multiple_of(t * SUBLANES, SUBLANES), SUBLANES)
        n_re = a_re * s_re - a_im * s_im + xre_ref[:, rows, :]
        n_im = a_re * s_im + a_im * s_re + xim_ref[:, rows, :]
        sre_ref[:, rows, :] = n_re
        sim_ref[:, rows, :] = n_im
        return n_re, n_im

    s_re, s_im = lax.fori_loop(0, tc, step, (st_ref[0], st_ref[1]), unroll=8)
    st_ref[0] = s_re
    st_ref[1] = s_im
    for h in range(2):
        s_r = jnp.concatenate([sre_ref[j, pl.ds(h, hrows, stride=2), :] for j in range(nstate)], axis=1)
        s_i = jnp.concatenate([sim_ref[j, pl.ds(h, hrows, stride=2), :] for j in range(nstate)], axis=1)
        u_h = jnp.concatenate([uh_ref[h, hf] for hf in range(nslab)], axis=1)
        y = _gelu(_dot(s_r.astype(BF16), cre_ref[h]) - _dot(s_i.astype(BF16), cim_ref[h]) + d_ref[h] * u_h)
        for hf in range(nslab):
            yo_ref[h, hf] = y[:, hf * LANES:(hf + 1) * LANES]
    for b in range(nb):
        for h in range(2):
            for hf in range(nslab):
                y_ref[:, seq_lanes(b, h, hf)] = yo_ref[h, hf, pl.ds(b, tc, stride=nb), :].astype(BF16)


def _ssm_mixer(u_scan, batch, abar_re, abar_im, bbar_re_t, bbar_im_t, c_re, c_im, d_skip):
    l = u_scan.shape[0]
    g, p, n = bbar_re_t.shape
    hg = g // 2
    hw = hg * p
    hs = hg * n
    assert batch * 2 == SUBLANES

    nstate = hs // LANES

    def half_b(bt):
        return jnp.stack([_block_diag(bt[:hg]), _block_diag(bt[hg:])]).astype(BF16)

    def half_c(c):
        ct = jnp.transpose(c, (0, 2, 1))
        return jnp.stack([_block_diag(ct[:hg]), _block_diag(ct[hg:])]).astype(BF16)

    def tile_a(a):
        a = jnp.transpose(a.reshape(2, nstate, LANES), (1, 0, 2))
        return jnp.broadcast_to(a[:, None], (nstate, batch, 2, LANES)).reshape(nstate, SUBLANES, LANES)

    tc = SSM_CHUNK_STEPS
    rows = tc * SUBLANES
    kern = functools.partial(_ssm_kernel, tc=tc, hw=hw)
    const = lambda i: (0, 0, 0)
    state = pltpu.VMEM((nstate, rows, LANES), F32)
    halves = pltpu.VMEM((2, hw // LANES, rows // 2, LANES), F32)
    return pl.pallas_call(
        kern, grid=(l // tc,),
        in_specs=[pl.BlockSpec((tc, SUBLANES * hw), lambda i: (i, 0)),
                  pl.BlockSpec((2, hw, hs), const), pl.BlockSpec((2, hw, hs), const),
                  pl.BlockSpec((nstate, SUBLANES, LANES), const), pl.BlockSpec((nstate, SUBLANES, LANES), const),
                  pl.BlockSpec((2, hs, hw), const), pl.BlockSpec((2, hs, hw), const),
                  pl.BlockSpec((2, 1, hw), const)],
        out_specs=pl.BlockSpec((tc, SUBLANES * hw), lambda i: (i, 0)),
        out_shape=jax.ShapeDtypeStruct((l, SUBLANES * hw), BF16),
        scratch_shapes=[pltpu.VMEM((2, nstate, SUBLANES, LANES), F32), state, state, state, state, halves, halves],
        compiler_params=_cparams(("arbitrary",)), name="ssm_mixer",
    )(u_scan, half_b(bbar_re_t), half_b(bbar_im_t), tile_a(abar_re), tile_a(abar_im),
      half_c(c_re), half_c(c_im), d_skip.reshape(2, 1, hw))


def _attn_group(q_ref, k_ref, v_ref, kb, vb, acc, mrun, drun, og, mg, lg, *, d, ta, i):
    p = ATTN_BLOCK * d
    ncombo = ta // ATTN_BLOCK
    nslab = q_ref.shape[1]
    scale = HEAD_DIM ** -0.5
    kb[:, p:p + ta, :] = k_ref[0]
    vb[:, p:p + ta, :] = v_ref[0]

    @pl.when(i == 0)
    def _():
        kb[:, :p, :] = jnp.zeros((nslab, p, LANES), F32)
        vb[:, :p, :] = jnp.zeros((nslab, p, LANES), F32)

    nh = LANES // HEAD_DIM
    qi = lax.broadcasted_iota(jnp.int32, (nh * ATTN_BLOCK, ATTN_BLOCK), 0) % ATTN_BLOCK
    kk = lax.broadcasted_iota(jnp.int32, (nh * ATTN_BLOCK, ATTN_BLOCK), 1)
    mask_cur = kk <= qi
    mask_prev = kk >= qi
    low = lax.broadcasted_iota(jnp.int32, (ATTN_BLOCK, ATTN_BLOCK), 1) < HEAD_DIM

    def rows_at(start):
        return pl.ds(start, ATTN_BLOCK) if d == 1 else pl.ds(start, ATTN_BLOCK, stride=d)

    def combo(c, carry):
        r = c % d
        n = c // d
        qs = n * p + r
        has_prev = jnp.logical_or(i > 0, n > 0)
        mask_p = jnp.logical_and(mask_prev, has_prev)
        for s in range(nslab):
            q = q_ref[0, s, rows_at(qs), :] * scale
            kc = kb[s, rows_at(p + qs), :].astype(BF16)
            kp = kb[s, rows_at(qs), :].astype(BF16)
            vc = vb[s, rows_at(p + qs), :].astype(BF16)
            vp = vb[s, rows_at(qs), :].astype(BF16)
            qh = jnp.concatenate([jnp.where(low, q, 0.0), jnp.where(low, 0.0, q)], axis=0).astype(BF16)
            sc = jnp.where(mask_cur, _dot_nt(qh, kc), -jnp.inf)
            sp = jnp.where(mask_p, _dot_nt(qh, kp), -jnp.inf)
            m = jnp.max(jnp.maximum(sc, sp), axis=1, keepdims=True)
            pc = jnp.exp(sc - m)
            pp = jnp.exp(sp - m)
            den = jnp.sum(pc + pp, axis=1, keepdims=True)
            o = _dot(pc.astype(BF16), vc) + _dot(pp.astype(BF16), vp)
            rows = rows_at(qs)
            og[s, rows, :] = jnp.where(low, o[:ATTN_BLOCK], o[ATTN_BLOCK:])
            mg[s, rows, :] = jnp.where(low, m[:ATTN_BLOCK], m[ATTN_BLOCK:])
            lg[s, rows, :] = jnp.where(low, den[:ATTN_BLOCK], den[ATTN_BLOCK:])
        return carry

    lax.fori_loop(0, ncombo, combo, 0, unroll=4 if d < max(ATTN_DILATIONS) else 2)
    kb[:, :p, :] = kb[:, ta:ta + p, :]
    vb[:, :p, :] = vb[:, ta:ta + p, :]

    rc = ATTN_MERGE_ROWS
    for s in range(nslab):
        for r0 in range(0, ta, rc):
            rs = slice(r0, r0 + rc)
            m_old = mrun[s, rs, :]
            m_blk = mg[s, rs, :]
            m_new = jnp.maximum(m_old, m_blk)
            a_old = jnp.exp(m_old - m_new)
            a_blk = jnp.exp(m_blk - m_new)
            acc[s, rs, :] = acc[s, rs, :] * a_old + og[s, rs, :] * a_blk
            drun[s, rs, :] = drun[s, rs, :] * a_old + lg[s, rs, :] * a_blk
            mrun[s, rs, :] = m_new


def _attn_kernel(q_ref, k_ref, v_ref, o_ref, *scratch, ta):
    ng = len(ATTN_DILATIONS)
    kbs = scratch[0:2 * ng:2]
    vbs = scratch[1:2 * ng:2]
    acc, mrun, drun, og, mg, lg = scratch[2 * ng:]
    i = pl.program_id(1)
    g = pl.program_id(2)

    @pl.when(g == 0)
    def _():
        acc[...] = jnp.zeros_like(acc)
        drun[...] = jnp.zeros_like(drun)
        mrun[...] = jnp.full(mrun.shape, -jnp.inf, F32)

    for gi, d in enumerate(ATTN_DILATIONS):
        @pl.when(g == gi)
        def _(gi=gi, d=d):
            _attn_group(q_ref, k_ref, v_ref, kbs[gi], vbs[gi], acc, mrun, drun, og, mg, lg, d=d, ta=ta, i=i)

    @pl.when(g == ng - 1)
    def _():
        for s in range(acc.shape[0]):
            o_ref[0, :, s * LANES:(s + 1) * LANES] = acc[s] / drun[s]


def _attention(q, k, v):
    b, nslab, l, _ = q.shape
    ng = len(ATTN_DILATIONS)
    gs = nslab // ng
    ta = ATTN_BLOCK * max(ATTN_DILATIONS)
    assert l % ta == 0
    blk = pl.BlockSpec((1, gs, ta, LANES), lambda bi, i, g: (bi, g, i, 0))
    scratch = []
    for d in ATTN_DILATIONS:
        scratch += [pltpu.VMEM((gs, ATTN_BLOCK * d + ta, LANES), F32)] * 2
    scratch += [pltpu.VMEM((gs, ta, LANES), F32)] * 6
    return pl.pallas_call(
        functools.partial(_attn_kernel, ta=ta), grid=(b, l // ta, ng),
        in_specs=[blk, blk, blk],
        out_specs=pl.BlockSpec((1, ta, gs * LANES), lambda bi, i, g: (bi, i, 0)),
        out_shape=jax.ShapeDtypeStruct((b, l, gs * LANES), F32),
        scratch_shapes=scratch,
        compiler_params=_cparams(("arbitrary", "arbitrary", "arbitrary")), name="dilated_attention",
    )(q, k, v)


def _merge_kernel(pm_ref, ys_ref, ao_ref, gate_ref, x_ref, wp_ref, wg_ref, ws_ref, wa_ref, wo_ref, g2_ref,
                  xo_ref, h2_ref, *, d_model, ssm_w):
    def weight(ref):
        return pltpu.bitcast(ref[...], BF16)

    y_pool = _dot(pm_ref[0], weight(wp_ref))
    glu = _dot(ys_ref[...], weight(wg_ref))
    sg = (glu[:, :ssm_w] * _sigmoid(glu[:, ssm_w:])).astype(BF16)
    y_ssm = _dot(sg, weight(ws_ref))
    y_attn = _dot(ao_ref[0].astype(BF16), weight(wa_ref))
    gate = pltpu.bitcast(gate_ref[0], BF16).astype(F32)
    merged = (gate[:, :d_model] * y_pool + gate[:, d_model:2 * d_model] * y_ssm
              + gate[:, 2 * d_model:] * y_attn)
    xn = x_ref[0] + _dot(merged.astype(BF16), weight(wo_ref))
    xo_ref[0] = xn
    h2_ref[...] = _rmsnorm(xn, g2_ref[...]).T.astype(BF16)


def _merge(pm, ys, ao, gates, x, w_pool, w_glu, w_ssm, w_attn, w_out, g2):
    b, l, d = x.shape
    tm = MERGE_TOKENS
    ssm_w = 2 * w_glu.shape[0]
    row = lambda bi, i: (bi, i, 0)
    const = lambda bi, i: (0, 0)
    full = lambda a: pl.BlockSpec(a.shape, const)
    return pl.pallas_call(
        functools.partial(_merge_kernel, d_model=d, ssm_w=ssm_w), grid=(b, l // tm),
        in_specs=[pl.BlockSpec((1, tm, pm.shape[-1]), row),
                  pl.BlockSpec((tm, ssm_w), lambda bi, i: (i, bi)),
                  pl.BlockSpec((1, tm, ao.shape[-1]), row),
                  pl.BlockSpec((1, tm // 2, 3 * d), row),
                  pl.BlockSpec((1, tm, d), row),
                  full(w_pool), full(w_glu), full(w_ssm), full(w_attn), full(w_out),
                  pl.BlockSpec((1, d), const)],
        out_specs=[pl.BlockSpec((1, tm, d), row),
                   pl.BlockSpec((d, tm), lambda bi, i: (0, bi * (l // tm) + i))],
        out_shape=[jax.ShapeDtypeStruct((b, l, d), F32), jax.ShapeDtypeStruct((d, b * l), BF16)],
        compiler_params=_cparams(("parallel", "parallel")), name="branch_merge",
    )(pm, ys, ao, gates, x, w_pool, w_glu, w_ssm, w_attn, w_out, g2.reshape(1, d))


def _candidate_pairs(k):
    return [(a, b) for a in range(k) for b in range(k) if (a + 1) * (b + 1) <= k]


_CODE_UNIT = 2.0 ** 121
_CODE_BASE = 80


def _removed_code(r):
    return -float(_CODE_BASE + r) * _CODE_UNIT


def _removed_round(w):
    return w * (-1.0 / _CODE_UNIT) - float(_CODE_BASE)


def _vreg_rows(x):
    return x.reshape(x.shape[0] // SUBLANES, SUBLANES, x.shape[1])


def _max_all_rows(w):
    p = jnp.max(w, axis=0)
    for sh in (4, 2, 1):
        p = jnp.maximum(p, pltpu.roll(p, sh, axis=0))
    return p


def _topk_round(work, rows):
    m = jnp.max(work, axis=0, keepdims=True)
    idx = jnp.min(jnp.where(work == m, rows, float(work.shape[0])), axis=0, keepdims=True)
    hit = rows == idx
    return m, hit, jnp.where(hit, -jnp.inf, work)


def _route_kernel(h_ref, wq_ref, k1_ref, k2_ref, r2_ref, e2_ref, n1_ref, e1_ref,
                  qt_ref, s1_ref, s2_ref, v1_ref, v2_ref, cand_ref, sel_ref, tied_ref,
                  *, nheads, nkeys, topk, tt):
    qt_ref[...] = _dot(pltpu.bitcast(wq_ref[...], BF16), h_ref[...])
    hc = k1_ref.shape[1]
    pairs = _candidate_pairs(topk)
    ncand = cand_ref.shape[1]

    def head(hd, carry):
        base = pl.multiple_of(hd * 2 * hc, 2 * hc)
        q1 = qt_ref[pl.ds(base, hc), :].astype(BF16)
        q2 = qt_ref[pl.ds(base + hc, hc), :].astype(BF16)
        s1_ref[...] = _dot(k1_ref[...], q1)
        s2_ref[...] = _dot(k2_ref[...], q2)

        def column(j, exact):
            if isinstance(j, int):
                cols = slice(j * LANES, (j + 1) * LANES)
            else:
                cols = pl.ds(pl.multiple_of(j * LANES, LANES), LANES)
            s1 = s1_ref[:, cols]
            s2 = s2_ref[:, cols]
            v1, v2, cand, selr = v1_ref.at[j], v2_ref.at[j], cand_ref.at[j], sel_ref.at[j]

            def build_candidates():
                for ci, (a, bq) in enumerate(pairs):
                    cand[ci:ci + 1, :] = v1[a:a + 1, :] + v2[bq:bq + 1, :]
                if ncand > len(pairs):
                    cand[len(pairs):, :] = jnp.full((ncand - len(pairs), LANES), -jnp.inf, F32)
                return cand[...]

            def finish(rank1, rank2, sel, cv):
                selr[...] = sel
                top = v1[0:1, :] + v2[0:1, :]
                z = jnp.sum(sel * jnp.exp(jnp.where(sel > 0.0, cv, top) - top), axis=0, keepdims=True)
                n1 = jnp.zeros(s1.shape, F32)
                off = 0
                for a in range(topk):
                    cnt = topk // (a + 1)
                    n_a = jnp.sum(selr[off:off + cnt, :], axis=0, keepdims=True)
                    n1 = jnp.where(rank1 == float(a), n_a, n1)
                    off += cnt
                n1_ref[hd, :, cols] = n1
                e1_ref[hd, :, cols] = jnp.exp(s1 - v1[0:1, :]) / z
                r2_ref[hd, j] = pltpu.bitcast(rank2.astype(BF16), jnp.uint32)
                e2_ref[hd, j] = pltpu.bitcast(jnp.exp(s2 - v2[0:1, :]).astype(BF16), jnp.uint32)

            if exact:
                key_rows = lax.broadcasted_iota(jnp.int32, s1.shape, 0).astype(F32)
                x1, x2 = s1, s2
                rk1 = jnp.full(s1.shape, float(nkeys), F32)
                rk2 = rk1
                for r in range(topk):
                    m1, hit1, x1 = _topk_round(x1, key_rows)
                    m2, hit2, x2 = _topk_round(x2, key_rows)
                    rk1 = jnp.where(hit1, float(r), rk1)
                    rk2 = jnp.where(hit2, float(r), rk2)
                    v1[r:r + 1, :] = m1
                    v2[r:r + 1, :] = m2
                cvs = build_candidates()
                cand_rows = lax.broadcasted_iota(jnp.int32, cvs.shape, 0).astype(F32)
                xc = cvs
                sl = jnp.zeros(cvs.shape, F32)
                for r in range(topk):
                    _, hit, xc = _topk_round(xc, cand_rows)
                    sl = jnp.where(hit, 1.0, sl)
                finish(rk1, rk2, sl, cvs)
                return None

            w1, w2 = _vreg_rows(s1), _vreg_rows(s2)
            for r in range(topk):
                m1 = _max_all_rows(w1)
                m2 = _max_all_rows(w2)
                w1 = jnp.where(w1 == m1[None], _removed_code(r), w1)
                w2 = jnp.where(w2 == m2[None], _removed_code(r), w2)
                v1[r:r + 1, :] = m1[0:1, :]
                v2[r:r + 1, :] = m2[0:1, :]
            w1, w2 = w1.reshape(s1.shape), w2.reshape(s2.shape)
            gone1 = w1 <= _removed_code(0)
            gone2 = w2 <= _removed_code(0)
            rank1 = jnp.where(gone1, _removed_round(w1), float(nkeys))
            rank2 = jnp.where(gone2, _removed_round(w2), float(nkeys))
            cv = build_candidates()
            wc = _vreg_rows(cv)
            for r in range(topk):
                wc = jnp.where(wc == _max_all_rows(wc)[None], _removed_code(r), wc)
            sel = jnp.where(wc.reshape(cv.shape) <= _removed_code(0), 1.0, 0.0)
            removed = (jnp.sum(jnp.where(gone1, 1.0, 0.0), axis=0, keepdims=True)
                       + jnp.sum(jnp.where(gone2, 1.0, 0.0), axis=0, keepdims=True)
                       + jnp.sum(sel, axis=0, keepdims=True))
            finish(rank1, rank2, sel, cv)
            expected = float(3 * topk + ncand - len(pairs))
            return jnp.max(jnp.where(removed != expected, 1.0, 0.0))

        for j in range(tt // LANES):
            tied_ref[j] = column(j, exact=False)

        def redo(j, carry2):
            @pl.when(tied_ref[j] > 0.0)
            def _():
                column(j, exact=True)
            return carry2

        lax.fori_loop(0, tt // LANES, redo, 0)
        return carry

    lax.fori_loop(0, nheads, head, 0)


def _peer_route(h2t, wq_t, k1, k2):
    d, t = h2t.shape
    nkeys, hc = k1.shape
    qrows = 2 * wq_t.shape[0]
    nheads = qrows // (2 * hc)
    tt = ROUTE_TOKENS
    npairs = len(_candidate_pairs(PEER_TOPK))
    ncand = -(-npairs // SUBLANES) * SUBLANES
    ncol = tt // LANES
    kern = functools.partial(_route_kernel, nheads=nheads, nkeys=nkeys, topk=PEER_TOPK, tt=tt)
    kblk = pl.BlockSpec((nheads, tt // LANES, nkeys // 2, LANES), lambda i: (0, i, 0, 0))
    ksh = jax.ShapeDtypeStruct((nheads, t // LANES, nkeys // 2, LANES), jnp.uint32)
    oblk = pl.BlockSpec((nheads, nkeys, tt), lambda i: (0, 0, i))
    osh = jax.ShapeDtypeStruct((nheads, nkeys, t), F32)
    return pl.pallas_call(
        kern, grid=(t // tt,),
        in_specs=[pl.BlockSpec((d, tt), lambda i: (0, i)),
                  pl.BlockSpec(wq_t.shape, lambda i: (0, 0)),
                  pl.BlockSpec(k1.shape, lambda i: (0, 0)),
                  pl.BlockSpec(k2.shape, lambda i: (0, 0))],
        out_specs=[kblk, kblk, oblk, oblk], out_shape=[ksh, ksh, osh, osh],
        scratch_shapes=[pltpu.VMEM((qrows, tt), F32),
                        pltpu.VMEM((nkeys, tt), F32), pltpu.VMEM((nkeys, tt), F32),
                        pltpu.VMEM((ncol, PEER_TOPK, LANES), F32), pltpu.VMEM((ncol, PEER_TOPK, LANES), F32),
                        pltpu.VMEM((ncol, ncand, LANES), F32), pltpu.VMEM((ncol, ncand, LANES), F32),
                        pltpu.SMEM((ncol,), F32)],
        compiler_params=_cparams(("parallel",)), name="peer_route",
    )(h2t, wq_t, k1, k2)


def _pack_kernel(x_ref, o_ref, *, transpose):
    x = x_ref[0]
    if transpose:
        x = x.T
    o_ref[...] = pltpu.bitcast(x.astype(BF16), jnp.uint32)


def _pack_table(stacked, layer, *, transpose):
    _, r, c = stacked.shape
    tb = min(r, PACK_ROWS if c <= 1024 else PACK_ROWS_WIDE)
    in_spec = pl.BlockSpec((1, tb, c), lambda i: (layer, i, 0))
    if transpose:
        out_spec = pl.BlockSpec((c // 2, tb), lambda i: (0, i))
        out_shape = jax.ShapeDtypeStruct((c // 2, r), jnp.uint32)
    else:
        out_spec = pl.BlockSpec((tb // 2, c), lambda i: (i, 0))
        out_shape = jax.ShapeDtypeStruct((r // 2, c), jnp.uint32)
    return pl.pallas_call(
        functools.partial(_pack_kernel, transpose=transpose), grid=(r // tb,),
        in_specs=[in_spec], out_specs=out_spec, out_shape=out_shape,
        compiler_params=_cparams(("parallel",)), name="pack_table",
    )(stacked)


def _ffn_kernel(h_ref, x_ref, g_ref, u_ref, un_ref, vt_ref, r2_ref, e2_ref, n1_ref, e1_ref, o_ref,
                acc_ref, at_ref, pt_ref, *, nheads, nkeys, cw, final_norm):
    j = pl.program_id(1)
    tt = h_ref.shape[1]
    et = 2 * u_ref.shape[0]
    nhalf = cw // LANES

    def first_pre_activations(table_ref):
        at_ref[0] = _dot(pltpu.bitcast(table_ref[0:at_ref.shape[1] // 2, :], BF16), h_ref[:, 0:cw])

    @pl.when(j == 0)
    def _():
        acc_ref[...] = jnp.zeros_like(acc_ref)
        first_pre_activations(u_ref)

    ktiles = nkeys // BF16_ROWS
    es = at_ref.shape[1]
    nsub = es // nkeys
    units = [(eh, c) for eh in range(et // es) for c in range(tt // cw)]
    n_at, n_pt = at_ref.shape[0], pt_ref.shape[0]

    def pre_activations(k):
        eh, c = units[k]
        u = pltpu.bitcast(u_ref[eh * es // 2:(eh + 1) * es // 2, :], BF16)
        at_ref[k % n_at] = _dot(u, h_ref[:, c * cw:(c + 1) * cw])

    assert n_at >= len(units)
    for k, (eh, c) in enumerate(units):
        if k + 1 < len(units):
            pre_activations(k + 1)
        else:
            first_pre_activations(un_ref)
        for jj in range(nsub):
            row = eh * nsub + jj
            for hf in range(nhalf):
                ck = c * nhalf + hf
                cols = slice(ck * LANES, (ck + 1) * LANES)
                lanes = slice(hf * LANES, (hf + 1) * LANES)
                gate = jnp.zeros((ktiles, BF16_ROWS, LANES), BF16)
                for hd in range(nheads):
                    n_row = jnp.broadcast_to(n1_ref[hd, row:row + 1, cols], (BF16_ROWS, LANES)).astype(BF16)
                    e_row = jnp.broadcast_to(e1_ref[hd, row:row + 1, cols], (BF16_ROWS, LANES)).astype(BF16)
                    r2 = pltpu.bitcast(r2_ref[hd, ck], BF16).reshape(ktiles, BF16_ROWS, LANES)
                    e2 = pltpu.bitcast(e2_ref[hd, ck], BF16).reshape(ktiles, BF16_ROWS, LANES)
                    gate = gate + jnp.where(r2 < n_row[None], e2, 0) * e_row[None]
                a = at_ref[k % n_at, jj * nkeys:(jj + 1) * nkeys, lanes].astype(BF16)
                p = (gate * _gelu(a.reshape(ktiles, BF16_ROWS, LANES))).reshape(nkeys, LANES)
                pt_ref[k % n_pt, jj * nkeys // 2:(jj + 1) * nkeys // 2, lanes] = pltpu.bitcast(p, jnp.uint32)
        p = pltpu.bitcast(pt_ref[k % n_pt], BF16)
        vt = pltpu.bitcast(vt_ref[:, eh * es:(eh + 1) * es], BF16)
        acc_ref[c] += _dot(vt, p)

    @pl.when(j == pl.num_programs(1) - 1)
    def _():
        for c in range(tt // cw):
            rows = slice(c * cw, (c + 1) * cw)
            y = x_ref[rows, :] + acc_ref[c].T
            o_ref[rows, :] = _rmsnorm(y, g_ref[...]) if final_norm else y


def _peer_experts(h2t, x, u_tab, v_tab_t, r2, e2, n1, e1, final_g):
    d, t = h2t.shape
    ne = 2 * u_tab.shape[0]
    nheads, nkeys, _ = n1.shape
    tt, et, cw = EXPERT_TOKENS, EXPERT_BLOCK, EXPERT_CHUNK
    es = nkeys * SUBLANES
    at_ring, pt_ring = 4, 4
    final_norm = final_g is not None
    g = (final_g if final_norm else jnp.ones((d,), F32)).reshape(1, d)
    kern = functools.partial(_ffn_kernel, nheads=nheads, nkeys=nkeys, cw=cw, final_norm=final_norm)
    full_keys = pl.BlockSpec((nheads, tt // LANES, nkeys // 2, LANES), lambda i, j: (0, i, 0, 0))
    sub_keys = pl.BlockSpec((nheads, et // nkeys, tt), lambda i, j: (0, j, i))
    return pl.pallas_call(
        kern, grid=(t // tt, ne // et),
        in_specs=[pl.BlockSpec((d, tt), lambda i, j: (0, i)),
                  pl.BlockSpec((tt, d), lambda i, j: (i, 0)),
                  pl.BlockSpec((1, d), lambda i, j: (0, 0)),
                  pl.BlockSpec((et // 2, d), lambda i, j: (j, 0)),
                  pl.BlockSpec((es // 2, d), lambda i, j: (jnp.minimum(j + 1, ne // et - 1) * (et // es), 0)),
                  pl.BlockSpec((d // 2, et), lambda i, j: (0, j)),
                  full_keys, full_keys, sub_keys, sub_keys],
        out_specs=pl.BlockSpec((tt, d), lambda i, j: (i, 0)),
        out_shape=jax.ShapeDtypeStruct((t, d), F32),
        scratch_shapes=[pltpu.VMEM((tt // cw, d, cw), F32),
                        pltpu.VMEM((at_ring, es, cw), F32),
                        pltpu.VMEM((pt_ring, es // 2, cw), jnp.uint32)],
        compiler_params=_cparams(("parallel", "arbitrary")), name="peer_experts",
    )(h2t, x, g, u_tab, u_tab, v_tab_t, r2, e2, n1, e1)


def kernel(x, positions, norm1_g, w_in, pool_w, pool_scale, pool_proj, ssm_a_re, ssm_a_im, ssm_log_dt,
           ssm_b_re, ssm_b_im, ssm_c_re, ssm_c_im, ssm_d, ssm_glu, ssm_proj, attn_proj, w_out, norm2_g,
           peer_wq, peer_k1, peer_k2, peer_u, peer_v, final_g):
    b, l, d = x.shape
    depth = w_in.shape[0]
    pool_w_cols = pool_proj.shape[1]
    ssm_w_cols = ssm_proj.shape[1]
    attn_w = len(ATTN_DILATIONS) * HEADS_PER_GROUP * HEAD_DIM
    cos, sin = _rope_tables(positions)
    for layer in range(depth):
        pm, u_ssm, q, k, v, gates = _in_projection(
            x, norm1_g[layer], _pack_table(w_in, layer, transpose=False), cos, sin,
            pool_w[layer].astype(BF16), pool_scale[layer],
            pool_w=pool_w_cols, ssm_w=ssm_w_cols, attn_w=attn_w)
        abar_re, abar_im, bbar_re_t, bbar_im_t = _ssm_params(
            ssm_a_re[layer], ssm_a_im[layer], ssm_log_dt[layer], ssm_b_re[layer], ssm_b_im[layer])
        ys = _ssm_mixer(u_ssm, b, abar_re, abar_im, bbar_re_t, bbar_im_t,
                        ssm_c_re[layer], ssm_c_im[layer], ssm_d[layer])
        ao = _attention(q, k, v)
        packed = [_pack_table(wt, layer, transpose=False)
                  for wt in (pool_proj, ssm_glu, ssm_proj, attn_proj, w_out)]
        x, h2t = _merge(pm, ys, ao, gates, x, *packed, norm2_g[layer])
        r2, e2, n1, e1 = _peer_route(h2t, _pack_table(peer_wq, layer, transpose=True), peer_k1[layer].astype(BF16),
                                     peer_k2[layer].astype(BF16))
        x = _peer_experts(h2t, x.reshape(b * l, d), _pack_table(peer_u, layer, transpose=False),
                          _pack_table(peer_v, layer, transpose=True), r2, e2, n1, e1,
                          final_g if layer == depth - 1 else None).reshape(b, l, d)
    return x
```
